```python
import math
import jax, jax.numpy as jnp
from jax import lax
import numpy as np

D_MODEL = 1024
BATCH = 4
SEQ = 4096
DEPTH = 1

CHUNK = 64
SSD_EXPAND = 2
SSD_INNER = SSD_EXPAND * D_MODEL
SSD_HEADDIM = 64
SSD_HEADS = SSD_INNER // SSD_HEADDIM
SSD_GROUPS = 4
SSD_STATE = 128
SSD_CONV = 4
SSD_XBC = SSD_INNER + 2 * SSD_GROUPS * SSD_STATE
SSD_CHUNK = CHUNK
ATT_HEADS = 16
ATT_HEADDIM = 64
ATT_LATENT = 256
IDX_HEADS = 8
IDX_HEADDIM = 64
TOPK_MAX = 256
Q_BLOCK = 128
REL_BUCKETS = 32
REL_MAX_DIST = 128
D_FF = 2816
FFN_CONV = 3
DN_ALPHA = (2.0 * DEPTH) ** 0.25
DN_BETA = (8.0 * DEPTH) ** -0.25
LN_EPS = 1e-5

kernel_name = "hybrid_ssd_dsa_gated_deepnorm_block"


def _in_split_sizes():
    return (SSD_INNER, SSD_XBC, SSD_HEADS,
            ATT_HEADS * ATT_HEADDIM, ATT_LATENT, IDX_HEADS * IDX_HEADDIM, IDX_HEADDIM, IDX_HEADS,
            D_MODEL, D_MODEL)


def _split_points():
    pts, acc = [], 0
    for s in _in_split_sizes()[:-1]:
        acc += s
        pts.append(acc)
    return pts


def _layernorm(x, g, b):
    xf = x.astype(jnp.float32)
    mu = jnp.mean(xf, -1, keepdims=True)
    var = jnp.mean(jnp.square(xf - mu), -1, keepdims=True)
    return ((xf - mu) * lax.rsqrt(var + LN_EPS) * g + b).astype(x.dtype)


def _rmsnorm(x, g):
    xf = x.astype(jnp.float32)
    return (xf * lax.rsqrt(jnp.mean(xf * xf, -1, keepdims=True) + LN_EPS) * g).astype(x.dtype)


def _causal_dwconv(x, w, b):
    K, C = w.shape
    y = lax.conv_general_dilated(x, w[:, None, :].astype(x.dtype), window_strides=(1,),
                                 padding=[(K - 1, 0)], dimension_numbers=('NWC', 'WIO', 'NWC'),
                                 feature_group_count=C)
    return y + b


def _t5_bucket(rel):
    nb = REL_BUCKETS // 2
    max_exact = nb // 2
    bucket = jnp.where(rel > 0, nb, 0)
    n = jnp.abs(rel)
    nf = jnp.maximum(n, 1).astype(jnp.float32)
    large = max_exact + (jnp.log(nf / max_exact) / math.log(REL_MAX_DIST / max_exact)
                         * (nb - max_exact)).astype(jnp.int32)
    large = jnp.minimum(large, nb - 1)
    return bucket + jnp.where(n < max_exact, n, large)


def _ssd_scan(X, a, Bm, Cm):
    Bsz, L, H, P = X.shape
    nc = L // SSD_CHUNK
    E = H // SSD_GROUPS
    X = X.reshape(Bsz, nc, SSD_CHUNK, SSD_GROUPS, E, P)
    a = a.reshape(Bsz, nc, SSD_CHUNK, SSD_GROUPS, E).transpose(0, 3, 4, 1, 2)
    Bm = Bm.reshape(Bsz, nc, SSD_CHUNK, SSD_GROUPS, SSD_STATE)
    Cm = Cm.reshape(Bsz, nc, SSD_CHUNK, SSD_GROUPS, SSD_STATE)
    a_cum = jnp.cumsum(a.astype(jnp.float32), axis=-1)
    causal = jnp.tril(jnp.ones((SSD_CHUNK, SSD_CHUNK), dtype=bool))
    seg = a_cum[..., :, None] - a_cum[..., None, :]
    decay = jnp.exp(jnp.where(causal, seg, -jnp.inf))
    cb = jnp.einsum('bclgn,bcsgn->bgcls', Cm, Bm)
    y_diag = jnp.einsum('bgecls,bcsgep->bclgep', cb[:, :, None] * decay, X)
    decay_states = jnp.exp(a_cum[..., -1:] - a_cum).transpose(0, 3, 4, 1, 2)
    states = jnp.einsum('bclgn,bclgep->cbgepn', Bm, X * decay_states[..., None])
    chunk_decay = jnp.exp(a_cum[..., -1]).transpose(3, 0, 1, 2)

    def step(carry, inp):
        st, dec = inp
        return carry * dec[..., None, None] + st, carry

    init = jnp.zeros(states.shape[1:], states.dtype)
    _, prev = lax.scan(step, init, (states, chunk_decay))
    y_off = jnp.einsum('bclgn,cbgepn->bclgep', Cm, prev) * \
        jnp.exp(a_cum).transpose(0, 3, 4, 1, 2)[..., None]
    return (y_diag + y_off).reshape(Bsz, L, H, P)


def _ssd_branch(z, xbc, dt, conv_w, conv_b, dt_bias, A_log, D_skip, norm_g):
    Bsz, L, _ = z.shape
    xbc = jax.nn.silu(_causal_dwconv(xbc, conv_w, conv_b))
    xs, Bm, Cm = jnp.split(xbc, [SSD_INNER, SSD_INNER + SSD_GROUPS * SSD_STATE], axis=-1)
    xs = xs.reshape(Bsz, L, SSD_HEADS, SSD_HEADDIM)
    Bm = Bm.reshape(Bsz, L, SSD_GROUPS, SSD_STATE)
    Cm = Cm.reshape(Bsz, L, SSD_GROUPS, SSD_STATE)
    dt = jax.nn.softplus(dt.astype(jnp.float32) + dt_bias.astype(jnp.float32))
    A = -jnp.exp(A_log.astype(jnp.float32))
    y = _ssd_scan(xs * dt[..., None], dt * A, Bm, Cm)
    y = (y + xs * D_skip[:, None]).reshape(Bsz, L, SSD_INNER)
    v = (y * jax.nn.silu(z)).astype(jnp.float32).reshape(Bsz, L, SSD_GROUPS, SSD_INNER // SSD_GROUPS)
    v = v * lax.rsqrt(jnp.mean(v * v, -1, keepdims=True) + LN_EPS)
    return (v.reshape(Bsz, L, SSD_INNER) * norm_g).astype(z.dtype)


def _dsa_branch(q, ckv, q_idx, k_idx, w_idx, kv_norm_g, w_uk, w_uv, idxk_g, idxk_b, rel_bias):
    Bsz, L, _ = q.shape
    topk = min(TOPK_MAX, L // 4)
    nb = L // Q_BLOCK
    q = q.reshape(Bsz, L, ATT_HEADS, ATT_HEADDIM)
    ckv = _rmsnorm(ckv, kv_norm_g)
    q_lat = jnp.einsum('bthd,hcd->bthc', q, w_uk) * (ATT_HEADDIM ** -0.5)
    q_idx = q_idx.reshape(Bsz, L, IDX_HEADS, IDX_HEADDIM)
    k_idx = _layernorm(k_idx, idxk_g, idxk_b)
    w_idx = w_idx * (IDX_HEADS ** -0.5 * IDX_HEADDIM ** -0.5)
    pos = jnp.arange(L, dtype=jnp.int32)
    key_chunk = pos // CHUNK

    def blk(a):
        return a.reshape(Bsz, nb, Q_BLOCK, *a.shape[2:]).swapaxes(0, 1)

    def attend(inp):
        ql, qi, wi, qpos = inp
        qchunk = qpos // CHUNK
        s = jax.nn.relu(jnp.einsum('bthd,bsd->bths', qi, k_idx))
        score = jnp.einsum('bths,bth->bts', s, wi).astype(jnp.float32)
        admissible = key_chunk[None, :] <= qchunk[:, None]
        score = jnp.where(admissible[None], score, -jnp.inf)
        _, idx = lax.top_k(score, topk)
        valid = (idx // CHUNK) <= qchunk[None, :, None]
        kv = jax.vmap(lambda c, i: jnp.take(c, i, axis=0))(ckv, idx)
        logits = jnp.einsum('bthc,btkc->bthk', ql, kv).astype(jnp.float32)
        bias = jnp.take(rel_bias, _t5_bucket(idx - qpos[None, :, None]), axis=0)
        logits = logits + bias.astype(jnp.float32).transpose(0, 1, 3, 2)
        logits = jnp.where(valid[:, :, None, :], logits, -jnp.inf)
        p = jax.nn.softmax(logits, axis=-1).astype(kv.dtype)
        o_lat = jnp.einsum('bthk,btkc->bthc', p, kv)
        o = jnp.einsum('bthc,hcd->bthd', o_lat, w_uv)
        return o.reshape(o.shape[0], Q_BLOCK, ATT_HEADS * ATT_HEADDIM)

    out = lax.map(attend, (blk(q_lat), blk(q_idx), blk(w_idx), pos.reshape(nb, Q_BLOCK)))
    return out.swapaxes(0, 1).reshape(Bsz, L, ATT_HEADS * ATT_HEADDIM)


def _mixing(h, w_in, conv_w, conv_b, dt_bias, A_log, D_skip, ssd_norm_g,
            kv_norm_g, w_uk, w_uv, idxk_g, idxk_b, rel_bias, w_br_ssd, w_br_att, w_out):
    proj = h @ w_in
    z, xbc, dt, q, ckv, q_idx, k_idx, w_idx, g_ssd, g_att = jnp.split(proj, _split_points(), axis=-1)
    y_ssd = _ssd_branch(z, xbc, dt, conv_w, conv_b, dt_bias, A_log, D_skip, ssd_norm_g)
    y_att = _dsa_branch(q, ckv, q_idx, k_idx, w_idx, kv_norm_g, w_uk, w_uv, idxk_g, idxk_b, rel_bias)
    m = jax.nn.sigmoid(g_ssd) * (y_ssd @ w_br_ssd) + jax.nn.sigmoid(g_att) * (y_att @ w_br_att)
    return m @ w_out


def _conv_ffn(h, w_up, conv_w, conv_b, w_down):
    u = _causal_dwconv(h @ w_up, conv_w, conv_b)
    a, v = jnp.split(u, 2, axis=-1)
    return (jax.nn.silu(a) * v) @ w_down


def setup_inputs(seed: int = 0) -> dict:
    key = jax.random.key(seed)
    ks = iter(jax.random.split(key, 40))

    def nrm(shape, scale):
        return scale * jax.random.normal(next(ks), shape, jnp.float32)

    def gain(shape):
        return 1.0 + nrm(shape, 0.02)

    n_in = sum(_in_split_sizes())
    dt0 = jnp.exp(jax.random.uniform(next(ks), (DEPTH, SSD_HEADS), jnp.float32)
                  * (math.log(0.1) - math.log(0.001)) + math.log(0.001))
    dt_bias = dt0 + jnp.log(-jnp.expm1(-dt0))
    A_log = jnp.log(jax.random.uniform(next(ks), (DEPTH, SSD_HEADS), jnp.float32, 1.0, 16.0))
    return {
        "x": nrm((BATCH, SEQ, D_MODEL), 1.0),
        "ln_in_g": gain((D_MODEL,)),
        "ln_in_b": nrm((D_MODEL,), 0.02),
        "w_in": nrm((DEPTH, D_MODEL, n_in), D_MODEL ** -0.5),
        "ssd_conv_w": nrm((DEPTH, SSD_CONV, SSD_XBC), SSD_CONV ** -0.5),
        "ssd_conv_b": nrm((DEPTH, SSD_XBC), 0.01),
        "ssd_dt_bias": dt_bias,
        "ssd_A_log": A_log,
        "ssd_D": 1.0 + nrm((DEPTH, SSD_HEADS), 0.1),
        "ssd_norm_g": gain((DEPTH, SSD_INNER)),
        "att_kv_norm_g": gain((DEPTH, ATT_LATENT)),
        "att_w_uk": nrm((DEPTH, ATT_HEADS, ATT_LATENT, ATT_HEADDIM), ATT_LATENT ** -0.5),
        "att_w_uv": nrm((DEPTH, ATT_HEADS, ATT_LATENT, ATT_HEADDIM), ATT_LATENT ** -0.5 * DN_BETA),
        "idx_k_norm_g": gain((DEPTH, IDX_HEADDIM)),
        "idx_k_norm_b": nrm((DEPTH, IDX_HEADDIM), 0.02),
        "rel_bias": nrm((REL_BUCKETS, ATT_HEADS), 0.2),
        "w_br_ssd": nrm((DEPTH, SSD_INNER, D_MODEL), SSD_INNER ** -0.5 * DN_BETA),
        "w_br_att": nrm((DEPTH, ATT_HEADS * ATT_HEADDIM, D_MODEL), (ATT_HEADS * ATT_HEADDIM) ** -0.5 * DN_BETA),
        "w_out": nrm((DEPTH, D_MODEL, D_MODEL), D_MODEL ** -0.5 * DN_BETA),
        "ln1_g": gain((DEPTH, D_MODEL)),
        "ln1_b": nrm((DEPTH, D_MODEL), 0.02),
        "ffn_w_up": nrm((DEPTH, D_MODEL, 2 * D_FF), D_MODEL ** -0.5 * DN_BETA),
        "ffn_conv_w": nrm((DEPTH, FFN_CONV, 2 * D_FF), FFN_CONV ** -0.5),
        "ffn_conv_b": nrm((DEPTH, 2 * D_FF), 0.01),
        "ffn_w_down": nrm((DEPTH, D_FF, D_MODEL), D_FF ** -0.5 * DN_BETA),
        "ln2_g": gain((DEPTH, D_MODEL)),
        "ln2_b": nrm((DEPTH, D_MODEL), 0.02),
    }


def reference(x, ln_in_g, ln_in_b, w_in, ssd_conv_w, ssd_conv_b, ssd_dt_bias, ssd_A_log, ssd_D,
              ssd_norm_g, att_kv_norm_g, att_w_uk, att_w_uv, idx_k_norm_g, idx_k_norm_b, rel_bias,
              w_br_ssd, w_br_att, w_out, ln1_g, ln1_b, ffn_w_up, ffn_conv_w, ffn_conv_b, ffn_w_down,
              ln2_g, ln2_b):
    h = _layernorm(x, ln_in_g, ln_in_b)
    for l in range(DEPTH):
        mix = _mixing(h, w_in[l], ssd_conv_w[l], ssd_conv_b[l], ssd_dt_bias[l], ssd_A_log[l], ssd_D[l],
                      ssd_norm_g[l], att_kv_norm_g[l], att_w_uk[l], att_w_uv[l], idx_k_norm_g[l],
                      idx_k_norm_b[l], rel_bias, w_br_ssd[l], w_br_att[l], w_out[l])
        h = _layernorm(DN_ALPHA * h + mix.astype(h.dtype), ln1_g[l], ln1_b[l])
        f = _conv_ffn(h, ffn_w_up[l], ffn_conv_w[l], ffn_conv_b[l], ffn_w_down[l])
        h = _layernorm(DN_ALPHA * h + f.astype(h.dtype), ln2_g[l], ln2_b[l])
    return h.astype(x.dtype)
```

```python
import functools
import math

import numpy as np
import jax
import jax.numpy as jnp
from jax import lax
from jax.experimental import pallas as pl
from jax.experimental.pallas import tpu as pltpu

F32 = jnp.float32
BF16 = jnp.bfloat16

D_MODEL = 1024
CHUNK = 64
SSD_INNER = 2048
SSD_HEADDIM = 64
SSD_HEADS = 32
SSD_GROUPS = 4
SSD_STATE = 128
SSD_CONV = 4
SSD_XBC = SSD_INNER + 2 * SSD_GROUPS * SSD_STATE
ATT_HEADS = 16
ATT_HEADDIM = 64
ATT_LATENT = 256
IDX_HEADS = 8
IDX_HEADDIM = 64
TOPK_MAX = 256
REL_BUCKETS = 32
REL_MAX_DIST = 128
D_FF = 2816
FFN_CONV = 3
DN_ALPHA = 2.0 ** 0.25
LN_EPS = 1e-5

LANES = 128
SUBLANES = 8
VMEM_LIMIT = 56 * 1024 * 1024

SSD_Q = 128
ATT_QB = 256
ATT_TK = 256
INT_MIN = -2 ** 31
NEG_BIG = -1e30
HI = lax.Precision.HIGHEST


def _cparams(*sem):
    return pltpu.CompilerParams(dimension_semantics=sem, vmem_limit_bytes=VMEM_LIMIT)


def _const_spec(shape):
    nd = len(shape)
    return pl.BlockSpec(shape, lambda *_: (0,) * nd, pipeline_mode=pl.Buffered(1))


def _silu(x):
    return x / (1.0 + jnp.exp(-x))


def _softplus(x):
    return jnp.maximum(x, 0.0) + jnp.log1p(jnp.exp(-jnp.abs(x)))


def _ln_rows(x, g, b):
    mu = jnp.mean(x, -1, keepdims=True)
    xc = x - mu
    var = jnp.mean(xc * xc, -1, keepdims=True)
    return xc * lax.rsqrt(var + LN_EPS) * g + b


def _dot(a, b, **kw):
    return jnp.dot(a, b, preferred_element_type=F32, **kw)


def _dot_nt(a, b):
    return lax.dot_general(a, b, (((1,), (1,)), ((), ())), preferred_element_type=F32)


def _dot_tn(a, b):
    return lax.dot_general(a, b, (((0,), (0,)), ((), ())), preferred_element_type=F32)


def _ln_kernel(x_ref, g_ref, b_ref, hf_ref, hb_ref):
    h = _ln_rows(x_ref[...], g_ref[...], b_ref[...])
    hf_ref[...] = h
    hb_ref[...] = h.astype(BF16)


def _layernorm_in(x2, g, b, tm=512):
    T, D = x2.shape
    row = pl.BlockSpec((tm, D), lambda i: (i, 0))
    return pl.pallas_call(
        _ln_kernel,
        out_shape=(jax.ShapeDtypeStruct((T, D), F32), jax.ShapeDtypeStruct((T, D), BF16)),
        grid=(T // tm,),
        in_specs=[row, _const_spec((1, D)), _const_spec((1, D))],
        out_specs=(row, row),
        compiler_params=_cparams("parallel"),
    )(x2, g.reshape(1, D), b.reshape(1, D))


def _mm_kernel(a_ref, w_ref, o_ref):
    o_ref[...] = _dot(a_ref[...], w_ref[...]).astype(o_ref.dtype)


def _matmul(a, w, tn, tm=512, out_dtype=F32):
    M, K = a.shape
    N = w.shape[1]
    assert M % tm == 0 and N % tn == 0
    return pl.pallas_call(
        _mm_kernel,
        out_shape=jax.ShapeDtypeStruct((M, N), out_dtype),
        grid=(N // tn, M // tm),
        in_specs=[pl.BlockSpec((tm, K), lambda j, i: (i, 0)),
                  pl.BlockSpec((K, tn), lambda j, i: (0, j))],
        out_specs=pl.BlockSpec((tm, tn), lambda j, i: (i, j)),
        compiler_params=_cparams("parallel", "parallel"),
    )(a, w)


def _ssd_kernel(xbc_ref, z0_ref, z1_ref, dt_ref, convw_ref, convb_ref, dtb_ref, dtbT_ref,
                a_ref, aT_ref, dskip_ref, ng_ref, e_ref, y_ref, xpad, state, ybuf):
    c = pl.program_id(1)
    Q = xbc_ref.shape[0]
    G, N, P = SSD_GROUPS, SSD_STATE, SSD_HEADDIM
    GW = SSD_INNER // G
    HALO = SUBLANES

    @pl.when(c == 0)
    def _():
        xpad[0:HALO, :] = jnp.zeros((HALO, SSD_XBC), F32)
        state[...] = jnp.zeros(state.shape, F32)

    @pl.when(c > 0)
    def _():
        xpad[0:HALO, :] = xpad[Q:Q + HALO, :]

    xpad[HALO:HALO + Q, :] = xbc_ref[...]
    conv = convb_ref[...]
    for k in range(SSD_CONV):
        conv = conv + convw_ref[k:k + 1, :] * xpad[pl.ds(HALO - (SSD_CONV - 1) + k, Q), :]
    xc = _silu(conv)
    xs = xc[:, :SSD_INNER]

    dt_blk = dt_ref[...]
    dt = _softplus(dt_blk[:, :SSD_HEADS] + dtb_ref[...])
    a = dt * a_ref[...]
    dtT = _softplus(dt_blk.T[:SSD_HEADS, :] + dtbT_ref[...])
    aT = dtT * aT_ref[...]

    row = lax.broadcasted_iota(jnp.int32, (Q, Q), 0)
    col = lax.broadcasted_iota(jnp.int32, (Q, Q), 1)
    causal = row >= col
    tril = jnp.where(causal, 1.0, 0.0).astype(F32)
    triu = jnp.where(row <= col, 1.0, 0.0).astype(F32)
    acum = _dot(tril, a, precision=HI)
    acumT = _dot(aT, triu, precision=HI)

    e = e_ref[...]
    dt_e = _dot(dt, e, precision=HI)
    eac_e = _dot(jnp.exp(acum), e, precision=HI)
    ds_e = _dot(jnp.exp(acum[Q - 1:Q, :] - acum), e, precision=HI)
    X = xs * dt_e
    Xb = X.astype(BF16)
    Xd = (X * ds_e).astype(BF16)
    lane = lax.broadcasted_iota(jnp.int32, (Q, LANES), 1)
    lo_half = lane < P

    for g in range(G):
        Bg = xc[:, SSD_INNER + g * N:SSD_INNER + (g + 1) * N].astype(BF16)
        Cg = xc[:, SSD_INNER + G * N + g * N:SSD_INNER + G * N + (g + 1) * N].astype(BF16)
        cb = _dot_nt(Cg, Bg)
        st = state[g]
        eg = eac_e[:, g * GW:(g + 1) * GW]
        yoff = _dot(Cg, st.astype(BF16)) * eg
        new = _dot_tn(Bg, Xd[:, g * GW:(g + 1) * GW])
        state[g] = st * eg[Q - 1:Q, :] + new
        for jj in range(GW // LANES):
            p = g * (GW // LANES) + jj
            h0 = 2 * p
            Xp = Xb[:, p * LANES:(p + 1) * LANES]
            zero = jnp.zeros_like(Xp)
            top = jnp.where(lo_half, Xp, zero)
            bot = jnp.where(lo_half, zero, Xp)
            yd = None
            for hh, rhs in ((h0, top), (h0 + 1, bot)):
                seg = acum[:, hh:hh + 1] - acumT[hh:hh + 1, :]
                decay = jnp.exp(jnp.where(causal, seg, -jnp.inf))
                part = _dot((cb * decay).astype(BF16), rhs)
                yd = part if yd is None else yd + part
            ybuf[:, p * LANES:(p + 1) * LANES] = yd + yoff[:, jj * LANES:(jj + 1) * LANES]

    y = ybuf[...] + xs * dskip_ref[...]
    for g in range(G):
        zg = (z0_ref if g < G // 2 else z1_ref)[:, (g % (G // 2)) * GW:(g % (G // 2) + 1) * GW]
        v = y[:, g * GW:(g + 1) * GW] * _silu(zg)
        v = v * lax.rsqrt(jnp.mean(v * v, -1, keepdims=True) + LN_EPS)
        y_ref[:, g * GW:(g + 1) * GW] = (v * ng_ref[:, g * GW:(g + 1) * GW]).astype(BF16)


def _ssd_branch(xz, small, conv_w, conv_b, dt_bias, A_log, D_skip, norm_g, B, L):
    T = B * L
    Q = SSD_Q
    nc = L // Q
    H = SSD_HEADS
    A = -jnp.exp(A_log.astype(F32))
    expand = jnp.asarray(np.kron(np.eye(H, dtype=np.float32), np.ones((1, SSD_HEADDIM), np.float32)))
    d_e = jnp.repeat(D_skip.astype(F32), SSD_HEADDIM).reshape(1, SSD_INNER)
    zw = SSD_INNER // 2
    tok = lambda w, j: pl.BlockSpec((Q, w), lambda b, c, j=j: (b * nc + c, j))
    return pl.pallas_call(
        _ssd_kernel,
        out_shape=jax.ShapeDtypeStruct((T, SSD_INNER), BF16),
        grid=(B, nc),
        in_specs=[tok(SSD_XBC, 0), tok(zw, SSD_XBC // zw), tok(zw, SSD_XBC // zw + 1), tok(LANES, 6),
                  _const_spec((SSD_CONV, SSD_XBC)), _const_spec((1, SSD_XBC)),
                  _const_spec((1, H)), _const_spec((H, 1)), _const_spec((1, H)), _const_spec((H, 1)),
                  _const_spec((1, SSD_INNER)), _const_spec((1, SSD_INNER)), _const_spec((H, SSD_INNER))],
        out_specs=pl.BlockSpec((Q, SSD_INNER), lambda b, c: (b * nc + c, 0)),
        scratch_shapes=[pltpu.VMEM((Q + 2 * SUBLANES, SSD_XBC), F32),
                        pltpu.VMEM((SSD_GROUPS, SSD_STATE, SSD_INNER // SSD_GROUPS), F32),
                        pltpu.VMEM((Q, SSD_INNER), F32)],
        compiler_params=_cparams("parallel", "arbitrary"),
    )(xz, xz, xz, small, conv_w, conv_b.reshape(1, -1), dt_bias.reshape(1, H), dt_bias.reshape(H, 1),
      A.reshape(1, H), A.reshape(H, 1), d_e, norm_g.reshape(1, -1), expand)


def _prep_kernel(ckv_ref, kidx_ref, kvg_ref, ig_ref, ib_ref, ckvn_ref, kk_ref):
    c = ckv_ref[...]
    cn = c * lax.rsqrt(jnp.mean(c * c, -1, keepdims=True) + LN_EPS) * kvg_ref[...]
    ckvn_ref[...] = cn.astype(BF16)
    k = kidx_ref[...]
    lane = lax.broadcasted_iota(jnp.int32, k.shape, 1)
    live = lane < IDX_HEADDIM
    mu = jnp.sum(k, -1, keepdims=True) * (1.0 / IDX_HEADDIM)
    kc = jnp.where(live, k - mu, 0.0)
    var = jnp.sum(kc * kc, -1, keepdims=True) * (1.0 / IDX_HEADDIM)
    kn = jnp.where(live, kc * lax.rsqrt(var + LN_EPS) * ig_ref[...] + ib_ref[...], 0.0)
    kk_ref[:, :LANES] = kn.astype(BF16)
    kk_ref[:, LANES:] = pltpu.roll(kn, IDX_HEADDIM, 1).astype(BF16)


def _dsa_prep(qc, small, kv_g, idx_g, idx_b, tm=512):
    T = qc.shape[0]
    pad = lambda v: jnp.pad(v.astype(F32), (0, LANES - IDX_HEADDIM)).reshape(1, LANES)
    return pl.pallas_call(
        _prep_kernel,
        out_shape=(jax.ShapeDtypeStruct((T, ATT_LATENT), BF16),
                   jax.ShapeDtypeStruct((T, 2 * LANES), BF16)),
        grid=(T // tm,),
        in_specs=[pl.BlockSpec((tm, ATT_LATENT), lambda i: (i, 4)),
                  pl.BlockSpec((tm, LANES), lambda i: (i, 4)),
                  _const_spec((1, ATT_LATENT)), _const_spec((1, LANES)), _const_spec((1, LANES))],
        out_specs=(pl.BlockSpec((tm, ATT_LATENT), lambda i: (i, 0)),
                   pl.BlockSpec((tm, 2 * LANES), lambda i: (i, 0))),
        compiler_params=_cparams("parallel"),
    )(qc, small, kv_g.reshape(1, -1), pad(idx_g), pad(idx_b))


def _attn_kernel(cbias_ref, q_ref, qidx_ref, widx_ref, ckv_ref, kk_ref, bias_ref, wuk_ref, wuv_ref,
                 o_ref, keys, qlat, m_scr, l_scr, acc, *, topk):
    i = pl.program_id(1)
    QB, TK, NH, C = ATT_QB, ATT_TK, ATT_HEADS, ATT_LATENT
    n_tiles = i + 1

    for p in range(NH // 2):
        qp = q_ref[:, p * LANES:(p + 1) * LANES].astype(BF16)
        ql = _dot(qp, wuk_ref[p]) * (ATT_HEADDIM ** -0.5)
        qlat[(2 * p) * QB:(2 * p + 1) * QB, :] = ql[:, :C].astype(BF16)
        qlat[(2 * p + 1) * QB:(2 * p + 2) * QB, :] = ql[:, C:].astype(BF16)

    w = widx_ref[...] * (IDX_HEADS ** -0.5 * IDX_HEADDIM ** -0.5)
    qpos = i * QB + lax.broadcasted_iota(jnp.int32, (QB, TK), 0)
    kin = lax.broadcasted_iota(jnp.int32, (QB, TK), 1)

    def score_tile(j, _):
        kk = kk_ref[pl.ds(pl.multiple_of(j * TK, TK), TK), :]
        s = jnp.zeros((QB, TK), F32)
        for p in range(IDX_HEADS // 2):
            qp = qidx_ref[:, p * LANES:(p + 1) * LANES].astype(BF16)
            se = _dot_nt(qp, kk[:, :LANES])
            so = _dot_nt(qp, kk[:, LANES:])
            s = s + w[:, 2 * p:2 * p + 1] * jnp.maximum(se, 0.0)
            s = s + w[:, 2 * p + 1:2 * p + 2] * jnp.maximum(so, 0.0)
        bits = pltpu.bitcast(s, jnp.int32)
        skey = bits ^ ((bits >> 31) & 0x7FFFFFFF)
        adm = ((j * TK + kin) // CHUNK) <= (qpos // CHUNK)
        keys[j] = jnp.where(adm, skey, INT_MIN)
        return 0

    lax.fori_loop(0, n_tiles, score_tile, 0)

    def count_ge(cand):
        def body(j, cnt):
            k = keys[j]
            hit = jnp.where(k >= cand, 1, 0)
            return cnt + hit[:, :LANES] + hit[:, LANES:]
        cnt = lax.fori_loop(0, n_tiles, body, jnp.zeros((QB, LANES), jnp.int32))
        return jnp.sum(cnt, axis=1, keepdims=True)

    tau0 = jnp.full((QB, 1), INT_MIN, jnp.int32)
    tau0 = jnp.where(count_ge(jnp.zeros((QB, 1), jnp.int32)) >= topk, 0, tau0)

    def bit_step(t, tau):
        cand = tau + jnp.left_shift(jnp.int32(1), 30 - t)
        return jnp.where(count_ge(cand) >= topk, cand, tau)

    tau = lax.fori_loop(0, 31, bit_step, tau0)
    tau = jnp.maximum(tau, INT_MIN + 1)

    m_scr[...] = jnp.full(m_scr.shape, NEG_BIG, F32)
    l_scr[...] = jnp.zeros(l_scr.shape, F32)
    acc[...] = jnp.zeros(acc.shape, F32)

    def attend(j, bias_of_head):
        kv = ckv_ref[pl.ds(pl.multiple_of(j * TK, TK), TK), :]
        sel = keys[j] >= tau
        for h in range(NH):
            rows = slice(h * QB, (h + 1) * QB)
            s = _dot_nt(qlat[rows, :], kv) + bias_of_head(h)
            s = jnp.where(sel, s, NEG_BIG)
            m_old = m_scr[rows, :]
            m_new = jnp.maximum(m_old, jnp.max(s, axis=1, keepdims=True))
            alpha = jnp.exp(m_old - m_new)
            pr = jnp.exp(s - m_new)
            l_scr[rows, :] = alpha * l_scr[rows, :] + jnp.sum(pr, axis=1, keepdims=True)
            acc[rows, :] = alpha * acc[rows, :] + _dot(pr.astype(BF16), kv)
            m_scr[rows, :] = m_new

    def far_tile(j, _):
        attend(j, lambda h: cbias_ref[h])
        return 0

    lax.fori_loop(0, i - 1, far_tile, 0)

    @pl.when(i > 0)
    def _():
        attend(i - 1, lambda h: bias_ref[h, :, :TK])

    attend(i, lambda h: bias_ref[h, :, TK:])

    for p in range(NH // 2):
        o = None
        for h in (2 * p, 2 * p + 1):
            rows = slice(h * QB, (h + 1) * QB)
            ol = (acc[rows, :] / l_scr[rows, :]).astype(BF16)
            part = _dot(ol, wuv_ref[h])
            o = part if o is None else o + part
        o_ref[:, p * LANES:(p + 1) * LANES] = o.astype(o_ref.dtype)


def _t5_bucket_np(rel):
    nb = REL_BUCKETS // 2
    max_exact = nb // 2
    n = np.abs(rel).astype(np.int64)
    nn = np.maximum(n, 1)
    sq = nn * nn
    log2_sq = np.floor(np.log2(sq.astype(np.float64))).astype(np.int64)
    log2_sq = np.where(2 ** (log2_sq + 1) <= sq, log2_sq + 1, log2_sq)
    log2_sq = np.where(2 ** log2_sq > sq, log2_sq - 1, log2_sq)
    large = np.minimum(max_exact + (log2_sq - 6), nb - 1)
    return np.where(rel > 0, nb, 0) + np.where(n < max_exact, n, large)


def _dsa_attention(qc, small, ckv_n, kk, w_uk, w_uv, rel_bias, B, L):
    T = B * L
    QB, TK, NH, C, DH = ATT_QB, ATT_TK, ATT_HEADS, ATT_LATENT, ATT_HEADDIM
    nq = L // QB
    topk = min(TOPK_MAX, L // 4)
    rel = (np.arange(2 * TK)[None, :] - TK) - np.arange(QB)[:, None]
    bias_near = jnp.transpose(jnp.take(rel_bias.astype(F32), jnp.asarray(_t5_bucket_np(rel)), axis=0), (2, 0, 1))
    far_bucket = int(_t5_bucket_np(np.array([-(TK + 1)]))[0])
    cbias = rel_bias[far_bucket].astype(F32)
    uk = jnp.transpose(w_uk.astype(F32), (0, 2, 1))
    wuk_bd = jnp.zeros((NH // 2, 2 * DH, 2 * C), F32)
    wuk_bd = wuk_bd.at[:, :DH, :C].set(uk[0::2]).at[:, DH:, C:].set(uk[1::2]).astype(BF16)
    wuv_pad = jnp.zeros((NH, C, 2 * DH), F32)
    wuv_pad = wuv_pad.at[0::2, :, :DH].set(w_uv[0::2]).at[1::2, :, DH:].set(w_uv[1::2]).astype(BF16)

    grid_spec = pltpu.PrefetchScalarGridSpec(
        num_scalar_prefetch=0,
        grid=(B, nq),
        in_specs=[
            pl.BlockSpec(memory_space=pltpu.SMEM),
            pl.BlockSpec((QB, NH * DH), lambda b, i: (b * nq + i, 0)),
            pl.BlockSpec((QB, IDX_HEADS * IDX_HEADDIM), lambda b, i: (b * nq + i, 0)),
            pl.BlockSpec((QB, LANES), lambda b, i: (b * nq + i, 5)),
            pl.BlockSpec((L, C), lambda b, i: (b, 0)),
            pl.BlockSpec((L, 2 * LANES), lambda b, i: (b, 0)),
            _const_spec((NH, QB, 2 * TK)),
            _const_spec((NH // 2, 2 * DH, 2 * C)),
            _const_spec((NH, C, 2 * DH)),
        ],
        out_specs=pl.BlockSpec((QB, NH * DH), lambda b, i: (b * nq + i, 0)),
        scratch_shapes=[pltpu.VMEM((L // TK, QB, TK), jnp.int32),
                        pltpu.VMEM((NH * QB, C), BF16),
                        pltpu.VMEM((NH * QB, 1), F32),
                        pltpu.VMEM((NH * QB, 1), F32),
                        pltpu.VMEM((NH * QB, C), F32)],
    )
    return pl.pallas_call(
        functools.partial(_attn_kernel, topk=topk),
        out_shape=jax.ShapeDtypeStruct((T, NH * DH), BF16),
        grid_spec=grid_spec,
        compiler_params=_cparams("parallel", "arbitrary"),
    )(cbias, qc, small, small, ckv_n, kk, bias_near, wuk_bd, wuv_pad)


def _merge_kernel(ys_ref, ya_ref, gs_ref, ga_ref, h_ref, wbs_ref, wba_ref, wo_ref, g_ref, b_ref,
                  hf_ref, hb_ref):
    sig = lambda v: 1.0 / (1.0 + jnp.exp(-v))
    m = sig(gs_ref[...]) * _dot(ys_ref[...], wbs_ref[...]) + sig(ga_ref[...]) * _dot(ya_ref[...], wba_ref[...])
    mix = _dot(m.astype(BF16), wo_ref[...])
    h1 = _ln_rows(DN_ALPHA * h_ref[...] + mix, g_ref[...], b_ref[...])
    hf_ref[...] = h1
    hb_ref[...] = h1.astype(BF16)


def _merge(y_ssd, y_att, gates, h, w_br_ssd, w_br_att, w_out, g, b, tm=256):
    T, D = h.shape
    row = lambda w, j=0: pl.BlockSpec((tm, w), lambda i, j=j: (i, j))
    return pl.pallas_call(
        _merge_kernel,
        out_shape=(jax.ShapeDtypeStruct((T, D), F32), jax.ShapeDtypeStruct((T, D), BF16)),
        grid=(T // tm,),
        in_specs=[row(SSD_INNER), row(ATT_HEADS * ATT_HEADDIM), row(D, 0), row(D, 1), row(D),
                  _const_spec(w_br_ssd.shape), _const_spec(w_br_att.shape), _const_spec(w_out.shape),
                  _const_spec((1, D)), _const_spec((1, D))],
        out_specs=(row(D), row(D)),
        compiler_params=_cparams("parallel"),
    )(y_ssd, y_att, gates, gates, h, w_br_ssd, w_br_att, w_out, g.reshape(1, D), b.reshape(1, D))


FFN_COLS = 1408


def _ffn_up_kernel(h_ref, wup_ref, cw_ref, cb_ref, g_ref, upad, tail):
    c = pl.program_id(1)
    tm = h_ref.shape[0]
    HALO = SUBLANES

    @pl.when(c == 0)
    def _():
        tail[...] = jnp.zeros(tail.shape, F32)

    hb = h_ref[...]
    for cc in range(D_FF // FFN_COLS):
        halves = []
        for half in range(2):
            cols = slice(half * D_FF + cc * FFN_COLS, half * D_FF + (cc + 1) * FFN_COLS)
            upad[0:HALO, :] = tail[:, cols]
            upad[HALO:HALO + tm, :] = _dot(hb, wup_ref[:, cols])
            tail[:, cols] = upad[tm:tm + HALO, :]
            u = cb_ref[:, cols]
            for k in range(FFN_CONV):
                u = u + cw_ref[k:k + 1, cols] * upad[pl.ds(HALO - (FFN_CONV - 1) + k, tm), :]
            halves.append(u)
        g_ref[:, cc * FFN_COLS:(cc + 1) * FFN_COLS] = (_silu(halves[0]) * halves[1]).astype(BF16)


def _ffn_up(h1b, w_up, conv_w, conv_b, B, L, tm=256):
    T, D = h1b.shape
    nt = L // tm
    return pl.pallas_call(
        _ffn_up_kernel,
        out_shape=jax.ShapeDtypeStruct((T, D_FF), BF16),
        grid=(B, nt),
        in_specs=[pl.BlockSpec((tm, D), lambda b, c: (b * nt + c, 0)),
                  _const_spec(w_up.shape), _const_spec(conv_w.shape), _const_spec((1, 2 * D_FF))],
        out_specs=pl.BlockSpec((tm, D_FF), lambda b, c: (b * nt + c, 0)),
        scratch_shapes=[pltpu.VMEM((tm + SUBLANES, FFN_COLS), F32),
                        pltpu.VMEM((SUBLANES, 2 * D_FF), F32)],
        compiler_params=_cparams("parallel", "arbitrary"),
    )(h1b, w_up, conv_w, conv_b.reshape(1, -1))


def _ffn_down_kernel(g_ref, h_ref, wd_ref, lg_ref, lb_ref, o_ref):
    f = _dot(g_ref[...], wd_ref[...])
    o_ref[...] = _ln_rows(DN_ALPHA * h_ref[...] + f, lg_ref[...], lb_ref[...])


def _ffn_down(g, h1, w_down, lg, lb, tm=256):
    T, D = h1.shape
    return pl.pallas_call(
        _ffn_down_kernel,
        out_shape=jax.ShapeDtypeStruct((T, D), F32),
        grid=(T // tm,),
        in_specs=[pl.BlockSpec((tm, D_FF), lambda i: (i, 0)), pl.BlockSpec((tm, D), lambda i: (i, 0)),
                  _const_spec(w_down.shape), _const_spec((1, D)), _const_spec((1, D))],
        out_specs=pl.BlockSpec((tm, D), lambda i: (i, 0)),
        compiler_params=_cparams("parallel"),
    )(g, h1, w_down, lg.reshape(1, D), lb.reshape(1, D))


def _split_w_in(w_in):
    sizes = (SSD_INNER, SSD_XBC, SSD_HEADS, ATT_HEADS * ATT_HEADDIM, ATT_LATENT,
             IDX_HEADS * IDX_HEADDIM, IDX_HEADDIM, IDX_HEADS, D_MODEL, D_MODEL)
    pts = np.cumsum(sizes)[:-1].tolist()
    return jnp.split(w_in, pts, axis=1)


def _pad_cols(w, width):
    return jnp.pad(w, ((0, 0), (0, width - w.shape[1])))


def kernel(x, ln_in_g, ln_in_b, w_in, ssd_conv_w, ssd_conv_b, ssd_dt_bias, ssd_A_log, ssd_D, ssd_norm_g, att_kv_norm_g, att_w_uk, att_w_uv, idx_k_norm_g, idx_k_norm_b, rel_bias, w_br_ssd, w_br_att, w_out, ln1_g, ln1_b, ffn_w_up, ffn_conv_w, ffn_conv_b, ffn_w_down, ln2_g, ln2_b):
    B, L, D = x.shape
    T = B * L
    l = 0
    wz, wxbc, wdt, wq, wckv, wqi, wki, wwi, wgs, wga = _split_w_in(w_in[l])
    w_xz = jnp.concatenate([wxbc, wz], axis=1).astype(BF16)
    w_qc = jnp.concatenate([wq, wckv], axis=1).astype(BF16)
    w_g = jnp.concatenate([wgs, wga], axis=1).astype(BF16)
    w_small = jnp.concatenate([wqi, _pad_cols(wki, LANES), _pad_cols(wwi, LANES), _pad_cols(wdt, LANES)],
                              axis=1).astype(BF16)

    h, hb = _layernorm_in(x.reshape(T, D), ln_in_g, ln_in_b)
    xz = _matmul(hb, w_xz, tn=1024)
    qc = _matmul(hb, w_qc, tn=1280)
    gates = _matmul(hb, w_g, tn=1024)
    small = _matmul(hb, w_small, tn=896)

    y_ssd = _ssd_branch(xz, small, ssd_conv_w[l], ssd_conv_b[l], ssd_dt_bias[l], ssd_A_log[l], ssd_D[l],
                        ssd_norm_g[l], B, L)
    ckv_n, kk = _dsa_prep(qc, small, att_kv_norm_g[l], idx_k_norm_g[l], idx_k_norm_b[l])
    y_att = _dsa_attention(qc, small, ckv_n, kk, att_w_uk[l], att_w_uv[l], rel_bias, B, L)

    h1, h1b = _merge(y_ssd, y_att, gates, h, w_br_ssd[l].astype(BF16), w_br_att[l].astype(BF16),
                     w_out[l].astype(BF16), ln1_g[l], ln1_b[l])
    g = _ffn_up(h1b, ffn_w_up[l].astype(BF16), ffn_conv_w[l], ffn_conv_b[l], B, L)
    out = _ffn_down(g, h1, ffn_w_down[l].astype(BF16), ln2_g[l], ln2_b[l])
    return out.reshape(B, L, D).astype(x.dtype)
```

```python
import functools
import math

import numpy as np
import jax
import jax.numpy as jnp
from jax import lax
from jax.experimental import pallas as pl
from jax.experimental.pallas import tpu as pltpu

F32 = jnp.float32
BF16 = jnp.bfloat16

D_MODEL = 1024
CHUNK = 64
SSD_INNER = 2048
SSD_HEADDIM = 64
SSD_HEADS = 32
SSD_GROUPS = 4
SSD_STATE = 128
SSD_CONV = 4
SSD_XBC = SSD_INNER + 2 * SSD_GROUPS * SSD_STATE
ATT_HEADS = 16
ATT_HEADDIM = 64
ATT_LATENT = 256
IDX_HEADS = 8
IDX_HEADDIM = 64
TOPK_MAX = 256
REL_BUCKETS = 32
REL_MAX_DIST = 128
D_FF = 2816
FFN_CONV = 3
DN_ALPHA = 2.0 ** 0.25
LN_EPS = 1e-5

LANES = 128
SUBLANES = 8
VMEM_LIMIT = 56 * 1024 * 1024

SSD_Q = 128
ATT_QB = 256
ATT_TK = 256
INT_MIN = -2 ** 31
NEG_BIG = -1e30
LOG2E = math.log2(math.e)
HI = lax.Precision.HIGHEST


def _cparams(*sem):
    return pltpu.CompilerParams(dimension_semantics=sem, vmem_limit_bytes=VMEM_LIMIT)


def _const_spec(shape):
    nd = len(shape)
    return pl.BlockSpec(shape, lambda *_: (0,) * nd, pipeline_mode=pl.Buffered(1))


def _silu(x):
    return x / (1.0 + jnp.exp(-x))


def _softplus(x):
    return jnp.maximum(x, 0.0) + jnp.log1p(jnp.exp(-jnp.abs(x)))


def _ln_rows(x, g, b):
    mu = jnp.mean(x, -1, keepdims=True)
    xc = x - mu
    var = jnp.mean(xc * xc, -1, keepdims=True)
    return xc * lax.rsqrt(var + LN_EPS) * g + b


def _dot(a, b, **kw):
    return jnp.dot(a, b, preferred_element_type=F32, **kw)


def _dot_nt(a, b):
    return lax.dot_general(a, b, (((1,), (1,)), ((), ())), preferred_element_type=F32)


def _dot_tn(a, b):
    return lax.dot_general(a, b, (((0,), (0,)), ((), ())), preferred_element_type=F32)


def _ln_kernel(x_ref, g_ref, b_ref, hf_ref, hb_ref):
    h = _ln_rows(x_ref[...], g_ref[...], b_ref[...])
    hf_ref[...] = h
    hb_ref[...] = h.astype(BF16)


def _layernorm_in(x2, g, b, tm=512):
    T, D = x2.shape
    row = pl.BlockSpec((tm, D), lambda i: (i, 0))
    return pl.pallas_call(
        _ln_kernel,
        out_shape=(jax.ShapeDtypeStruct((T, D), F32), jax.ShapeDtypeStruct((T, D), BF16)),
        grid=(T // tm,),
        in_specs=[row, _const_spec((1, D)), _const_spec((1, D))],
        out_specs=(row, row),
        compiler_params=_cparams("parallel"),
    )(x2, g.reshape(1, D), b.reshape(1, D))


def _mm_kernel(a_ref, w_ref, o_ref):
    o_ref[...] = _dot(a_ref[...], w_ref[...]).astype(o_ref.dtype)


def _matmul(a, w, tn, tm=512, out_dtype=F32):
    M, K = a.shape
    N = w.shape[1]
    assert M % tm == 0 and N % tn == 0
    return pl.pallas_call(
        _mm_kernel,
        out_shape=jax.ShapeDtypeStruct((M, N), out_dtype),
        grid=(N // tn, M // tm),
        in_specs=[pl.BlockSpec((tm, K), lambda j, i: (i, 0)),
                  pl.BlockSpec((K, tn), lambda j, i: (0, j))],
        out_specs=pl.BlockSpec((tm, tn), lambda j, i: (i, j)),
        compiler_params=_cparams("parallel", "parallel"),
    )(a, w)


def _ssd_kernel(xbc_ref, z0_ref, z1_ref, dt_ref, convw_ref, convb_ref, dtb_ref, dtbT_ref,
                a_ref, aT_ref, dskip_ref, ng_ref, e_ref, y_ref, xpad, state, ybuf):
    c = pl.program_id(1)
    Q = xbc_ref.shape[0]
    G, N, P = SSD_GROUPS, SSD_STATE, SSD_HEADDIM
    GW = SSD_INNER // G
    HALO = SUBLANES

    @pl.when(c == 0)
    def _():
        xpad[0:HALO, :] = jnp.zeros((HALO, SSD_XBC), F32)
        state[...] = jnp.zeros(state.shape, F32)

    @pl.when(c > 0)
    def _():
        xpad[0:HALO, :] = xpad[Q:Q + HALO, :]

    xpad[HALO:HALO + Q, :] = xbc_ref[...]
    conv = convb_ref[...]
    for k in range(SSD_CONV):
        conv = conv + convw_ref[k:k + 1, :] * xpad[pl.ds(HALO - (SSD_CONV - 1) + k, Q), :]
    xc = _silu(conv)
    xs = xc[:, :SSD_INNER]

    dt_blk = dt_ref[...]
    dt = _softplus(dt_blk[:, :SSD_HEADS] + dtb_ref[...])
    a = dt * a_ref[...]
    dtT = _softplus(dt_blk.T[:SSD_HEADS, :] + dtbT_ref[...])
    aT = dtT * aT_ref[...]

    row = lax.broadcasted_iota(jnp.int32, (Q, Q), 0)
    col = lax.broadcasted_iota(jnp.int32, (Q, Q), 1)
    causal = row >= col
    tril = jnp.where(causal, 1.0, 0.0).astype(F32)
    triu = jnp.where(row <= col, 1.0, 0.0).astype(F32)
    acum = _dot(tril, a, precision=HI)
    acumT = _dot(aT, triu, precision=HI)

    e = e_ref[...]
    dt_e = _dot(dt, e, precision=HI)
    eac_e = _dot(jnp.exp(acum), e, precision=HI)
    ds_e = _dot(jnp.exp(acum[Q - 1:Q, :] - acum), e, precision=HI)
    X = xs * dt_e
    Xb = X.astype(BF16)
    Xd = (X * ds_e).astype(BF16)
    lane = lax.broadcasted_iota(jnp.int32, (Q, LANES), 1)
    lo_half = lane < P

    for g in range(G):
        Bg = xc[:, SSD_INNER + g * N:SSD_INNER + (g + 1) * N].astype(BF16)
        Cg = xc[:, SSD_INNER + G * N + g * N:SSD_INNER + G * N + (g + 1) * N].astype(BF16)
        cb = _dot_nt(Cg, Bg)
        st = state[g]
        eg = eac_e[:, g * GW:(g + 1) * GW]
        yoff = _dot(Cg, st.astype(BF16)) * eg
        new = _dot_tn(Bg, Xd[:, g * GW:(g + 1) * GW])
        state[g] = st * eg[Q - 1:Q, :] + new
        for jj in range(GW // LANES):
            p = g * (GW // LANES) + jj
            h0 = 2 * p
            Xp = Xb[:, p * LANES:(p + 1) * LANES]
            zero = jnp.zeros_like(Xp)
            top = jnp.where(lo_half, Xp, zero)
            bot = jnp.where(lo_half, zero, Xp)
            yd = None
            for hh, rhs in ((h0, top), (h0 + 1, bot)):
                seg = acum[:, hh:hh + 1] - acumT[hh:hh + 1, :]
                decay = jnp.exp(jnp.where(causal, seg, -jnp.inf))
                part = _dot((cb * decay).astype(BF16), rhs)
                yd = part if yd is None else yd + part
            ybuf[:, p * LANES:(p + 1) * LANES] = yd + yoff[:, jj * LANES:(jj + 1) * LANES]

    y = ybuf[...] + xs * dskip_ref[...]
    for g in range(G):
        zg = (z0_ref if g < G // 2 else z1_ref)[:, (g % (G // 2)) * GW:(g % (G // 2) + 1) * GW]
        v = y[:, g * GW:(g + 1) * GW] * _silu(zg)
        v = v * lax.rsqrt(jnp.mean(v * v, -1, keepdims=True) + LN_EPS)
        y_ref[:, g * GW:(g + 1) * GW] = (v * ng_ref[:, g * GW:(g + 1) * GW]).astype(BF16)


def _ssd_branch(xz, small, conv_w, conv_b, dt_bias, A_log, D_skip, norm_g, B, L):
    T = B * L
    Q = SSD_Q
    nc = L // Q
    H = SSD_HEADS
    A = -jnp.exp(A_log.astype(F32))
    expand = jnp.asarray(np.kron(np.eye(H, dtype=np.float32), np.ones((1, SSD_HEADDIM), np.float32)))
    d_e = jnp.repeat(D_skip.astype(F32), SSD_HEADDIM).reshape(1, SSD_INNER)
    zw = SSD_INNER // 2
    tok = lambda w, j: pl.BlockSpec((Q, w), lambda b, c, j=j: (b * nc + c, j))
    return pl.pallas_call(
        _ssd_kernel,
        out_shape=jax.ShapeDtypeStruct((T, SSD_INNER), BF16),
        grid=(B, nc),
        in_specs=[tok(SSD_XBC, 0), tok(zw, SSD_XBC // zw), tok(zw, SSD_XBC // zw + 1), tok(LANES, 6),
                  _const_spec((SSD_CONV, SSD_XBC)), _const_spec((1, SSD_XBC)),
                  _const_spec((1, H)), _const_spec((H, 1)), _const_spec((1, H)), _const_spec((H, 1)),
                  _const_spec((1, SSD_INNER)), _const_spec((1, SSD_INNER)), _const_spec((H, SSD_INNER))],
        out_specs=pl.BlockSpec((Q, SSD_INNER), lambda b, c: (b * nc + c, 0)),
        scratch_shapes=[pltpu.VMEM((Q + 2 * SUBLANES, SSD_XBC), F32),
                        pltpu.VMEM((SSD_GROUPS, SSD_STATE, SSD_INNER // SSD_GROUPS), F32),
                        pltpu.VMEM((Q, SSD_INNER), F32)],
        compiler_params=_cparams("parallel", "arbitrary"),
    )(xz, xz, xz, small, conv_w, conv_b.reshape(1, -1), dt_bias.reshape(1, H), dt_bias.reshape(H, 1),
      A.reshape(1, H), A.reshape(H, 1), d_e, norm_g.reshape(1, -1), expand)


def _prep_kernel(ckv_ref, kidx_ref, kvg_ref, ig_ref, ib_ref, ckvn_ref, ckvt_ref, kn_ref):
    c = ckv_ref[...]
    cn = c * lax.rsqrt(jnp.mean(c * c, -1, keepdims=True) + LN_EPS) * kvg_ref[...]
    ckvn_ref[...] = cn.astype(BF16)
    ckvt_ref[0] = cn.T.astype(BF16)
    k = kidx_ref[...]
    lane = lax.broadcasted_iota(jnp.int32, k.shape, 1)
    live = lane < IDX_HEADDIM
    mu = jnp.sum(k, -1, keepdims=True) * (1.0 / IDX_HEADDIM)
    kc = jnp.where(live, k - mu, 0.0)
    var = jnp.sum(kc * kc, -1, keepdims=True) * (1.0 / IDX_HEADDIM)
    kn = jnp.where(live, kc * lax.rsqrt(var + LN_EPS) * ig_ref[...] + ib_ref[...], 0.0)
    kn_ref[...] = kn.astype(BF16)


def _dsa_prep(qc, small, kv_g, idx_g, idx_b):
    T = qc.shape[0]
    tm = ATT_TK
    pad = lambda v: jnp.pad(v.astype(F32), (0, LANES - IDX_HEADDIM)).reshape(1, LANES)
    return pl.pallas_call(
        _prep_kernel,
        out_shape=(jax.ShapeDtypeStruct((T, ATT_LATENT), BF16),
                   jax.ShapeDtypeStruct((T // tm, ATT_LATENT, tm), BF16),
                   jax.ShapeDtypeStruct((T, LANES), BF16)),
        grid=(T // tm,),
        in_specs=[pl.BlockSpec((tm, ATT_LATENT), lambda i: (i, 4)),
                  pl.BlockSpec((tm, LANES), lambda i: (i, 4)),
                  _const_spec((1, ATT_LATENT)), _const_spec((1, LANES)), _const_spec((1, LANES))],
        out_specs=(pl.BlockSpec((tm, ATT_LATENT), lambda i: (i, 0)),
                   pl.BlockSpec((1, ATT_LATENT, tm), lambda i: (i, 0, 0)),
                   pl.BlockSpec((tm, LANES), lambda i: (i, 0))),
        compiler_params=_cparams("parallel"),
    )(qc, small, kv_g.reshape(1, -1), pad(idx_g), pad(idx_b))


def _attn_kernel(q_ref, qidx_ref, widx_ref, ckv_ref, ckvt_ref, kidx_ref, bias_ref, wuk_ref, wuvt_ref,
                 o_ref, keys, qt, qit, qlat, m_scr, l_scr, acc, ot, *, topk):
    i = pl.program_id(1)
    QB, TK, NH, C, DH, DI = ATT_QB, ATT_TK, ATT_HEADS, ATT_LATENT, ATT_HEADDIM, IDX_HEADDIM
    n_tiles = i + 1

    qt[...] = q_ref[...].T.astype(BF16)
    for h in range(NH):
        ql = _dot(wuk_ref[h], qt[h * DH:(h + 1) * DH, :])
        qlat[h] = (ql * (DH ** -0.5 * LOG2E)).astype(BF16)

    qit[0:IDX_HEADS * DI, :] = qidx_ref[...].T.astype(BF16)
    qit[IDX_HEADS * DI:, :] = jnp.zeros((LANES - DI, QB), BF16)
    wt = widx_ref[...].T[:IDX_HEADS, :] * (IDX_HEADS ** -0.5 * DI ** -0.5)
    qchunk = (i * QB + lax.broadcasted_iota(jnp.int32, (TK, QB), 1)) // CHUNK
    kin = lax.broadcasted_iota(jnp.int32, (TK, QB), 0)

    def score_tile(j, _):
        kt = kidx_ref[pl.ds(pl.multiple_of(j * TK, TK), TK), :]
        s = jnp.zeros((TK, QB), F32)
        for h in range(IDX_HEADS):
            sh = _dot(kt, qit[h * DI:h * DI + LANES, :])
            s = s + wt[h:h + 1, :] * jnp.maximum(sh, 0.0)
        bits = pltpu.bitcast(s, jnp.int32)
        skey = bits ^ ((bits >> 31) & 0x7FFFFFFF)
        adm = ((j * TK + kin) // CHUNK) <= qchunk
        keys[j] = jnp.where(adm, skey, INT_MIN)
        return 0

    lax.fori_loop(0, n_tiles, score_tile, 0)

    def count_ge(cand):
        def body(j, cnt):
            hit = jnp.where(keys[j] >= cand, 1, 0)
            return cnt + jnp.sum(hit.reshape(TK // SUBLANES, SUBLANES, QB), axis=0)
        cnt = lax.fori_loop(0, n_tiles, body, jnp.zeros((SUBLANES, QB), jnp.int32))
        return jnp.sum(cnt, axis=0, keepdims=True)

    tau0 = jnp.full((1, QB), INT_MIN, jnp.int32)
    tau0 = jnp.where(count_ge(jnp.zeros((1, QB), jnp.int32)) >= topk, 0, tau0)

    def bit_step(t, tau):
        cand = tau + jnp.left_shift(jnp.int32(1), 30 - t)
        return jnp.where(count_ge(cand) >= topk, cand, tau)

    tau = lax.fori_loop(0, 31, bit_step, tau0)
    tau = jnp.maximum(tau, INT_MIN + 1)

    m_scr[...] = jnp.full(m_scr.shape, NEG_BIG, F32)
    l_scr[...] = jnp.zeros(l_scr.shape, F32)
    acc[...] = jnp.zeros(acc.shape, F32)

    def attend(j, near_half):
        kv = ckv_ref[pl.ds(pl.multiple_of(j * TK, TK), TK), :]
        kvt = ckvt_ref[j]
        maskb = jnp.where(keys[j] >= tau, 0.0, NEG_BIG)
        for h in range(NH):
            s = _dot(kv, qlat[h]) + maskb
            if near_half is not None:
                s = s + bias_ref[h, near_half * TK:(near_half + 1) * TK, :]
            m_old = m_scr[h]
            m_new = jnp.maximum(m_old, jnp.max(s, axis=0, keepdims=True))
            alpha = jnp.exp2(m_old - m_new)
            pr = jnp.exp2(s - m_new)
            l_scr[h] = alpha * l_scr[h] + jnp.sum(pr, axis=0, keepdims=True)
            acc[h] = alpha * acc[h] + _dot(kvt, pr.astype(BF16))
            m_scr[h] = m_new

    def far_tile(j, _):
        attend(j, None)
        return 0

    lax.fori_loop(0, i - 1, far_tile, 0)

    @pl.when(i > 0)
    def _():
        attend(i - 1, 0)

    attend(i, 1)

    for h in range(NH):
        ol = (acc[h] / l_scr[h]).astype(BF16)
        ot[h * DH:(h + 1) * DH, :] = _dot(wuvt_ref[h], ol)
    o_ref[...] = ot[...].T.astype(o_ref.dtype)


def _t5_bucket_np(rel):
    nb = REL_BUCKETS // 2
    max_exact = nb // 2
    n = np.abs(rel).astype(np.int64)
    nn = np.maximum(n, 1)
    sq = nn * nn
    log2_sq = np.floor(np.log2(sq.astype(np.float64))).astype(np.int64)
    log2_sq = np.where(2 ** (log2_sq + 1) <= sq, log2_sq + 1, log2_sq)
    log2_sq = np.where(2 ** log2_sq > sq, log2_sq - 1, log2_sq)
    large = np.minimum(max_exact + (log2_sq - 6), nb - 1)
    return np.where(rel > 0, nb, 0) + np.where(n < max_exact, n, large)


def _dsa_attention(qc, small, ckv_n, ckv_t, kidx_n, w_uk, w_uv, rel_bias, B, L):
    T = B * L
    QB, TK, NH, C, DH = ATT_QB, ATT_TK, ATT_HEADS, ATT_LATENT, ATT_HEADDIM
    nq = L // QB
    topk = min(TOPK_MAX, L // 4)
    n_rel = 2 * TK + QB - 1
    period = n_rel + 1
    rel_of = np.arange(n_rel) - (TK + QB - 1)
    far_bucket = int(_t5_bucket_np(np.array([-(TK + 1)]))[0])
    table = (rel_bias.astype(F32) - rel_bias[far_bucket].astype(F32)[None, :]) * LOG2E
    v = jnp.take(table, jnp.asarray(_t5_bucket_np(rel_of)), axis=0).T
    u = jnp.pad(v, ((0, 0), (0, 1)))[:, (QB - 1 - np.arange(period)) % period]
    flat = jnp.tile(u, (1, 2 * TK))[:, :2 * TK * (period - 1)]
    bias_near = flat.reshape(NH, 2 * TK, period - 1)[:, :, :QB]
    wuk = w_uk.astype(BF16)
    wuvt = jnp.transpose(w_uv, (0, 2, 1)).astype(BF16)

    return pl.pallas_call(
        functools.partial(_attn_kernel, topk=topk),
        out_shape=jax.ShapeDtypeStruct((T, NH * DH), BF16),
        grid=(B, nq),
        in_specs=[
            pl.BlockSpec((QB, NH * DH), lambda b, i: (b * nq + i, 0)),
            pl.BlockSpec((QB, IDX_HEADS * IDX_HEADDIM), lambda b, i: (b * nq + i, 0)),
            pl.BlockSpec((QB, LANES), lambda b, i: (b * nq + i, 5)),
            pl.BlockSpec((L, C), lambda b, i: (b, 0)),
            pl.BlockSpec((L // TK, C, TK), lambda b, i: (b, 0, 0)),
            pl.BlockSpec((L, LANES), lambda b, i: (b, 0)),
            _const_spec((NH, 2 * TK, QB)),
            _const_spec((NH, C, DH)),
            _const_spec((NH, DH, C)),
        ],
        out_specs=pl.BlockSpec((QB, NH * DH), lambda b, i: (b * nq + i, 0)),
        scratch_shapes=[pltpu.VMEM((L // TK, TK, QB), jnp.int32),
                        pltpu.VMEM((NH * DH, QB), BF16),
                        pltpu.VMEM((IDX_HEADS * IDX_HEADDIM + LANES - IDX_HEADDIM, QB), BF16),
                        pltpu.VMEM((NH, C, QB), BF16),
                        pltpu.VMEM((NH, 1, QB), F32),
                        pltpu.VMEM((NH, 1, QB), F32),
                        pltpu.VMEM((NH, C, QB), F32),
                        pltpu.VMEM((NH * DH, QB), F32)],
        compiler_params=_cparams("parallel", "arbitrary"),
    )(qc, small, small, ckv_n, ckv_t, kidx_n, bias_near, wuk, wuvt)


def _merge_kernel(ys_ref, ya_ref, gs_ref, ga_ref, h_ref, wbs_ref, wba_ref, wo_ref, g_ref, b_ref,
                  hf_ref, hb_ref):
    sig = lambda v: 1.0 / (1.0 + jnp.exp(-v))
    m = sig(gs_ref[...]) * _dot(ys_ref[...], wbs_ref[...]) + sig(ga_ref[...]) * _dot(ya_ref[...], wba_ref[...])
    mix = _dot(m.astype(BF16), wo_ref[...])
    h1 = _ln_rows(DN_ALPHA * h_ref[...] + mix, g_ref[...], b_ref[...])
    hf_ref[...] = h1
    hb_ref[...] = h1.astype(BF16)


def _merge(y_ssd, y_att, gates, h, w_br_ssd, w_br_att, w_out, g, b, tm=256):
    T, D = h.shape
    row = lambda w, j=0: pl.BlockSpec((tm, w), lambda i, j=j: (i, j))
    return pl.pallas_call(
        _merge_kernel,
        out_shape=(jax.ShapeDtypeStruct((T, D), F32), jax.ShapeDtypeStruct((T, D), BF16)),
        grid=(T // tm,),
        in_specs=[row(SSD_INNER), row(ATT_HEADS * ATT_HEADDIM), row(D, 0), row(D, 1), row(D),
                  _const_spec(w_br_ssd.shape), _const_spec(w_br_att.shape), _const_spec(w_out.shape),
                  _const_spec((1, D)), _const_spec((1, D))],
        out_specs=(row(D), row(D)),
        compiler_params=_cparams("parallel"),
    )(y_ssd, y_att, gates, gates, h, w_br_ssd, w_br_att, w_out, g.reshape(1, D), b.reshape(1, D))


FFN_COLS = 1408


def _ffn_up_kernel(h_ref, wup_ref, cw_ref, cb_ref, g_ref, upad, tail):
    c = pl.program_id(1)
    tm = h_ref.shape[0]
    HALO = SUBLANES

    @pl.when(c == 0)
    def _():
        tail[...] = jnp.zeros(tail.shape, F32)

    hb = h_ref[...]
    for cc in range(D_FF // FFN_COLS):
        halves = []
        for half in range(2):
            cols = slice(half * D_FF + cc * FFN_COLS, half * D_FF + (cc + 1) * FFN_COLS)
            upad[0:HALO, :] = tail[:, cols]
            upad[HALO:HALO + tm, :] = _dot(hb, wup_ref[:, cols])
            tail[:, cols] = upad[tm:tm + HALO, :]
            u = cb_ref[:, cols]
            for k in range(FFN_CONV):
                u = u + cw_ref[k:k + 1, cols] * upad[pl.ds(HALO - (FFN_CONV - 1) + k, tm), :]
            halves.append(u)
        g_ref[:, cc * FFN_COLS:(cc + 1) * FFN_COLS] = (_silu(halves[0]) * halves[1]).astype(BF16)


def _ffn_up(h1b, w_up, conv_w, conv_b, B, L, tm=256):
    T, D = h1b.shape
    nt = L // tm
    return pl.pallas_call(
        _ffn_up_kernel,
        out_shape=jax.ShapeDtypeStruct((T, D_FF), BF16),
        grid=(B, nt),
        in_specs=[pl.BlockSpec((tm, D), lambda b, c: (b * nt + c, 0)),
                  _const_spec(w_up.shape), _const_spec(conv_w.shape), _const_spec((1, 2 * D_FF))],
        out_specs=pl.BlockSpec((tm, D_FF), lambda b, c: (b * nt + c, 0)),
        scratch_shapes=[pltpu.VMEM((tm + SUBLANES, FFN_COLS), F32),
                        pltpu.VMEM((SUBLANES, 2 * D_FF), F32)],
        compiler_params=_cparams("parallel", "arbitrary"),
    )(h1b, w_up, conv_w, conv_b.reshape(1, -1))


def _ffn_down_kernel(g_ref, h_ref, wd_ref, lg_ref, lb_ref, o_ref):
    f = _dot(g_ref[...], wd_ref[...])
    o_ref[...] = _ln_rows(DN_ALPHA * h_ref[...] + f, lg_ref[...], lb_ref[...])


def _ffn_down(g, h1, w_down, lg, lb, tm=256):
    T, D = h1.shape
    return pl.pallas_call(
        _ffn_down_kernel,
        out_shape=jax.ShapeDtypeStruct((T, D), F32),
        grid=(T // tm,),
        in_specs=[pl.BlockSpec((tm, D_FF), lambda i: (i, 0)), pl.BlockSpec((tm, D), lambda i: (i, 0)),
                  _const_spec(w_down.shape), _const_spec((1, D)), _const_spec((1, D))],
        out_specs=pl.BlockSpec((tm, D), lambda i: (i, 0)),
        compiler_params=_cparams("parallel"),
    )(g, h1, w_down, lg.reshape(1, D), lb.reshape(1, D))


def _split_w_in(w_in):
    sizes = (SSD_INNER, SSD_XBC, SSD_HEADS, ATT_HEADS * ATT_HEADDIM, ATT_LATENT,
             IDX_HEADS * IDX_HEADDIM, IDX_HEADDIM, IDX_HEADS, D_MODEL, D_MODEL)
    pts = np.cumsum(sizes)[:-1].tolist()
    return jnp.split(w_in, pts, axis=1)


def _pad_cols(w, width):
    return jnp.pad(w, ((0, 0), (0, width - w.shape[1])))


def kernel(x, ln_in_g, ln_in_b, w_in, ssd_conv_w, ssd_conv_b, ssd_dt_bias, ssd_A_log, ssd_D, ssd_norm_g, att_kv_norm_g, att_w_uk, att_w_uv, idx_k_norm_g, idx_k_norm_b, rel_bias, w_br_ssd, w_br_att, w_out, ln1_g, ln1_b, ffn_w_up, ffn_conv_w, ffn_conv_b, ffn_w_down, ln2_g, ln2_b):
    B, L, D = x.shape
    T = B * L
    l = 0
    wz, wxbc, wdt, wq, wckv, wqi, wki, wwi, wgs, wga = _split_w_in(w_in[l])
    w_xz = jnp.concatenate([wxbc, wz], axis=1).astype(BF16)
    w_qc = jnp.concatenate([wq, wckv], axis=1).astype(BF16)
    w_g = jnp.concatenate([wgs, wga], axis=1).astype(BF16)
    w_small = jnp.concatenate([wqi, _pad_cols(wki, LANES), _pad_cols(wwi, LANES), _pad_cols(wdt, LANES)],
                              axis=1).astype(BF16)

    h, hb = _layernorm_in(x.reshape(T, D), ln_in_g, ln_in_b)
    xz = _matmul(hb, w_xz, tn=1024)
    qc = _matmul(hb, w_qc, tn=1280)
    gates = _matmul(hb, w_g, tn=1024)
    small = _matmul(hb, w_small, tn=896)

    y_ssd = _ssd_branch(xz, small, ssd_conv_w[l], ssd_conv_b[l], ssd_dt_bias[l], ssd_A_log[l], ssd_D[l],
                        ssd_norm_g[l], B, L)
    ckv_n, ckv_t, kidx_n = _dsa_prep(qc, small, att_kv_norm_g[l], idx_k_norm_g[l], idx_k_norm_b[l])
    y_att = _dsa_attention(qc, small, ckv_n, ckv_t, kidx_n, att_w_uk[l], att_w_uv[l], rel_bias, B, L)

    h1, h1b = _merge(y_ssd, y_att, gates, h, w_br_ssd[l].astype(BF16), w_br_att[l].astype(BF16),
                     w_out[l].astype(BF16), ln1_g[l], ln1_b[l])
    g = _ffn_up(h1b, ffn_w_up[l].astype(BF16), ffn_conv_w[l], ffn_conv_b[l], B, L)
    out = _ffn_down(g, h1, ffn_w_down[l].astype(BF16), ln2_g[l], ln2_b[l])
    return out.reshape(B, L, D).astype(x.dtype)
```

```python
import functools
import math

import numpy as np
import jax
import jax.numpy as jnp
from jax import lax
from jax.experimental import pallas as pl
from jax.experimental.pallas import tpu as pltpu

F32 = jnp.float32
BF16 = jnp.bfloat16

D_MODEL = 1024
CHUNK = 64
SSD_INNER = 2048
SSD_HEADDIM = 64
SSD_HEADS = 32
SSD_GROUPS = 4
SSD_STATE = 128
SSD_CONV = 4
SSD_XBC = SSD_INNER + 2 * SSD_GROUPS * SSD_STATE
ATT_HEADS = 16
ATT_HEADDIM = 64
ATT_LATENT = 256
IDX_HEADS = 8
IDX_HEADDIM = 64
TOPK_MAX = 256
REL_BUCKETS = 32
REL_MAX_DIST = 128
D_FF = 2816
FFN_CONV = 3
DN_ALPHA = 2.0 ** 0.25
LN_EPS = 1e-5

LANES = 128
SUBLANES = 8
VMEM_LIMIT = 56 * 1024 * 1024

SSD_Q = 128
ATT_QB = 256
ATT_TK = 256
INT_MIN = -2 ** 31
NEG_BIG = -1e30
LOG2E = math.log2(math.e)


def _cparams(*sem):
    return pltpu.CompilerParams(dimension_semantics=sem, vmem_limit_bytes=VMEM_LIMIT)


def _const_spec(shape):
    nd = len(shape)
    return pl.BlockSpec(shape, lambda *_: (0,) * nd, pipeline_mode=pl.Buffered(1))


def _silu(x):
    return x / (1.0 + jnp.exp(-x))


def _softplus(x):
    return jnp.maximum(x, 0.0) + jnp.log1p(jnp.exp(-jnp.abs(x)))


def _ln_rows(x, g, b):
    mu = jnp.mean(x, -1, keepdims=True)
    xc = x - mu
    var = jnp.mean(xc * xc, -1, keepdims=True)
    return xc * lax.rsqrt(var + LN_EPS) * g + b


def _dot(a, b, **kw):
    return jnp.dot(a, b, preferred_element_type=F32, **kw)


def _dot_exact01(m01, x, x_on_left):
    total = None
    r = x
    for _ in range(3):
        part = r.astype(BF16)
        r = r - part.astype(F32)
        t = _dot(part, m01) if x_on_left else _dot(m01, part)
        total = t if total is None else total + t
    return total


def _dot_nt(a, b):
    return lax.dot_general(a, b, (((1,), (1,)), ((), ())), preferred_element_type=F32)


def _dot_tn(a, b):
    return lax.dot_general(a, b, (((0,), (0,)), ((), ())), preferred_element_type=F32)


def _ln_kernel(x_ref, g_ref, b_ref, hf_ref, hb_ref):
    h = _ln_rows(x_ref[...], g_ref[...], b_ref[...])
    hf_ref[...] = h
    hb_ref[...] = h.astype(BF16)


def _layernorm_in(x2, g, b, tm=512):
    T, D = x2.shape
    row = pl.BlockSpec((tm, D), lambda i: (i, 0))
    return pl.pallas_call(
        _ln_kernel,
        out_shape=(jax.ShapeDtypeStruct((T, D), F32), jax.ShapeDtypeStruct((T, D), BF16)),
        grid=(T // tm,),
        in_specs=[row, _const_spec((1, D)), _const_spec((1, D))],
        out_specs=(row, row),
        compiler_params=_cparams("parallel"),
    )(x2, g.reshape(1, D), b.reshape(1, D))


def _mm_kernel(a_ref, w_ref, o_ref):
    o_ref[...] = _dot(a_ref[...], w_ref[...]).astype(o_ref.dtype)


def _matmul(a, w, tn, tm=512, out_dtype=F32):
    M, K = a.shape
    N = w.shape[1]
    assert M % tm == 0 and N % tn == 0
    return pl.pallas_call(
        _mm_kernel,
        out_shape=jax.ShapeDtypeStruct((M, N), out_dtype),
        grid=(N // tn, M // tm),
        in_specs=[pl.BlockSpec((tm, K), lambda j, i: (i, 0)),
                  pl.BlockSpec((K, tn), lambda j, i: (0, j))],
        out_specs=pl.BlockSpec((tm, tn), lambda j, i: (i, j)),
        compiler_params=_cparams("parallel", "parallel"),
    )(a, w)


def _ssd_kernel(xbc_ref, z0_ref, z1_ref, dt_ref, convw_ref, convb_ref, dtb_ref, dtbT_ref,
                a_ref, aT_ref, dskip_ref, ng_ref, e_ref, y_ref, xpad, state, ybuf):
    c = pl.program_id(1)
    Q = xbc_ref.shape[0]
    G, N, P = SSD_GROUPS, SSD_STATE, SSD_HEADDIM
    GW = SSD_INNER // G
    HALO = SUBLANES

    @pl.when(c == 0)
    def _():
        xpad[0:HALO, :] = jnp.zeros((HALO, SSD_XBC), F32)
        state[...] = jnp.zeros(state.shape, F32)

    @pl.when(c > 0)
    def _():
        xpad[0:HALO, :] = xpad[Q:Q + HALO, :]

    xpad[HALO:HALO + Q, :] = xbc_ref[...].astype(F32)
    conv = convb_ref[...]
    for k in range(SSD_CONV):
        conv = conv + convw_ref[k:k + 1, :] * xpad[pl.ds(HALO - (SSD_CONV - 1) + k, Q), :]
    xc = _silu(conv)
    xs = xc[:, :SSD_INNER]

    dt_blk = dt_ref[...]
    dt = _softplus(dt_blk[:, :SSD_HEADS] + dtb_ref[...])
    a = dt * a_ref[...]
    dtT = _softplus(dt_blk.T[:SSD_HEADS, :] + dtbT_ref[...])
    aT = dtT * aT_ref[...]

    row = lax.broadcasted_iota(jnp.int32, (Q, Q), 0)
    col = lax.broadcasted_iota(jnp.int32, (Q, Q), 1)
    causal = row >= col
    tril = jnp.where(causal, 1.0, 0.0).astype(BF16)
    triu = jnp.where(row <= col, 1.0, 0.0).astype(BF16)
    acum = _dot_exact01(tril, a, False)
    acumT = _dot_exact01(triu, aT, True)

    e = e_ref[...]
    dt_e = _dot_exact01(e, dt, True)
    eac_e = _dot_exact01(e, jnp.exp(acum), True)
    ds_e = _dot_exact01(e, jnp.exp(acum[Q - 1:Q, :] - acum), True)
    X = xs * dt_e
    Xb = X.astype(BF16)
    Xd = (X * ds_e).astype(BF16)
    lane = lax.broadcasted_iota(jnp.int32, (Q, LANES), 1)
    lo_half = lane < P

    for g in range(G):
        Bg = xc[:, SSD_INNER + g * N:SSD_INNER + (g + 1) * N].astype(BF16)
        Cg = xc[:, SSD_INNER + G * N + g * N:SSD_INNER + G * N + (g + 1) * N].astype(BF16)
        cb = _dot_nt(Cg, Bg)
        st = state[g]
        eg = eac_e[:, g * GW:(g + 1) * GW]
        yoff = _dot(Cg, st.astype(BF16)) * eg
        new = _dot_tn(Bg, Xd[:, g * GW:(g + 1) * GW])
        state[g] = st * eg[Q - 1:Q, :] + new
        for jj in range(GW // LANES):
            p = g * (GW // LANES) + jj
            h0 = 2 * p
            Xp = Xb[:, p * LANES:(p + 1) * LANES]
            zero = jnp.zeros_like(Xp)
            top = jnp.where(lo_half, Xp, zero)
            bot = jnp.where(lo_half, zero, Xp)
            yd = None
            for hh, rhs in ((h0, top), (h0 + 1, bot)):
                seg = acum[:, hh:hh + 1] - acumT[hh:hh + 1, :]
                decay = jnp.exp(jnp.where(causal, seg, -jnp.inf))
                part = _dot((cb * decay).astype(BF16), rhs)
                yd = part if yd is None else yd + part
            ybuf[:, p * LANES:(p + 1) * LANES] = yd + yoff[:, jj * LANES:(jj + 1) * LANES]

    y = ybuf[...] + xs * dskip_ref[...]
    for g in range(G):
        zg = (z0_ref if g < G // 2 else z1_ref)[:, (g % (G // 2)) * GW:(g % (G // 2) + 1) * GW]
        v = y[:, g * GW:(g + 1) * GW] * _silu(zg.astype(F32))
        v = v * lax.rsqrt(jnp.mean(v * v, -1, keepdims=True) + LN_EPS)
        y_ref[:, g * GW:(g + 1) * GW] = (v * ng_ref[:, g * GW:(g + 1) * GW]).astype(BF16)


def _ssd_branch(xz, small, conv_w, conv_b, dt_bias, A_log, D_skip, norm_g, B, L):
    T = B * L
    Q = SSD_Q
    nc = L // Q
    H = SSD_HEADS
    A = -jnp.exp(A_log.astype(F32))
    expand = jnp.asarray(np.kron(np.eye(H, dtype=np.float32), np.ones((1, SSD_HEADDIM), np.float32)), BF16)
    d_e = jnp.repeat(D_skip.astype(F32), SSD_HEADDIM).reshape(1, SSD_INNER)
    zw = SSD_INNER // 2
    tok = lambda w, j: pl.BlockSpec((Q, w), lambda b, c, j=j: (b * nc + c, j))
    return pl.pallas_call(
        _ssd_kernel,
        out_shape=jax.ShapeDtypeStruct((T, SSD_INNER), BF16),
        grid=(B, nc),
        in_specs=[tok(SSD_XBC, 0), tok(zw, SSD_XBC // zw), tok(zw, SSD_XBC // zw + 1), tok(LANES, 6),
                  _const_spec((SSD_CONV, SSD_XBC)), _const_spec((1, SSD_XBC)),
                  _const_spec((1, H)), _const_spec((H, 1)), _const_spec((1, H)), _const_spec((H, 1)),
                  _const_spec((1, SSD_INNER)), _const_spec((1, SSD_INNER)), _const_spec((H, SSD_INNER))],
        out_specs=pl.BlockSpec((Q, SSD_INNER), lambda b, c: (b * nc + c, 0)),
        scratch_shapes=[pltpu.VMEM((Q + 2 * SUBLANES, SSD_XBC), F32),
                        pltpu.VMEM((SSD_GROUPS, SSD_STATE, SSD_INNER // SSD_GROUPS), F32),
                        pltpu.VMEM((Q, SSD_INNER), F32)],
        compiler_params=_cparams("parallel", "arbitrary"),
    )(xz, xz, xz, small, conv_w, conv_b.reshape(1, -1), dt_bias.reshape(1, H), dt_bias.reshape(H, 1),
      A.reshape(1, H), A.reshape(H, 1), d_e, norm_g.reshape(1, -1), expand)


def _prep_kernel(ckv_ref, kidx_ref, kvg_ref, ig_ref, ib_ref, ckvn_ref, ckvt_ref, kn_ref):
    c = ckv_ref[...].astype(F32)
    cn = c * lax.rsqrt(jnp.mean(c * c, -1, keepdims=True) + LN_EPS) * kvg_ref[...]
    ckvn_ref[...] = cn.astype(BF16)
    ckvt_ref[0] = cn.T.astype(BF16)
    k = kidx_ref[...]
    lane = lax.broadcasted_iota(jnp.int32, k.shape, 1)
    live = lane < IDX_HEADDIM
    mu = jnp.sum(k, -1, keepdims=True) * (1.0 / IDX_HEADDIM)
    kc = jnp.where(live, k - mu, 0.0)
    var = jnp.sum(kc * kc, -1, keepdims=True) * (1.0 / IDX_HEADDIM)
    kn = jnp.where(live, kc * lax.rsqrt(var + LN_EPS) * ig_ref[...] + ib_ref[...], 0.0)
    kn_ref[...] = kn.astype(BF16)


def _dsa_prep(qc, small, kv_g, idx_g, idx_b):
    T = qc.shape[0]
    tm = ATT_TK
    pad = lambda v: jnp.pad(v.astype(F32), (0, LANES - IDX_HEADDIM)).reshape(1, LANES)
    return pl.pallas_call(
        _prep_kernel,
        out_shape=(jax.ShapeDtypeStruct((T, ATT_LATENT), BF16),
                   jax.ShapeDtypeStruct((T // tm, ATT_LATENT, tm), BF16),
                   jax.ShapeDtypeStruct((T, LANES), BF16)),
        grid=(T // tm,),
        in_specs=[pl.BlockSpec((tm, ATT_LATENT), lambda i: (i, 4)),
                  pl.BlockSpec((tm, LANES), lambda i: (i, 4)),
                  _const_spec((1, ATT_LATENT)), _const_spec((1, LANES)), _const_spec((1, LANES))],
        out_specs=(pl.BlockSpec((tm, ATT_LATENT), lambda i: (i, 0)),
                   pl.BlockSpec((1, ATT_LATENT, tm), lambda i: (i, 0, 0)),
                   pl.BlockSpec((tm, LANES), lambda i: (i, 0))),
        compiler_params=_cparams("parallel"),
    )(qc, small, kv_g.reshape(1, -1), pad(idx_g), pad(idx_b))


def _attn_kernel(q_ref, qidx_ref, widx_ref, ckv_ref, ckvt_ref, kidx_ref, brow_ref, wuk_ref, wuvt_ref,
                 o_ref, bias_ref, keys, qt, qit, qlat, m_scr, l_scr, acc, ot, *, topk):
    i = pl.program_id(1)
    QB, TK, NH, C, DH, DI = ATT_QB, ATT_TK, ATT_HEADS, ATT_LATENT, ATT_HEADDIM, IDX_HEADDIM
    n_tiles = i + 1

    @pl.when(i == 0)
    def _():
        period = brow_ref.shape[1]
        for h in range(NH):
            base = jnp.broadcast_to(brow_ref[h:h + 1, :], (2 * TK, period))
            bias_ref[h] = pltpu.roll(base, 0, 1, stride=1, stride_axis=0)[:, :QB]

    qt[...] = q_ref[...].astype(F32).T.astype(BF16)
    for h in range(NH):
        ql = _dot(wuk_ref[h], qt[h * DH:(h + 1) * DH, :])
        qlat[h] = (ql * (DH ** -0.5 * LOG2E)).astype(BF16)

    qit[0:IDX_HEADS * DI, :] = qidx_ref[...].T.astype(BF16)
    qit[IDX_HEADS * DI:, :] = jnp.zeros((LANES - DI, QB), BF16)
    wt = widx_ref[...].T[:IDX_HEADS, :] * (IDX_HEADS ** -0.5 * DI ** -0.5)
    qchunk = (i * QB + lax.broadcasted_iota(jnp.int32, (TK, QB), 1)) // CHUNK
    kin = lax.broadcasted_iota(jnp.int32, (TK, QB), 0)

    def score_tile(j, _):
        kt = kidx_ref[pl.ds(pl.multiple_of(j * TK, TK), TK), :]
        s = jnp.zeros((TK, QB), F32)
        for h in range(IDX_HEADS):
            sh = _dot(kt, qit[h * DI:h * DI + LANES, :])
            s = s + wt[h:h + 1, :] * jnp.maximum(sh, 0.0)
        bits = pltpu.bitcast(s, jnp.int32)
        skey = bits ^ ((bits >> 31) & 0x7FFFFFFF)
        adm = ((j * TK + kin) // CHUNK) <= qchunk
        keys[j] = jnp.where(adm, skey, INT_MIN)
        return 0

    lax.fori_loop(0, n_tiles, score_tile, 0)

    def count_ge(cand):
        def body(j, cnt):
            hit = jnp.where(keys[j] >= cand, 1, 0)
            return cnt + jnp.sum(hit.reshape(TK // SUBLANES, SUBLANES, QB), axis=0)
        cnt = lax.fori_loop(0, n_tiles, body, jnp.zeros((SUBLANES, QB), jnp.int32))
        return jnp.sum(cnt, axis=0, keepdims=True)

    tau0 = jnp.full((1, QB), INT_MIN, jnp.int32)
    tau0 = jnp.where(count_ge(jnp.zeros((1, QB), jnp.int32)) >= topk, 0, tau0)

    def bit_step(t, tau):
        cand = tau + jnp.left_shift(jnp.int32(1), 30 - t)
        return jnp.where(count_ge(cand) >= topk, cand, tau)

    tau = lax.fori_loop(0, 31, bit_step, tau0)
    tau = jnp.maximum(tau, INT_MIN + 1)

    m_scr[...] = jnp.full(m_scr.shape, NEG_BIG, F32)
    l_scr[...] = jnp.zeros(l_scr.shape, F32)
    acc[...] = jnp.zeros(acc.shape, F32)

    def attend(j, near_half):
        kv = ckv_ref[pl.ds(pl.multiple_of(j * TK, TK), TK), :]
        kvt = ckvt_ref[j]
        maskb = jnp.where(keys[j] >= tau, 0.0, NEG_BIG)
        for h in range(NH):
            s = _dot(kv, qlat[h]) + maskb
            if near_half is not None:
                s = s + bias_ref[h, near_half * TK:(near_half + 1) * TK, :]
            m_old = m_scr[h]
            m_new = jnp.maximum(m_old, jnp.max(s, axis=0, keepdims=True))
            alpha = jnp.exp2(m_old - m_new)
            pr = jnp.exp2(s - m_new)
            l_scr[h] = alpha * l_scr[h] + jnp.sum(pr, axis=0, keepdims=True)
            acc[h] = alpha * acc[h] + _dot(kvt, pr.astype(BF16))
            m_scr[h] = m_new

    def far_tile(j, _):
        attend(j, None)
        return 0

    lax.fori_loop(0, i - 1, far_tile, 0)

    @pl.when(i > 0)
    def _():
        attend(i - 1, 0)

    attend(i, 1)

    for h in range(NH):
        ol = (acc[h] / l_scr[h]).astype(BF16)
        ot[h * DH:(h + 1) * DH, :] = _dot(wuvt_ref[h], ol)
    o_ref[...] = ot[...].T.astype(o_ref.dtype)


def _t5_bucket_np(rel):
    nb = REL_BUCKETS // 2
    max_exact = nb // 2
    n = np.abs(rel).astype(np.int64)
    nn = np.maximum(n, 1)
    sq = nn * nn
    log2_sq = np.floor(np.log2(sq.astype(np.float64))).astype(np.int64)
    log2_sq = np.where(2 ** (log2_sq + 1) <= sq, log2_sq + 1, log2_sq)
    log2_sq = np.where(2 ** log2_sq > sq, log2_sq - 1, log2_sq)
    large = np.minimum(max_exact + (log2_sq - 6), nb - 1)
    return np.where(rel > 0, nb, 0) + np.where(n < max_exact, n, large)


def _dsa_attention(qc, small, ckv_n, ckv_t, kidx_n, w_uk, w_uv, rel_bias, B, L):
    T = B * L
    QB, TK, NH, C, DH = ATT_QB, ATT_TK, ATT_HEADS, ATT_LATENT, ATT_HEADDIM
    nq = L // QB
    topk = min(TOPK_MAX, L // 4)
    n_rel = 2 * TK + QB - 1
    period = n_rel + 1
    rel_of = np.arange(n_rel) - (TK + QB - 1)
    far_bucket = int(_t5_bucket_np(np.array([-(TK + 1)]))[0])
    table = (rel_bias.astype(F32) - rel_bias[far_bucket].astype(F32)[None, :]) * LOG2E
    v = jnp.take(table, jnp.asarray(_t5_bucket_np(rel_of)), axis=0).T
    u = jnp.pad(v, ((0, 0), (0, 1)))[:, (QB - 1 - np.arange(period)) % period]
    wuk = w_uk.astype(BF16)
    wuvt = jnp.transpose(w_uv, (0, 2, 1)).astype(BF16)

    return pl.pallas_call(
        functools.partial(_attn_kernel, topk=topk),
        out_shape=jax.ShapeDtypeStruct((T, NH * DH), BF16),
        grid=(B, nq),
        in_specs=[
            pl.BlockSpec((QB, NH * DH), lambda b, i: (b * nq + i, 0)),
            pl.BlockSpec((QB, IDX_HEADS * IDX_HEADDIM), lambda b, i: (b * nq + i, 0)),
            pl.BlockSpec((QB, LANES), lambda b, i: (b * nq + i, 5)),
            pl.BlockSpec((L, C), lambda b, i: (b, 0)),
            pl.BlockSpec((L // TK, C, TK), lambda b, i: (b, 0, 0)),
            pl.BlockSpec((L, LANES), lambda b, i: (b, 0)),
            _const_spec((NH, period)),
            _const_spec((NH, C, DH)),
            _const_spec((NH, DH, C)),
        ],
        out_specs=pl.BlockSpec((QB, NH * DH), lambda b, i: (b * nq + i, 0)),
        scratch_shapes=[pltpu.VMEM((NH, 2 * TK, QB), F32),
                        pltpu.VMEM((L // TK, TK, QB), jnp.int32),
                        pltpu.VMEM((NH * DH, QB), BF16),
                        pltpu.VMEM((IDX_HEADS * IDX_HEADDIM + LANES - IDX_HEADDIM, QB), BF16),
                        pltpu.VMEM((NH, C, QB), BF16),
                        pltpu.VMEM((NH, 1, QB), F32),
                        pltpu.VMEM((NH, 1, QB), F32),
                        pltpu.VMEM((NH, C, QB), F32),
                        pltpu.VMEM((NH * DH, QB), F32)],
        compiler_params=_cparams("parallel", "arbitrary"),
    )(qc, small, small, ckv_n, ckv_t, kidx_n, u, wuk, wuvt)


def _merge_kernel(ys_ref, ya_ref, gs_ref, ga_ref, h_ref, wbs_ref, wba_ref, wo_ref, g_ref, b_ref,
                  hf_ref, hb_ref):
    sig = lambda v: 1.0 / (1.0 + jnp.exp(-v.astype(F32)))
    m = sig(gs_ref[...]) * _dot(ys_ref[...], wbs_ref[...]) + sig(ga_ref[...]) * _dot(ya_ref[...], wba_ref[...])
    mix = _dot(m.astype(BF16), wo_ref[...])
    h1 = _ln_rows(DN_ALPHA * h_ref[...] + mix, g_ref[...], b_ref[...])
    hf_ref[...] = h1
    hb_ref[...] = h1.astype(BF16)


def _merge(y_ssd, y_att, gates, h, w_br_ssd, w_br_att, w_out, g, b, tm=256):
    T, D = h.shape
    row = lambda w, j=0: pl.BlockSpec((tm, w), lambda i, j=j: (i, j))
    return pl.pallas_call(
        _merge_kernel,
        out_shape=(jax.ShapeDtypeStruct((T, D), F32), jax.ShapeDtypeStruct((T, D), BF16)),
        grid=(T // tm,),
        in_specs=[row(SSD_INNER), row(ATT_HEADS * ATT_HEADDIM), row(D, 0), row(D, 1), row(D),
                  _const_spec(w_br_ssd.shape), _const_spec(w_br_att.shape), _const_spec(w_out.shape),
                  _const_spec((1, D)), _const_spec((1, D))],
        out_specs=(row(D), row(D)),
        compiler_params=_cparams("parallel"),
    )(y_ssd, y_att, gates, gates, h, w_br_ssd, w_br_att, w_out, g.reshape(1, D), b.reshape(1, D))


FFN_COLS = 1408


def _ffn_up_kernel(h_ref, wup_ref, cw_ref, cb_ref, g_ref, upad, tail):
    c = pl.program_id(1)
    tm = h_ref.shape[0]
    HALO = SUBLANES

    @pl.when(c == 0)
    def _():
        tail[...] = jnp.zeros(tail.shape, F32)

    hb = h_ref[...]
    for cc in range(D_FF // FFN_COLS):
        halves = []
        for half in range(2):
            cols = slice(half * D_FF + cc * FFN_COLS, half * D_FF + (cc + 1) * FFN_COLS)
            upad[0:HALO, :] = tail[:, cols]
            upad[HALO:HALO + tm, :] = _dot(hb, wup_ref[:, cols])
            tail[:, cols] = upad[tm:tm + HALO, :]
            u = cb_ref[:, cols]
            for k in range(FFN_CONV):
                u = u + cw_ref[k:k + 1, cols] * upad[pl.ds(HALO - (FFN_CONV - 1) + k, tm), :]
            halves.append(u)
        g_ref[:, cc * FFN_COLS:(cc + 1) * FFN_COLS] = (_silu(halves[0]) * halves[1]).astype(BF16)


def _ffn_up(h1b, w_up, conv_w, conv_b, B, L, tm=256):
    T, D = h1b.shape
    nt = L // tm
    return pl.pallas_call(
        _ffn_up_kernel,
        out_shape=jax.ShapeDtypeStruct((T, D_FF), BF16),
        grid=(B, nt),
        in_specs=[pl.BlockSpec((tm, D), lambda b, c: (b * nt + c, 0)),
                  _const_spec(w_up.shape), _const_spec(conv_w.shape), _const_spec((1, 2 * D_FF))],
        out_specs=pl.BlockSpec((tm, D_FF), lambda b, c: (b * nt + c, 0)),
        scratch_shapes=[pltpu.VMEM((tm + SUBLANES, FFN_COLS), F32),
                        pltpu.VMEM((SUBLANES, 2 * D_FF), F32)],
        compiler_params=_cparams("parallel", "arbitrary"),
    )(h1b, w_up, conv_w, conv_b.reshape(1, -1))


def _ffn_down_kernel(g_ref, h_ref, wd_ref, lg_ref, lb_ref, o_ref):
    f = _dot(g_ref[...], wd_ref[...])
    o_ref[...] = _ln_rows(DN_ALPHA * h_ref[...] + f, lg_ref[...], lb_ref[...])


def _ffn_down(g, h1, w_down, lg, lb, tm=256):
    T, D = h1.shape
    return pl.pallas_call(
        _ffn_down_kernel,
        out_shape=jax.ShapeDtypeStruct((T, D), F32),
        grid=(T // tm,),
        in_specs=[pl.BlockSpec((tm, D_FF), lambda i: (i, 0)), pl.BlockSpec((tm, D), lambda i: (i, 0)),
                  _const_spec(w_down.shape), _const_spec((1, D)), _const_spec((1, D))],
        out_specs=pl.BlockSpec((tm, D), lambda i: (i, 0)),
        compiler_params=_cparams("parallel"),
    )(g, h1, w_down, lg.reshape(1, D), lb.reshape(1, D))


def _split_w_in(w_in):
    sizes = (SSD_INNER, SSD_XBC, SSD_HEADS, ATT_HEADS * ATT_HEADDIM, ATT_LATENT,
             IDX_HEADS * IDX_HEADDIM, IDX_HEADDIM, IDX_HEADS, D_MODEL, D_MODEL)
    pts = np.cumsum(sizes)[:-1].tolist()
    return jnp.split(w_in, pts, axis=1)


def _pad_cols(w, width):
    return jnp.pad(w, ((0, 0), (0, width - w.shape[1])))


def kernel(x, ln_in_g, ln_in_b, w_in, ssd_conv_w, ssd_conv_b, ssd_dt_bias, ssd_A_log, ssd_D, ssd_norm_g, att_kv_norm_g, att_w_uk, att_w_uv, idx_k_norm_g, idx_k_norm_b, rel_bias, w_br_ssd, w_br_att, w_out, ln1_g, ln1_b, ffn_w_up, ffn_conv_w, ffn_conv_b, ffn_w_down, ln2_g, ln2_b):
    B, L, D = x.shape
    T = B * L
    l = 0
    wz, wxbc, wdt, wq, wckv, wqi, wki, wwi, wgs, wga = _split_w_in(w_in[l])
    w_xz = jnp.concatenate([wxbc, wz], axis=1).astype(BF16)
    w_qc = jnp.concatenate([wq, wckv], axis=1).astype(BF16)
    w_g = jnp.concatenate([wgs, wga], axis=1).astype(BF16)
    w_small = jnp.concatenate([wqi, _pad_cols(wki, LANES), _pad_cols(wwi, LANES), _pad_cols(wdt, LANES)],
                              axis=1).astype(BF16)

    h, hb = _layernorm_in(x.reshape(T, D), ln_in_g, ln_in_b)
    xz = _matmul(hb, w_xz, tn=1024, out_dtype=BF16)
    qc = _matmul(hb, w_qc, tn=1280, out_dtype=BF16)
    gates = _matmul(hb, w_g, tn=1024, out_dtype=BF16)
    small = _matmul(hb, w_small, tn=896)

    y_ssd = _ssd_branch(xz, small, ssd_conv_w[l], ssd_conv_b[l], ssd_dt_bias[l], ssd_A_log[l], ssd_D[l],
                        ssd_norm_g[l], B, L)
    ckv_n, ckv_t, kidx_n = _dsa_prep(qc, small, att_kv_norm_g[l], idx_k_norm_g[l], idx_k_norm_b[l])
    y_att = _dsa_attention(qc, small, ckv_n, ckv_t, kidx_n, att_w_uk[l], att_w_uv[l], rel_bias, B, L)

    h1, h1b = _merge(y_ssd, y_att, gates, h, w_br_ssd[l].astype(BF16), w_br_att[l].astype(BF16),
                     w_out[l].astype(BF16), ln1_g[l], ln1_b[l])
    g = _ffn_up(h1b, ffn_w_up[l].astype(BF16), ffn_conv_w[l], ffn_conv_b[l], B, L)
    out = _ffn_down(g, h1, ffn_w_down[l].astype(BF16), ln2_g[l], ln2_b[l])
    return out.reshape(B, L, D).astype(x.dtype)
```

```python
import functools
import math

import numpy as np
import jax
import jax.numpy as jnp
from jax import lax
from jax.experimental import pallas as pl
from jax.experimental.pallas import tpu as pltpu

F32 = jnp.float32
BF16 = jnp.bfloat16

D_MODEL = 1024
CHUNK = 64
SSD_INNER = 2048
SSD_HEADDIM = 64
SSD_HEADS = 32
SSD_GROUPS = 4
SSD_STATE = 128
SSD_CONV = 4
SSD_XBC = SSD_INNER + 2 * SSD_GROUPS * SSD_STATE
ATT_HEADS = 16
ATT_HEADDIM = 64
ATT_LATENT = 256
IDX_HEADS = 8
IDX_HEADDIM = 64
TOPK_MAX = 256
REL_BUCKETS = 32
REL_MAX_DIST = 128
D_FF = 2816
FFN_CONV = 3
DN_ALPHA = 2.0 ** 0.25
LN_EPS = 1e-5

LANES = 128
SUBLANES = 8
VMEM_LIMIT = 56 * 1024 * 1024

SSD_Q = 128
ATT_QB = 256
ATT_TK = 256
INT_MIN = -2 ** 31
HALF16 = 2 ** 15
NEG_BIG = -1e30
LOG2E = math.log2(math.e)


def _cparams(*sem):
    return pltpu.CompilerParams(dimension_semantics=sem, vmem_limit_bytes=VMEM_LIMIT)


def _const_spec(shape):
    nd = len(shape)
    return pl.BlockSpec(shape, lambda *_: (0,) * nd, pipeline_mode=pl.Buffered(1))


def _silu(x):
    return x / (1.0 + jnp.exp(-x))


def _softplus(x):
    return jnp.maximum(x, 0.0) + jnp.log1p(jnp.exp(-jnp.abs(x)))


def _ln_rows(x, g, b):
    mu = jnp.mean(x, -1, keepdims=True)
    xc = x - mu
    var = jnp.mean(xc * xc, -1, keepdims=True)
    return xc * lax.rsqrt(var + LN_EPS) * g + b


def _dot(a, b, **kw):
    return jnp.dot(a, b, preferred_element_type=F32, **kw)


def _dot_exact01(m01, x, x_on_left):
    total = None
    r = x
    for _ in range(3):
        part = r.astype(BF16)
        r = r - part.astype(F32)
        t = _dot(part, m01) if x_on_left else _dot(m01, part)
        total = t if total is None else total + t
    return total


def _dot_nt(a, b):
    return lax.dot_general(a, b, (((1,), (1,)), ((), ())), preferred_element_type=F32)


def _dot_tn(a, b):
    return lax.dot_general(a, b, (((0,), (0,)), ((), ())), preferred_element_type=F32)


def _ln_kernel(x_ref, g_ref, b_ref, hf_ref, hb_ref):
    h = _ln_rows(x_ref[...], g_ref[...], b_ref[...])
    hf_ref[...] = h
    hb_ref[...] = h.astype(BF16)


def _layernorm_in(x2, g, b, tm=512):
    T, D = x2.shape
    row = pl.BlockSpec((tm, D), lambda i: (i, 0))
    return pl.pallas_call(
        _ln_kernel,
        out_shape=(jax.ShapeDtypeStruct((T, D), F32), jax.ShapeDtypeStruct((T, D), BF16)),
        grid=(T // tm,),
        in_specs=[row, _const_spec((1, D)), _const_spec((1, D))],
        out_specs=(row, row),
        compiler_params=_cparams("parallel"),
    )(x2, g.reshape(1, D), b.reshape(1, D))


def _mm_kernel(a_ref, w_ref, o_ref):
    o_ref[...] = _dot(a_ref[...], w_ref[...]).astype(o_ref.dtype)


def _matmul(a, w, tn, tm=512, out_dtype=F32):
    M, K = a.shape
    N = w.shape[1]
    assert M % tm == 0 and N % tn == 0
    return pl.pallas_call(
        _mm_kernel,
        out_shape=jax.ShapeDtypeStruct((M, N), out_dtype),
        grid=(N // tn, M // tm),
        in_specs=[pl.BlockSpec((tm, K), lambda j, i: (i, 0)),
                  pl.BlockSpec((K, tn), lambda j, i: (0, j))],
        out_specs=pl.BlockSpec((tm, tn), lambda j, i: (i, j)),
        compiler_params=_cparams("parallel", "parallel"),
    )(a, w)


def _ssd_kernel(xbc_ref, z0_ref, z1_ref, dt_ref, convw_ref, convb_ref, dtb_ref, dtbT_ref,
                a_ref, aT_ref, dskip_ref, ng_ref, e_ref, y_ref, xpad, state, ybuf):
    c = pl.program_id(1)
    Q = xbc_ref.shape[0]
    G, N, P = SSD_GROUPS, SSD_STATE, SSD_HEADDIM
    GW = SSD_INNER // G
    HALO = SUBLANES

    @pl.when(c == 0)
    def _():
        xpad[0:HALO, :] = jnp.zeros((HALO, SSD_XBC), F32)
        state[...] = jnp.zeros(state.shape, F32)

    @pl.when(c > 0)
    def _():
        xpad[0:HALO, :] = xpad[Q:Q + HALO, :]

    xpad[HALO:HALO + Q, :] = xbc_ref[...].astype(F32)
    conv = convb_ref[...]
    for k in range(SSD_CONV):
        conv = conv + convw_ref[k:k + 1, :] * xpad[pl.ds(HALO - (SSD_CONV - 1) + k, Q), :]
    xc = _silu(conv)
    xs = xc[:, :SSD_INNER]

    dt_blk = dt_ref[...]
    dt = _softplus(dt_blk[:, :SSD_HEADS] + dtb_ref[...])
    a = dt * a_ref[...]
    dtT = _softplus(dt_blk.T[:SSD_HEADS, :] + dtbT_ref[...])
    aT = dtT * aT_ref[...]

    row = lax.broadcasted_iota(jnp.int32, (Q, Q), 0)
    col = lax.broadcasted_iota(jnp.int32, (Q, Q), 1)
    causal = row >= col
    tril = jnp.where(causal, 1.0, 0.0).astype(BF16)
    triu = jnp.where(row <= col, 1.0, 0.0).astype(BF16)
    acum = _dot_exact01(tril, a, False)
    acumT = _dot_exact01(triu, aT, True)

    e = e_ref[...]
    dt_e = _dot_exact01(e, dt, True)
    eac_e = _dot_exact01(e, jnp.exp(acum), True)
    ds_e = _dot_exact01(e, jnp.exp(acum[Q - 1:Q, :] - acum), True)
    X = xs * dt_e
    Xb = X.astype(BF16)
    Xd = (X * ds_e).astype(BF16)
    lane = lax.broadcasted_iota(jnp.int32, (Q, LANES), 1)
    lo_half = lane < P

    for g in range(G):
        Bg = xc[:, SSD_INNER + g * N:SSD_INNER + (g + 1) * N].astype(BF16)
        Cg = xc[:, SSD_INNER + G * N + g * N:SSD_INNER + G * N + (g + 1) * N].astype(BF16)
        cb = _dot_nt(Cg, Bg)
        st = state[g]
        eg = eac_e[:, g * GW:(g + 1) * GW]
        yoff = _dot(Cg, st.astype(BF16)) * eg
        new = _dot_tn(Bg, Xd[:, g * GW:(g + 1) * GW])
        state[g] = st * eg[Q - 1:Q, :] + new
        for jj in range(GW // LANES):
            p = g * (GW // LANES) + jj
            h0 = 2 * p
            Xp = Xb[:, p * LANES:(p + 1) * LANES]
            zero = jnp.zeros_like(Xp)
            top = jnp.where(lo_half, Xp, zero)
            bot = jnp.where(lo_half, zero, Xp)
            yd = None
            for hh, rhs in ((h0, top), (h0 + 1, bot)):
                seg = acum[:, hh:hh + 1] - acumT[hh:hh + 1, :]
                decay = jnp.exp(jnp.where(causal, seg, -jnp.inf))
                part = _dot((cb * decay).astype(BF16), rhs)
                yd = part if yd is None else yd + part
            ybuf[:, p * LANES:(p + 1) * LANES] = yd + yoff[:, jj * LANES:(jj + 1) * LANES]

    y = ybuf[...] + xs * dskip_ref[...]
    for g in range(G):
        zg = (z0_ref if g < G // 2 else z1_ref)[:, (g % (G // 2)) * GW:(g % (G // 2) + 1) * GW]
        v = y[:, g * GW:(g + 1) * GW] * _silu(zg.astype(F32))
        v = v * lax.rsqrt(jnp.mean(v * v, -1, keepdims=True) + LN_EPS)
        y_ref[:, g * GW:(g + 1) * GW] = (v * ng_ref[:, g * GW:(g + 1) * GW]).astype(BF16)


def _ssd_branch(xz, small, conv_w, conv_b, dt_bias, A_log, D_skip, norm_g, B, L):
    T = B * L
    Q = SSD_Q
    nc = L // Q
    H = SSD_HEADS
    A = -jnp.exp(A_log.astype(F32))
    expand = jnp.asarray(np.kron(np.eye(H, dtype=np.float32), np.ones((1, SSD_HEADDIM), np.float32)), BF16)
    d_e = jnp.repeat(D_skip.astype(F32), SSD_HEADDIM).reshape(1, SSD_INNER)
    zw = SSD_INNER // 2
    tok = lambda w, j: pl.BlockSpec((Q, w), lambda b, c, j=j: (b * nc + c, j))
    return pl.pallas_call(
        _ssd_kernel,
        out_shape=jax.ShapeDtypeStruct((T, SSD_INNER), BF16),
        grid=(B, nc),
        in_specs=[tok(SSD_XBC, 0), tok(zw, SSD_XBC // zw), tok(zw, SSD_XBC // zw + 1), tok(LANES, 6),
                  _const_spec((SSD_CONV, SSD_XBC)), _const_spec((1, SSD_XBC)),
                  _const_spec((1, H)), _const_spec((H, 1)), _const_spec((1, H)), _const_spec((H, 1)),
                  _const_spec((1, SSD_INNER)), _const_spec((1, SSD_INNER)), _const_spec((H, SSD_INNER))],
        out_specs=pl.BlockSpec((Q, SSD_INNER), lambda b, c: (b * nc + c, 0)),
        scratch_shapes=[pltpu.VMEM((Q + 2 * SUBLANES, SSD_XBC), F32),
                        pltpu.VMEM((SSD_GROUPS, SSD_STATE, SSD_INNER // SSD_GROUPS), F32),
                        pltpu.VMEM((Q, SSD_INNER), F32)],
        compiler_params=_cparams("parallel", "arbitrary"),
    )(xz, xz, xz, small, conv_w, conv_b.reshape(1, -1), dt_bias.reshape(1, H), dt_bias.reshape(H, 1),
      A.reshape(1, H), A.reshape(H, 1), d_e, norm_g.reshape(1, -1), expand)


def _prep_kernel(ckv_ref, kidx_ref, kvg_ref, ig_ref, ib_ref, ckvn_ref, ckvt_ref, kn_ref):
    c = ckv_ref[...].astype(F32)
    cn = c * lax.rsqrt(jnp.mean(c * c, -1, keepdims=True) + LN_EPS) * kvg_ref[...]
    ckvn_ref[...] = cn.astype(BF16)
    ckvt_ref[0] = cn.T.astype(BF16)
    k = kidx_ref[...]
    lane = lax.broadcasted_iota(jnp.int32, k.shape, 1)
    live = lane < IDX_HEADDIM
    mu = jnp.sum(k, -1, keepdims=True) * (1.0 / IDX_HEADDIM)
    kc = jnp.where(live, k - mu, 0.0)
    var = jnp.sum(kc * kc, -1, keepdims=True) * (1.0 / IDX_HEADDIM)
    kn = jnp.where(live, kc * lax.rsqrt(var + LN_EPS) * ig_ref[...] + ib_ref[...], 0.0)
    kn_ref[...] = kn.astype(BF16)


def _dsa_prep(qc, small, kv_g, idx_g, idx_b):
    T = qc.shape[0]
    tm = ATT_TK
    pad = lambda v: jnp.pad(v.astype(F32), (0, LANES - IDX_HEADDIM)).reshape(1, LANES)
    return pl.pallas_call(
        _prep_kernel,
        out_shape=(jax.ShapeDtypeStruct((T, ATT_LATENT), BF16),
                   jax.ShapeDtypeStruct((T // tm, ATT_LATENT, tm), BF16),
                   jax.ShapeDtypeStruct((T, LANES), BF16)),
        grid=(T // tm,),
        in_specs=[pl.BlockSpec((tm, ATT_LATENT), lambda i: (i, 4)),
                  pl.BlockSpec((tm, LANES), lambda i: (i, 4)),
                  _const_spec((1, ATT_LATENT)), _const_spec((1, LANES)), _const_spec((1, LANES))],
        out_specs=(pl.BlockSpec((tm, ATT_LATENT), lambda i: (i, 0)),
                   pl.BlockSpec((1, ATT_LATENT, tm), lambda i: (i, 0, 0)),
                   pl.BlockSpec((tm, LANES), lambda i: (i, 0))),
        compiler_params=_cparams("parallel"),
    )(qc, small, kv_g.reshape(1, -1), pad(idx_g), pad(idx_b))


def _attn_kernel(q_ref, qidx_ref, widx_ref, ckv_ref, ckvt_ref, kidx_ref, brow_ref, wuk_ref, wuvt_ref,
                 o_ref, bias_ref, keys, khi, klo, qt, qit, qlat, m_scr, l_scr, acc, ot, *, topk):
    i = pl.program_id(1)
    QB, TK, NH, C, DH, DI = ATT_QB, ATT_TK, ATT_HEADS, ATT_LATENT, ATT_HEADDIM, IDX_HEADDIM
    n_tiles = i + 1

    @pl.when(i == 0)
    def _():
        period = brow_ref.shape[1]
        for h in range(NH):
            base = jnp.broadcast_to(brow_ref[h:h + 1, :], (2 * TK, period))
            bias_ref[h] = pltpu.roll(base, 0, 1, stride=1, stride_axis=0)[:, :QB]

    qt[...] = q_ref[...].astype(F32).T.astype(BF16)
    for h in range(NH):
        ql = _dot(wuk_ref[h], qt[h * DH:(h + 1) * DH, :])
        qlat[h] = (ql * (DH ** -0.5 * LOG2E)).astype(BF16)

    qit[0:IDX_HEADS * DI, :] = qidx_ref[...].T.astype(BF16)
    qit[IDX_HEADS * DI:, :] = jnp.zeros((LANES - DI, QB), BF16)
    wt = widx_ref[...].T[:IDX_HEADS, :] * (IDX_HEADS ** -0.5 * DI ** -0.5)
    qchunk = (i * QB + lax.broadcasted_iota(jnp.int32, (TK, QB), 1)) // CHUNK
    kin = lax.broadcasted_iota(jnp.int32, (TK, QB), 0)

    def score_tile(j, _):
        kt = kidx_ref[pl.ds(pl.multiple_of(j * TK, TK), TK), :]
        s = jnp.zeros((TK, QB), F32)
        for h in range(IDX_HEADS):
            sh = _dot(kt, qit[h * DI:h * DI + LANES, :])
            s = s + wt[h:h + 1, :] * jnp.maximum(sh, 0.0)
        bits = pltpu.bitcast(s, jnp.int32)
        skey = bits ^ ((bits >> 31) & 0x7FFFFFFF)
        adm = ((j * TK + kin) // CHUNK) <= qchunk
        skey = jnp.where(adm, skey, INT_MIN)
        keys[j] = skey
        khi[j] = (skey >> 16).astype(jnp.int16)
        klo[j] = ((skey & 0xFFFF) - HALF16).astype(jnp.int16)
        return 0

    lax.fori_loop(0, n_tiles, score_tile, 0)

    PK = 2 * SUBLANES

    def count16(ref, test):
        def body(j, cnt):
            hit = jnp.where(test(ref[j]), jnp.int16(1), jnp.int16(0))
            for r in range(TK // PK):
                cnt = cnt + hit[r * PK:(r + 1) * PK, :]
            return cnt
        cnt = lax.fori_loop(0, n_tiles, body, jnp.zeros((PK, QB), jnp.int16))
        return jnp.sum(cnt.astype(jnp.int32), axis=0, keepdims=True)

    def search16(ref, base_count):
        def bit_step(b, t):
            cand = t + jnp.left_shift(jnp.int32(1), 15 - b)
            c16 = (cand - HALF16).astype(jnp.int16)
            ok = base_count + count16(ref, lambda v: v >= c16) >= topk
            return jnp.where(ok, cand, t)
        return lax.fori_loop(0, 16, bit_step, jnp.zeros((1, QB), jnp.int32))

    zero = jnp.zeros((1, QB), jnp.int32)
    thi = search16(khi, zero)
    thi16 = (thi - HALF16).astype(jnp.int16)
    above = count16(khi, lambda v: v > thi16)

    def mark_equal(j, _):
        klo[j] = jnp.where(khi[j] == thi16, klo[j], jnp.int16(-HALF16))
        return 0

    lax.fori_loop(0, n_tiles, mark_equal, 0)
    tlo = search16(klo, above)
    tau = jnp.left_shift(thi - HALF16, 16) | tlo
    tau = jnp.maximum(tau, INT_MIN + 1)

    m_scr[...] = jnp.full(m_scr.shape, NEG_BIG, F32)
    l_scr[...] = jnp.zeros(l_scr.shape, F32)
    acc[...] = jnp.zeros(acc.shape, F32)

    def attend(j, near_half):
        kv = ckv_ref[pl.ds(pl.multiple_of(j * TK, TK), TK), :]
        kvt = ckvt_ref[j]
        maskb = jnp.where(keys[j] >= tau, 0.0, NEG_BIG)
        for h in range(NH):
            s = _dot(kv, qlat[h]) + maskb
            if near_half is not None:
                s = s + bias_ref[h, near_half * TK:(near_half + 1) * TK, :]
            m_old = m_scr[h]
            m_new = jnp.maximum(m_old, jnp.max(s, axis=0, keepdims=True))
            alpha = jnp.exp2(m_old - m_new)
            pr = jnp.exp2(s - m_new)
            l_scr[h] = alpha * l_scr[h] + jnp.sum(pr, axis=0, keepdims=True)
            acc[h] = alpha * acc[h] + _dot(kvt, pr.astype(BF16))
            m_scr[h] = m_new

    def far_tile(j, _):
        attend(j, None)
        return 0

    lax.fori_loop(0, i - 1, far_tile, 0)

    @pl.when(i > 0)
    def _():
        attend(i - 1, 0)

    attend(i, 1)

    for h in range(NH):
        ol = (acc[h] / l_scr[h]).astype(BF16)
        ot[h * DH:(h + 1) * DH, :] = _dot(wuvt_ref[h], ol)
    o_ref[...] = ot[...].T.astype(o_ref.dtype)


def _t5_bucket_np(rel):
    nb = REL_BUCKETS // 2
    max_exact = nb // 2
    n = np.abs(rel).astype(np.int64)
    nn = np.maximum(n, 1)
    sq = nn * nn
    log2_sq = np.floor(np.log2(sq.astype(np.float64))).astype(np.int64)
    log2_sq = np.where(2 ** (log2_sq + 1) <= sq, log2_sq + 1, log2_sq)
    log2_sq = np.where(2 ** log2_sq > sq, log2_sq - 1, log2_sq)
    large = np.minimum(max_exact + (log2_sq - 6), nb - 1)
    return np.where(rel > 0, nb, 0) + np.where(n < max_exact, n, large)


def _dsa_attention(qc, small, ckv_n, ckv_t, kidx_n, w_uk, w_uv, rel_bias, B, L):
    T = B * L
    QB, TK, NH, C, DH = ATT_QB, ATT_TK, ATT_HEADS, ATT_LATENT, ATT_HEADDIM
    nq = L // QB
    topk = min(TOPK_MAX, L // 4)
    n_rel = 2 * TK + QB - 1
    period = n_rel + 1
    rel_of = np.arange(n_rel) - (TK + QB - 1)
    far_bucket = int(_t5_bucket_np(np.array([-(TK + 1)]))[0])
    table = (rel_bias.astype(F32) - rel_bias[far_bucket].astype(F32)[None, :]) * LOG2E
    v = jnp.take(table, jnp.asarray(_t5_bucket_np(rel_of)), axis=0).T
    u = jnp.pad(v, ((0, 0), (0, 1)))[:, (QB - 1 - np.arange(period)) % period]
    wuk = w_uk.astype(BF16)
    wuvt = jnp.transpose(w_uv, (0, 2, 1)).astype(BF16)

    return pl.pallas_call(
        functools.partial(_attn_kernel, topk=topk),
        out_shape=jax.ShapeDtypeStruct((T, NH * DH), BF16),
        grid=(B, nq),
        in_specs=[
            pl.BlockSpec((QB, NH * DH), lambda b, i: (b * nq + i, 0)),
            pl.BlockSpec((QB, IDX_HEADS * IDX_HEADDIM), lambda b, i: (b * nq + i, 0)),
            pl.BlockSpec((QB, LANES), lambda b, i: (b * nq + i, 5)),
            pl.BlockSpec((L, C), lambda b, i: (b, 0)),
            pl.BlockSpec((L // TK, C, TK), lambda b, i: (b, 0, 0)),
            pl.BlockSpec((L, LANES), lambda b, i: (b, 0)),
            _const_spec((NH, period)),
            _const_spec((NH, C, DH)),
            _const_spec((NH, DH, C)),
        ],
        out_specs=pl.BlockSpec((QB, NH * DH), lambda b, i: (b * nq + i, 0)),
        scratch_shapes=[pltpu.VMEM((NH, 2 * TK, QB), F32),
                        pltpu.VMEM((L // TK, TK, QB), jnp.int32),
                        pltpu.VMEM((L // TK, TK, QB), jnp.int16),
                        pltpu.VMEM((L // TK, TK, QB), jnp.int16),
                        pltpu.VMEM((NH * DH, QB), BF16),
                        pltpu.VMEM((IDX_HEADS * IDX_HEADDIM + LANES - IDX_HEADDIM, QB), BF16),
                        pltpu.VMEM((NH, C, QB), BF16),
                        pltpu.VMEM((NH, 1, QB), F32),
                        pltpu.VMEM((NH, 1, QB), F32),
                        pltpu.VMEM((NH, C, QB), F32),
                        pltpu.VMEM((NH * DH, QB), F32)],
        compiler_params=_cparams("parallel", "arbitrary"),
    )(qc, small, small, ckv_n, ckv_t, kidx_n, u, wuk, wuvt)


def _merge_kernel(ys_ref, ya_ref, gs_ref, ga_ref, h_ref, wbs_ref, wba_ref, wo_ref, g_ref, b_ref,
                  hf_ref, hb_ref):
    sig = lambda v: 1.0 / (1.0 + jnp.exp(-v.astype(F32)))
    m = sig(gs_ref[...]) * _dot(ys_ref[...], wbs_ref[...]) + sig(ga_ref[...]) * _dot(ya_ref[...], wba_ref[...])
    mix = _dot(m.astype(BF16), wo_ref[...])
    h1 = _ln_rows(DN_ALPHA * h_ref[...] + mix, g_ref[...], b_ref[...])
    hf_ref[...] = h1
    hb_ref[...] = h1.astype(BF16)


def _merge(y_ssd, y_att, gates, h, w_br_ssd, w_br_att, w_out, g, b, tm=256):
    T, D = h.shape
    row = lambda w, j=0: pl.BlockSpec((tm, w), lambda i, j=j: (i, j))
    return pl.pallas_call(
        _merge_kernel,
        out_shape=(jax.ShapeDtypeStruct((T, D), F32), jax.ShapeDtypeStruct((T, D), BF16)),
        grid=(T // tm,),
        in_specs=[row(SSD_INNER), row(ATT_HEADS * ATT_HEADDIM), row(D, 0), row(D, 1), row(D),
                  _const_spec(w_br_ssd.shape), _const_spec(w_br_att.shape), _const_spec(w_out.shape),
                  _const_spec((1, D)), _const_spec((1, D))],
        out_specs=(row(D), row(D)),
        compiler_params=_cparams("parallel"),
    )(y_ssd, y_att, gates, gates, h, w_br_ssd, w_br_att, w_out, g.reshape(1, D), b.reshape(1, D))


FFN_COLS = 1408


def _ffn_up_kernel(h_ref, wup_ref, cw_ref, cb_ref, g_ref, upad, tail):
    c = pl.program_id(1)
    tm = h_ref.shape[0]
    HALO = SUBLANES

    @pl.when(c == 0)
    def _():
        tail[...] = jnp.zeros(tail.shape, F32)

    hb = h_ref[...]
    for cc in range(D_FF // FFN_COLS):
        halves = []
        for half in range(2):
            cols = slice(half * D_FF + cc * FFN_COLS, half * D_FF + (cc + 1) * FFN_COLS)
            upad[0:HALO, :] = tail[:, cols]
            upad[HALO:HALO + tm, :] = _dot(hb, wup_ref[:, cols])
            tail[:, cols] = upad[tm:tm + HALO, :]
            u = cb_ref[:, cols]
            for k in range(FFN_CONV):
                u = u + cw_ref[k:k + 1, cols] * upad[pl.ds(HALO - (FFN_CONV - 1) + k, tm), :]
            halves.append(u)
        g_ref[:, cc * FFN_COLS:(cc + 1) * FFN_COLS] = (_silu(halves[0]) * halves[1]).astype(BF16)


def _ffn_up(h1b, w_up, conv_w, conv_b, B, L, tm=256):
    T, D = h1b.shape
    nt = L // tm
    return pl.pallas_call(
        _ffn_up_kernel,
        out_shape=jax.ShapeDtypeStruct((T, D_FF), BF16),
        grid=(B, nt),
        in_specs=[pl.BlockSpec((tm, D), lambda b, c: (b * nt + c, 0)),
                  _const_spec(w_up.shape), _const_spec(conv_w.shape), _const_spec((1, 2 * D_FF))],
        out_specs=pl.BlockSpec((tm, D_FF), lambda b, c: (b * nt + c, 0)),
        scratch_shapes=[pltpu.VMEM((tm + SUBLANES, FFN_COLS), F32),
                        pltpu.VMEM((SUBLANES, 2 * D_FF), F32)],
        compiler_params=_cparams("parallel", "arbitrary"),
    )(h1b, w_up, conv_w, conv_b.reshape(1, -1))


def _ffn_down_kernel(g_ref, h_ref, wd_ref, lg_ref, lb_ref, o_ref):
    f = _dot(g_ref[...], wd_ref[...])
    o_ref[...] = _ln_rows(DN_ALPHA * h_ref[...] + f, lg_ref[...], lb_ref[...])


def _ffn_down(g, h1, w_down, lg, lb, tm=256):
    T, D = h1.shape
    return pl.pallas_call(
        _ffn_down_kernel,
        out_shape=jax.ShapeDtypeStruct((T, D), F32),
        grid=(T // tm,),
        in_specs=[pl.BlockSpec((tm, D_FF), lambda i: (i, 0)), pl.BlockSpec((tm, D), lambda i: (i, 0)),
                  _const_spec(w_down.shape), _const_spec((1, D)), _const_spec((1, D))],
        out_specs=pl.BlockSpec((tm, D), lambda i: (i, 0)),
        compiler_params=_cparams("parallel"),
    )(g, h1, w_down, lg.reshape(1, D), lb.reshape(1, D))


def _split_w_in(w_in):
    sizes = (SSD_INNER, SSD_XBC, SSD_HEADS, ATT_HEADS * ATT_HEADDIM, ATT_LATENT,
             IDX_HEADS * IDX_HEADDIM, IDX_HEADDIM, IDX_HEADS, D_MODEL, D_MODEL)
    pts = np.cumsum(sizes)[:-1].tolist()
    return jnp.split(w_in, pts, axis=1)


def _pad_cols(w, width):
    return jnp.pad(w, ((0, 0), (0, width - w.shape[1])))


def kernel(x, ln_in_g, ln_in_b, w_in, ssd_conv_w, ssd_conv_b, ssd_dt_bias, ssd_A_log, ssd_D, ssd_norm_g, att_kv_norm_g, att_w_uk, att_w_uv, idx_k_norm_g, idx_k_norm_b, rel_bias, w_br_ssd, w_br_att, w_out, ln1_g, ln1_b, ffn_w_up, ffn_conv_w, ffn_conv_b, ffn_w_down, ln2_g, ln2_b):
    B, L, D = x.shape
    T = B * L
    l = 0
    wz, wxbc, wdt, wq, wckv, wqi, wki, wwi, wgs, wga = _split_w_in(w_in[l])
    w_xz = jnp.concatenate([wxbc, wz], axis=1).astype(BF16)
    w_qc = jnp.concatenate([wq, wckv], axis=1).astype(BF16)
    w_g = jnp.concatenate([wgs, wga], axis=1).astype(BF16)
    w_small = jnp.concatenate([wqi, _pad_cols(wki, LANES), _pad_cols(wwi, LANES), _pad_cols(wdt, LANES)],
                              axis=1).astype(BF16)

    h, hb = _layernorm_in(x.reshape(T, D), ln_in_g, ln_in_b)
    xz = _matmul(hb, w_xz, tn=1024, out_dtype=BF16)
    qc = _matmul(hb, w_qc, tn=1280, out_dtype=BF16)
    gates = _matmul(hb, w_g, tn=1024, out_dtype=BF16)
    small = _matmul(hb, w_small, tn=896)

    y_ssd = _ssd_branch(xz, small, ssd_conv_w[l], ssd_conv_b[l], ssd_dt_bias[l], ssd_A_log[l], ssd_D[l],
                        ssd_norm_g[l], B, L)
    ckv_n, ckv_t, kidx_n = _dsa_prep(qc, small, att_kv_norm_g[l], idx_k_norm_g[l], idx_k_norm_b[l])
    y_att = _dsa_attention(qc, small, ckv_n, ckv_t, kidx_n, att_w_uk[l], att_w_uv[l], rel_bias, B, L)

    h1, h1b = _merge(y_ssd, y_att, gates, h, w_br_ssd[l].astype(BF16), w_br_att[l].astype(BF16),
                     w_out[l].astype(BF16), ln1_g[l], ln1_b[l])
    g = _ffn_up(h1b, ffn_w_up[l].astype(BF16), ffn_conv_w[l], ffn_conv_b[l], B, L)
    out = _ffn_down(g, h1, ffn_w_down[l].astype(BF16), ln2_g[l], ln2_b[l])
    return out.reshape(B, L, D).astype(x.dtype)
```

```python
import functools
import math

import numpy as np
import jax
import jax.numpy as jnp
from jax import lax
from jax.experimental import pallas as pl
from jax.experimental.pallas import tpu as pltpu

F32 = jnp.float32
BF16 = jnp.bfloat16

D_MODEL = 1024
CHUNK = 64
SSD_INNER = 2048
SSD_HEADDIM = 64
SSD_HEADS = 32
SSD_GROUPS = 4
SSD_STATE = 128
SSD_CONV = 4
SSD_XBC = SSD_INNER + 2 * SSD_GROUPS * SSD_STATE
ATT_HEADS = 16
ATT_HEADDIM = 64
ATT_LATENT = 256
IDX_HEADS = 8
IDX_HEADDIM = 64
TOPK_MAX = 256
REL_BUCKETS = 32
REL_MAX_DIST = 128
D_FF = 2816
FFN_CONV = 3
DN_ALPHA = 2.0 ** 0.25
LN_EPS = 1e-5

LANES = 128
SUBLANES = 8
VMEM_LIMIT = 56 * 1024 * 1024

SSD_Q = 128
ATT_QB = 256
ATT_TK = 256
INT_MIN = -2 ** 31
HALF16 = 2 ** 15
NEG_BIG = -1e30
LOG2E = math.log2(math.e)


def _cparams(*sem):
    return pltpu.CompilerParams(dimension_semantics=sem, vmem_limit_bytes=VMEM_LIMIT)


def _const_spec(shape):
    nd = len(shape)
    return pl.BlockSpec(shape, lambda *_: (0,) * nd, pipeline_mode=pl.Buffered(1))


def _silu(x):
    return x / (1.0 + jnp.exp(-x))


def _softplus(x):
    return jnp.maximum(x, 0.0) + jnp.log1p(jnp.exp(-jnp.abs(x)))


def _ln_rows(x, g, b):
    mu = jnp.mean(x, -1, keepdims=True)
    xc = x - mu
    var = jnp.mean(xc * xc, -1, keepdims=True)
    return xc * lax.rsqrt(var + LN_EPS) * g + b


def _dot(a, b, **kw):
    return jnp.dot(a, b, preferred_element_type=F32, **kw)


def _dot_exact01(m01, x, x_on_left):
    total = None
    r = x
    for _ in range(3):
        part = r.astype(BF16)
        r = r - part.astype(F32)
        t = _dot(part, m01) if x_on_left else _dot(m01, part)
        total = t if total is None else total + t
    return total


def _dot_nt(a, b):
    return lax.dot_general(a, b, (((1,), (1,)), ((), ())), preferred_element_type=F32)


def _dot_tn(a, b):
    return lax.dot_general(a, b, (((0,), (0,)), ((), ())), preferred_element_type=F32)


def _ln_kernel(x_ref, g_ref, b_ref, hf_ref, hb_ref):
    h = _ln_rows(x_ref[...], g_ref[...], b_ref[...])
    hf_ref[...] = h
    hb_ref[...] = h.astype(BF16)


def _layernorm_in(x2, g, b, tm=512):
    T, D = x2.shape
    row = pl.BlockSpec((tm, D), lambda i: (i, 0))
    return pl.pallas_call(
        _ln_kernel,
        out_shape=(jax.ShapeDtypeStruct((T, D), F32), jax.ShapeDtypeStruct((T, D), BF16)),
        grid=(T // tm,),
        in_specs=[row, _const_spec((1, D)), _const_spec((1, D))],
        out_specs=(row, row),
        compiler_params=_cparams("parallel"),
    )(x2, g.reshape(1, D), b.reshape(1, D))


def _mm_kernel(a_ref, w_ref, o_ref):
    o_ref[...] = _dot(a_ref[...], w_ref[...]).astype(o_ref.dtype)


def _matmul(a, w, tn, tm=512, out_dtype=F32):
    M, K = a.shape
    N = w.shape[1]
    assert M % tm == 0 and N % tn == 0
    return pl.pallas_call(
        _mm_kernel,
        out_shape=jax.ShapeDtypeStruct((M, N), out_dtype),
        grid=(N // tn, M // tm),
        in_specs=[pl.BlockSpec((tm, K), lambda j, i: (i, 0)),
                  pl.BlockSpec((K, tn), lambda j, i: (0, j))],
        out_specs=pl.BlockSpec((tm, tn), lambda j, i: (i, j)),
        compiler_params=_cparams("parallel", "parallel"),
    )(a, w)


def _ssd_kernel(xbc_ref, z0_ref, z1_ref, dt_ref, convw_ref, convb_ref, dtb_ref, dtbT_ref,
                a_ref, aT_ref, dskip_ref, ng_ref, e_ref, y_ref, xpad, state, ybuf):
    c = pl.program_id(1)
    Q = xbc_ref.shape[0]
    G, N, P = SSD_GROUPS, SSD_STATE, SSD_HEADDIM
    GW = SSD_INNER // G
    HALO = SUBLANES

    @pl.when(c == 0)
    def _():
        xpad[0:HALO, :] = jnp.zeros((HALO, SSD_XBC), F32)
        state[...] = jnp.zeros(state.shape, F32)

    @pl.when(c > 0)
    def _():
        xpad[0:HALO, :] = xpad[Q:Q + HALO, :]

    xpad[HALO:HALO + Q, :] = xbc_ref[...].astype(F32)
    conv = convb_ref[...]
    for k in range(SSD_CONV):
        conv = conv + convw_ref[k:k + 1, :] * xpad[pl.ds(HALO - (SSD_CONV - 1) + k, Q), :]
    xc = _silu(conv)
    xs = xc[:, :SSD_INNER]

    dt_blk = dt_ref[...]
    dt = _softplus(dt_blk[:, :SSD_HEADS] + dtb_ref[...])
    a = dt * a_ref[...]
    dtT = _softplus(dt_blk.T[:SSD_HEADS, :] + dtbT_ref[...])
    aT = dtT * aT_ref[...]

    row = lax.broadcasted_iota(jnp.int32, (Q, Q), 0)
    col = lax.broadcasted_iota(jnp.int32, (Q, Q), 1)
    causal = row >= col
    tril = jnp.where(causal, 1.0, 0.0).astype(BF16)
    triu = jnp.where(row <= col, 1.0, 0.0).astype(BF16)
    acum = _dot_exact01(tril, a, False)
    acumT = _dot_exact01(triu, aT, True)

    e = e_ref[...]
    dt_e = _dot_exact01(e, dt, True)
    eac_e = _dot_exact01(e, jnp.exp(acum), True)
    ds_e = _dot_exact01(e, jnp.exp(acum[Q - 1:Q, :] - acum), True)
    X = xs * dt_e
    Xb = X.astype(BF16)
    Xd = (X * ds_e).astype(BF16)
    lane = lax.broadcasted_iota(jnp.int32, (Q, LANES), 1)
    lo_half = lane < P

    for g in range(G):
        Bg = xc[:, SSD_INNER + g * N:SSD_INNER + (g + 1) * N].astype(BF16)
        Cg = xc[:, SSD_INNER + G * N + g * N:SSD_INNER + G * N + (g + 1) * N].astype(BF16)
        cb = _dot_nt(Cg, Bg)
        st = state[g]
        eg = eac_e[:, g * GW:(g + 1) * GW]
        yoff = _dot(Cg, st.astype(BF16)) * eg
        new = _dot_tn(Bg, Xd[:, g * GW:(g + 1) * GW])
        state[g] = st * eg[Q - 1:Q, :] + new
        for jj in range(GW // LANES):
            p = g * (GW // LANES) + jj
            h0 = 2 * p
            Xp = Xb[:, p * LANES:(p + 1) * LANES]
            zero = jnp.zeros_like(Xp)
            top = jnp.where(lo_half, Xp, zero)
            bot = jnp.where(lo_half, zero, Xp)
            yd = None
            for hh, rhs in ((h0, top), (h0 + 1, bot)):
                seg = acum[:, hh:hh + 1] - acumT[hh:hh + 1, :]
                decay = jnp.exp(jnp.where(causal, seg, -jnp.inf))
                part = _dot((cb * decay).astype(BF16), rhs)
                yd = part if yd is None else yd + part
            ybuf[:, p * LANES:(p + 1) * LANES] = yd + yoff[:, jj * LANES:(jj + 1) * LANES]

    y = ybuf[...] + xs * dskip_ref[...]
    for g in range(G):
        zg = (z0_ref if g < G // 2 else z1_ref)[:, (g % (G // 2)) * GW:(g % (G // 2) + 1) * GW]
        v = y[:, g * GW:(g + 1) * GW] * _silu(zg.astype(F32))
        v = v * lax.rsqrt(jnp.mean(v * v, -1, keepdims=True) + LN_EPS)
        y_ref[:, g * GW:(g + 1) * GW] = (v * ng_ref[:, g * GW:(g + 1) * GW]).astype(BF16)


def _ssd_branch(xz, small, conv_w, conv_b, dt_bias, A_log, D_skip, norm_g, B, L):
    T = B * L
    Q = SSD_Q
    nc = L // Q
    H = SSD_HEADS
    A = -jnp.exp(A_log.astype(F32))
    expand = jnp.asarray(np.kron(np.eye(H, dtype=np.float32), np.ones((1, SSD_HEADDIM), np.float32)), BF16)
    d_e = jnp.repeat(D_skip.astype(F32), SSD_HEADDIM).reshape(1, SSD_INNER)
    zw = SSD_INNER // 2
    tok = lambda w, j: pl.BlockSpec((Q, w), lambda b, c, j=j: (b * nc + c, j))
    return pl.pallas_call(
        _ssd_kernel,
        out_shape=jax.ShapeDtypeStruct((T, SSD_INNER), BF16),
        grid=(B, nc),
        in_specs=[tok(SSD_XBC, 0), tok(zw, SSD_XBC // zw), tok(zw, SSD_XBC // zw + 1), tok(LANES, 6),
                  _const_spec((SSD_CONV, SSD_XBC)), _const_spec((1, SSD_XBC)),
                  _const_spec((1, H)), _const_spec((H, 1)), _const_spec((1, H)), _const_spec((H, 1)),
                  _const_spec((1, SSD_INNER)), _const_spec((1, SSD_INNER)), _const_spec((H, SSD_INNER))],
        out_specs=pl.BlockSpec((Q, SSD_INNER), lambda b, c: (b * nc + c, 0)),
        scratch_shapes=[pltpu.VMEM((Q + 2 * SUBLANES, SSD_XBC), F32),
                        pltpu.VMEM((SSD_GROUPS, SSD_STATE, SSD_INNER // SSD_GROUPS), F32),
                        pltpu.VMEM((Q, SSD_INNER), F32)],
        compiler_params=_cparams("parallel", "arbitrary"),
    )(xz, xz, xz, small, conv_w, conv_b.reshape(1, -1), dt_bias.reshape(1, H), dt_bias.reshape(H, 1),
      A.reshape(1, H), A.reshape(H, 1), d_e, norm_g.reshape(1, -1), expand)


def _prep_kernel(ckv_ref, kidx_ref, kvg_ref, ig_ref, ib_ref, ckvn_ref, ckvt_ref, ckvt2_ref, kn_ref):
    c = ckv_ref[...].astype(F32)
    cn = c * lax.rsqrt(jnp.mean(c * c, -1, keepdims=True) + LN_EPS) * kvg_ref[...]
    ckvn_ref[...] = cn.astype(BF16)
    cnt = cn.T.astype(BF16)
    ckvt2_ref[0] = cnt
    ckvt_ref[0] = cnt[:, :ATT_TK]
    ckvt_ref[1] = cnt[:, ATT_TK:]
    k = kidx_ref[...]
    lane = lax.broadcasted_iota(jnp.int32, k.shape, 1)
    live = lane < IDX_HEADDIM
    mu = jnp.sum(k, -1, keepdims=True) * (1.0 / IDX_HEADDIM)
    kc = jnp.where(live, k - mu, 0.0)
    var = jnp.sum(kc * kc, -1, keepdims=True) * (1.0 / IDX_HEADDIM)
    kn = jnp.where(live, kc * lax.rsqrt(var + LN_EPS) * ig_ref[...] + ib_ref[...], 0.0)
    kn_ref[...] = kn.astype(BF16)


def _dsa_prep(qc, small, kv_g, idx_g, idx_b):
    T = qc.shape[0]
    tm = 2 * ATT_TK
    pad = lambda v: jnp.pad(v.astype(F32), (0, LANES - IDX_HEADDIM)).reshape(1, LANES)
    return pl.pallas_call(
        _prep_kernel,
        out_shape=(jax.ShapeDtypeStruct((T, ATT_LATENT), BF16),
                   jax.ShapeDtypeStruct((T // ATT_TK, ATT_LATENT, ATT_TK), BF16),
                   jax.ShapeDtypeStruct((T // tm, ATT_LATENT, tm), BF16),
                   jax.ShapeDtypeStruct((T, LANES), BF16)),
        grid=(T // tm,),
        in_specs=[pl.BlockSpec((tm, ATT_LATENT), lambda i: (i, 4)),
                  pl.BlockSpec((tm, LANES), lambda i: (i, 4)),
                  _const_spec((1, ATT_LATENT)), _const_spec((1, LANES)), _const_spec((1, LANES))],
        out_specs=(pl.BlockSpec((tm, ATT_LATENT), lambda i: (i, 0)),
                   pl.BlockSpec((2, ATT_LATENT, ATT_TK), lambda i: (i, 0, 0)),
                   pl.BlockSpec((1, ATT_LATENT, tm), lambda i: (i, 0, 0)),
                   pl.BlockSpec((tm, LANES), lambda i: (i, 0))),
        compiler_params=_cparams("parallel"),
    )(qc, small, kv_g.reshape(1, -1), pad(idx_g), pad(idx_b))


def _attn_kernel(q_ref, qidx_ref, widx_ref, ckv_ref, ckvt_ref, ckvt2_ref, kidx_ref, brow_ref, wuk_ref, wuvt_ref,
                 o_ref, bias_ref, keys, khi, klo, qt, qit, qlat, m_scr, l_scr, acc, ot, *, topk):
    i = pl.program_id(1)
    QB, TK, NH, C, DH, DI = ATT_QB, ATT_TK, ATT_HEADS, ATT_LATENT, ATT_HEADDIM, IDX_HEADDIM
    n_tiles = i + 1

    @pl.when(i == 0)
    def _():
        period = brow_ref.shape[1]
        for h in range(NH):
            base = jnp.broadcast_to(brow_ref[h:h + 1, :], (2 * TK, period))
            bias_ref[h] = pltpu.roll(base, 0, 1, stride=1, stride_axis=0)[:, :QB]

    qt[...] = q_ref[...].astype(F32).T.astype(BF16)
    for h in range(NH):
        ql = _dot(wuk_ref[h], qt[h * DH:(h + 1) * DH, :])
        qlat[h] = (ql * (DH ** -0.5 * LOG2E)).astype(BF16)

    qit[0:IDX_HEADS * DI, :] = qidx_ref[...].T.astype(BF16)
    qit[IDX_HEADS * DI:, :] = jnp.zeros((LANES - DI, QB), BF16)
    wt = widx_ref[...].T[:IDX_HEADS, :] * (IDX_HEADS ** -0.5 * DI ** -0.5)
    qchunk = (i * QB + lax.broadcasted_iota(jnp.int32, (TK, QB), 1)) // CHUNK
    kin = lax.broadcasted_iota(jnp.int32, (TK, QB), 0)

    def score_tile(j, diagonal):
        kt = kidx_ref[pl.ds(pl.multiple_of(j * TK, TK), TK), :]
        s = jnp.zeros((TK, QB), F32)
        for h in range(IDX_HEADS):
            sh = _dot(kt, qit[h * DI:h * DI + LANES, :])
            s = s + wt[h:h + 1, :] * jnp.maximum(sh, 0.0)
        bits = pltpu.bitcast(s, jnp.int32)
        skey = bits ^ ((bits >> 31) & 0x7FFFFFFF)
        if diagonal:
            adm = ((j * TK + kin) // CHUNK) <= qchunk
            skey = jnp.where(adm, skey, INT_MIN)
        keys[j] = skey
        khi[j] = (skey >> 16).astype(jnp.int16)
        klo[j] = ((skey & 0xFFFF) - HALF16).astype(jnp.int16)

    def score_earlier(j, _):
        score_tile(j, False)
        return 0

    lax.fori_loop(0, i, score_earlier, 0)
    score_tile(i, True)

    PK = 2 * SUBLANES

    def count16(ref, test):
        def body(j, cnt):
            hit = jnp.where(test(ref[j]), jnp.int16(1), jnp.int16(0))
            for r in range(TK // PK):
                cnt = cnt + hit[r * PK:(r + 1) * PK, :]
            return cnt
        cnt = lax.fori_loop(0, n_tiles, body, jnp.zeros((PK, QB), jnp.int16))
        return jnp.sum(cnt.astype(jnp.int32), axis=0, keepdims=True)

    def search16(ref, base_count):
        def bit_step(b, t):
            cand = t + jnp.left_shift(jnp.int32(1), 15 - b)
            c16 = (cand - HALF16).astype(jnp.int16)
            ok = base_count + count16(ref, lambda v: v >= c16) >= topk
            return jnp.where(ok, cand, t)
        return lax.fori_loop(0, 16, bit_step, jnp.zeros((1, QB), jnp.int32))

    zero = jnp.zeros((1, QB), jnp.int32)
    thi = search16(khi, zero)
    thi16 = (thi - HALF16).astype(jnp.int16)
    above = count16(khi, lambda v: v > thi16)

    def mark_equal(j, _):
        klo[j] = jnp.where(khi[j] == thi16, klo[j], jnp.int16(-HALF16))
        return 0

    lax.fori_loop(0, n_tiles, mark_equal, 0)
    tlo = search16(klo, above)
    tau = jnp.left_shift(thi - HALF16, 16) | tlo
    tau = jnp.maximum(tau, INT_MIN + 1)

    m_scr[...] = jnp.full(m_scr.shape, NEG_BIG, F32)
    l_scr[...] = jnp.zeros(l_scr.shape, F32)
    acc[...] = jnp.zeros(acc.shape, F32)

    def attend(kv, kvt, key_tiles, near_half):
        maskb = jnp.where(jnp.concatenate(key_tiles, axis=0) >= tau, 0.0, NEG_BIG)
        for h in range(NH):
            s = _dot(kv, qlat[h]) + maskb
            if near_half is not None:
                s = s + bias_ref[h, near_half * TK:(near_half + 1) * TK, :]
            m_old = m_scr[h]
            m_new = jnp.maximum(m_old, jnp.max(s, axis=0, keepdims=True))
            alpha = jnp.exp2(m_old - m_new)
            pr = jnp.exp2(s - m_new)
            l_scr[h] = alpha * l_scr[h] + jnp.sum(pr, axis=0, keepdims=True)
            acc[h] = alpha * acc[h] + _dot(kvt, pr.astype(BF16))
            m_scr[h] = m_new

    def attend_one(j, near_half):
        attend(ckv_ref[pl.ds(pl.multiple_of(j * TK, TK), TK), :], ckvt_ref[j], [keys[j]], near_half)

    def far_pair(jp, _):
        attend(ckv_ref[pl.ds(pl.multiple_of(jp * 2 * TK, 2 * TK), 2 * TK), :], ckvt2_ref[jp],
               [keys[2 * jp], keys[2 * jp + 1]], None)
        return 0

    n_far = jnp.maximum(i - 1, 0)
    lax.fori_loop(0, n_far // 2, far_pair, 0)

    @pl.when(n_far % 2 == 1)
    def _():
        attend_one(n_far - 1, None)

    @pl.when(i > 0)
    def _():
        attend_one(i - 1, 0)

    attend_one(i, 1)

    for h in range(NH):
        ol = (acc[h] / l_scr[h]).astype(BF16)
        ot[h * DH:(h + 1) * DH, :] = _dot(wuvt_ref[h], ol)
    o_ref[...] = ot[...].T.astype(o_ref.dtype)


def _t5_bucket_np(rel):
    nb = REL_BUCKETS // 2
    max_exact = nb // 2
    n = np.abs(rel).astype(np.int64)
    nn = np.maximum(n, 1)
    sq = nn * nn
    log2_sq = np.floor(np.log2(sq.astype(np.float64))).astype(np.int64)
    log2_sq = np.where(2 ** (log2_sq + 1) <= sq, log2_sq + 1, log2_sq)
    log2_sq = np.where(2 ** log2_sq > sq, log2_sq - 1, log2_sq)
    large = np.minimum(max_exact + (log2_sq - 6), nb - 1)
    return np.where(rel > 0, nb, 0) + np.where(n < max_exact, n, large)


def _dsa_attention(qc, small, ckv_n, ckv_t, ckv_t2, kidx_n, w_uk, w_uv, rel_bias, B, L):
    T = B * L
    QB, TK, NH, C, DH = ATT_QB, ATT_TK, ATT_HEADS, ATT_LATENT, ATT_HEADDIM
    nq = L // QB
    topk = min(TOPK_MAX, L // 4)
    n_rel = 2 * TK + QB - 1
    period = n_rel + 1
    rel_of = np.arange(n_rel) - (TK + QB - 1)
    far_bucket = int(_t5_bucket_np(np.array([-(TK + 1)]))[0])
    table = (rel_bias.astype(F32) - rel_bias[far_bucket].astype(F32)[None, :]) * LOG2E
    v = jnp.take(table, jnp.asarray(_t5_bucket_np(rel_of)), axis=0).T
    u = jnp.pad(v, ((0, 0), (0, 1)))[:, (QB - 1 - np.arange(period)) % period]
    wuk = w_uk.astype(BF16)
    wuvt = jnp.transpose(w_uv, (0, 2, 1)).astype(BF16)

    return pl.pallas_call(
        functools.partial(_attn_kernel, topk=topk),
        out_shape=jax.ShapeDtypeStruct((T, NH * DH), BF16),
        grid=(B, nq),
        in_specs=[
            pl.BlockSpec((QB, NH * DH), lambda b, i: (b * nq + i, 0)),
            pl.BlockSpec((QB, IDX_HEADS * IDX_HEADDIM), lambda b, i: (b * nq + i, 0)),
            pl.BlockSpec((QB, LANES), lambda b, i: (b * nq + i, 5)),
            pl.BlockSpec((L, C), lambda b, i: (b, 0)),
            pl.BlockSpec((L // TK, C, TK), lambda b, i: (b, 0, 0)),
            pl.BlockSpec((L // (2 * TK), C, 2 * TK), lambda b, i: (b, 0, 0)),
            pl.BlockSpec((L, LANES), lambda b, i: (b, 0)),
            _const_spec((NH, period)),
            _const_spec((NH, C, DH)),
            _const_spec((NH, DH, C)),
        ],
        out_specs=pl.BlockSpec((QB, NH * DH), lambda b, i: (b * nq + i, 0)),
        scratch_shapes=[pltpu.VMEM((NH, 2 * TK, QB), F32),
                        pltpu.VMEM((L // TK, TK, QB), jnp.int32),
                        pltpu.VMEM((L // TK, TK, QB), jnp.int16),
                        pltpu.VMEM((L // TK, TK, QB), jnp.int16),
                        pltpu.VMEM((NH * DH, QB), BF16),
                        pltpu.VMEM((IDX_HEADS * IDX_HEADDIM + LANES - IDX_HEADDIM, QB), BF16),
                        pltpu.VMEM((NH, C, QB), BF16),
                        pltpu.VMEM((NH, 1, QB), F32),
                        pltpu.VMEM((NH, 1, QB), F32),
                        pltpu.VMEM((NH, C, QB), F32),
                        pltpu.VMEM((NH * DH, QB), F32)],
        compiler_params=_cparams("parallel", "arbitrary"),
    )(qc, small, small, ckv_n, ckv_t, ckv_t2, kidx_n, u, wuk, wuvt)


def _merge_kernel(ys_ref, ya_ref, gs_ref, ga_ref, h_ref, wbs_ref, wba_ref, wo_ref, g_ref, b_ref,
                  hf_ref, hb_ref):
    sig = lambda v: 1.0 / (1.0 + jnp.exp(-v.astype(F32)))
    m = sig(gs_ref[...]) * _dot(ys_ref[...], wbs_ref[...]) + sig(ga_ref[...]) * _dot(ya_ref[...], wba_ref[...])
    mix = _dot(m.astype(BF16), wo_ref[...])
    h1 = _ln_rows(DN_ALPHA * h_ref[...] + mix, g_ref[...], b_ref[...])
    hf_ref[...] = h1
    hb_ref[...] = h1.astype(BF16)


def _merge(y_ssd, y_att, gates, h, w_br_ssd, w_br_att, w_out, g, b, tm=256):
    T, D = h.shape
    row = lambda w, j=0: pl.BlockSpec((tm, w), lambda i, j=j: (i, j))
    return pl.pallas_call(
        _merge_kernel,
        out_shape=(jax.ShapeDtypeStruct((T, D), F32), jax.ShapeDtypeStruct((T, D), BF16)),
        grid=(T // tm,),
        in_specs=[row(SSD_INNER), row(ATT_HEADS * ATT_HEADDIM), row(D, 0), row(D, 1), row(D),
                  _const_spec(w_br_ssd.shape), _const_spec(w_br_att.shape), _const_spec(w_out.shape),
                  _const_spec((1, D)), _const_spec((1, D))],
        out_specs=(row(D), row(D)),
        compiler_params=_cparams("parallel"),
    )(y_ssd, y_att, gates, gates, h, w_br_ssd, w_br_att, w_out, g.reshape(1, D), b.reshape(1, D))


FFN_COLS = 1408


def _ffn_up_kernel(h_ref, wup_ref, cw_ref, cb_ref, g_ref, upad, tail):
    c = pl.program_id(1)
    tm = h_ref.shape[0]
    HALO = SUBLANES

    @pl.when(c == 0)
    def _():
        tail[...] = jnp.zeros(tail.shape, F32)

    hb = h_ref[...]
    for cc in range(D_FF // FFN_COLS):
        halves = []
        for half in range(2):
            cols = slice(half * D_FF + cc * FFN_COLS, half * D_FF + (cc + 1) * FFN_COLS)
            upad[0:HALO, :] = tail[:, cols]
            upad[HALO:HALO + tm, :] = _dot(hb, wup_ref[:, cols])
            tail[:, cols] = upad[tm:tm + HALO, :]
            u = cb_ref[:, cols]
            for k in range(FFN_CONV):
                u = u + cw_ref[k:k + 1, cols] * upad[pl.ds(HALO - (FFN_CONV - 1) + k, tm), :]
            halves.append(u)
        g_ref[:, cc * FFN_COLS:(cc + 1) * FFN_COLS] = (_silu(halves[0]) * halves[1]).astype(BF16)


def _ffn_up(h1b, w_up, conv_w, conv_b, B, L, tm=256):
    T, D = h1b.shape
    nt = L // tm
    return pl.pallas_call(
        _ffn_up_kernel,
        out_shape=jax.ShapeDtypeStruct((T, D_FF), BF16),
        grid=(B, nt),
        in_specs=[pl.BlockSpec((tm, D), lambda b, c: (b * nt + c, 0)),
                  _const_spec(w_up.shape), _const_spec(conv_w.shape), _const_spec((1, 2 * D_FF))],
        out_specs=pl.BlockSpec((tm, D_FF), lambda b, c: (b * nt + c, 0)),
        scratch_shapes=[pltpu.VMEM((tm + SUBLANES, FFN_COLS), F32),
                        pltpu.VMEM((SUBLANES, 2 * D_FF), F32)],
        compiler_params=_cparams("parallel", "arbitrary"),
    )(h1b, w_up, conv_w, conv_b.reshape(1, -1))


def _ffn_down_kernel(g_ref, h_ref, wd_ref, lg_ref, lb_ref, o_ref):
    f = _dot(g_ref[...], wd_ref[...])
    o_ref[...] = _ln_rows(DN_ALPHA * h_ref[...] + f, lg_ref[...], lb_ref[...])


def _ffn_down(g, h1, w_down, lg, lb, tm=256):
    T, D = h1.shape
    return pl.pallas_call(
        _ffn_down_kernel,
        out_shape=jax.ShapeDtypeStruct((T, D), F32),
        grid=(T // tm,),
        in_specs=[pl.BlockSpec((tm, D_FF), lambda i: (i, 0)), pl.BlockSpec((tm, D), lambda i: (i, 0)),
                  _const_spec(w_down.shape), _const_spec((1, D)), _const_spec((1, D))],
        out_specs=pl.BlockSpec((tm, D), lambda i: (i, 0)),
        compiler_params=_cparams("parallel"),
    )(g, h1, w_down, lg.reshape(1, D), lb.reshape(1, D))


def _split_w_in(w_in):
    sizes = (SSD_INNER, SSD_XBC, SSD_HEADS, ATT_HEADS * ATT_HEADDIM, ATT_LATENT,
             IDX_HEADS * IDX_HEADDIM, IDX_HEADDIM, IDX_HEADS, D_MODEL, D_MODEL)
    pts = np.cumsum(sizes)[:-1].tolist()
    return jnp.split(w_in, pts, axis=1)


def _pad_cols(w, width):
    return jnp.pad(w, ((0, 0), (0, width - w.shape[1])))


def kernel(x, ln_in_g, ln_in_b, w_in, ssd_conv_w, ssd_conv_b, ssd_dt_bias, ssd_A_log, ssd_D, ssd_norm_g, att_kv_norm_g, att_w_uk, att_w_uv, idx_k_norm_g, idx_k_norm_b, rel_bias, w_br_ssd, w_br_att, w_out, ln1_g, ln1_b, ffn_w_up, ffn_conv_w, ffn_conv_b, ffn_w_down, ln2_g, ln2_b):
    B, L, D = x.shape
    T = B * L
    l = 0
    wz, wxbc, wdt, wq, wckv, wqi, wki, wwi, wgs, wga = _split_w_in(w_in[l])
    w_xz = jnp.concatenate([wxbc, wz], axis=1).astype(BF16)
    w_qc = jnp.concatenate([wq, wckv], axis=1).astype(BF16)
    w_g = jnp.concatenate([wgs, wga], axis=1).astype(BF16)
    w_small = jnp.concatenate([wqi, _pad_cols(wki, LANES), _pad_cols(wwi, LANES), _pad_cols(wdt, LANES)],
                              axis=1).astype(BF16)

    h, hb = _layernorm_in(x.reshape(T, D), ln_in_g, ln_in_b)
    xz = _matmul(hb, w_xz, tn=1024, out_dtype=BF16)
    qc = _matmul(hb, w_qc, tn=1280, out_dtype=BF16)
    gates = _matmul(hb, w_g, tn=1024, out_dtype=BF16)
    small = _matmul(hb, w_small, tn=896)

    y_ssd = _ssd_branch(xz, small, ssd_conv_w[l], ssd_conv_b[l], ssd_dt_bias[l], ssd_A_log[l], ssd_D[l],
                        ssd_norm_g[l], B, L)
    ckv_n, ckv_t, ckv_t2, kidx_n = _dsa_prep(qc, small, att_kv_norm_g[l], idx_k_norm_g[l], idx_k_norm_b[l])
    y_att = _dsa_attention(qc, small, ckv_n, ckv_t, ckv_t2, kidx_n, att_w_uk[l], att_w_uv[l], rel_bias, B, L)

    h1, h1b = _merge(y_ssd, y_att, gates, h, w_br_ssd[l].astype(BF16), w_br_att[l].astype(BF16),
                     w_out[l].astype(BF16), ln1_g[l], ln1_b[l])
    g = _ffn_up(h1b, ffn_w_up[l].astype(BF16), ffn_conv_w[l], ffn_conv_b[l], B, L)
    out = _ffn_down(g, h1, ffn_w_down[l].astype(BF16), ln2_g[l], ln2_b[l])
    return out.reshape(B, L, D).astype(x.dtype)
```

```python
import functools
import math

import numpy as np
import jax
import jax.numpy as jnp
from jax import lax
from jax.experimental import pallas as pl
from jax.experimental.pallas import tpu as pltpu

F32 = jnp.float32
BF16 = jnp.bfloat16

D_MODEL = 1024
CHUNK = 64
SSD_INNER = 2048
SSD_HEADDIM = 64
SSD_HEADS = 32
SSD_GROUPS = 4
SSD_STATE = 128
SSD_CONV = 4
SSD_XBC = SSD_INNER + 2 * SSD_GROUPS * SSD_STATE
ATT_HEADS = 16
ATT_HEADDIM = 64
ATT_LATENT = 256
IDX_HEADS = 8
IDX_HEADDIM = 64
TOPK_MAX = 256
REL_BUCKETS = 32
REL_MAX_DIST = 128
D_FF = 2816
FFN_CONV = 3
DN_ALPHA = 2.0 ** 0.25
LN_EPS = 1e-5

LANES = 128
SUBLANES = 8
VMEM_LIMIT = 56 * 1024 * 1024

SSD_Q = 128
ATT_QB = 256
ATT_TK = 256
INT_MIN = -2 ** 31
HALF16 = 2 ** 15
NEG_BIG = -1e30
LOG2E = math.log2(math.e)


def _cparams(*sem):
    return pltpu.CompilerParams(dimension_semantics=sem, vmem_limit_bytes=VMEM_LIMIT)


def _const_spec(shape):
    nd = len(shape)
    return pl.BlockSpec(shape, lambda *_: (0,) * nd, pipeline_mode=pl.Buffered(1))


def _silu(x):
    return x / (1.0 + jnp.exp(-x))


def _softplus(x):
    return jnp.maximum(x, 0.0) + jnp.log1p(jnp.exp(-jnp.abs(x)))


def _ln_rows(x, g, b):
    mu = jnp.mean(x, -1, keepdims=True)
    xc = x - mu
    var = jnp.mean(xc * xc, -1, keepdims=True)
    return xc * lax.rsqrt(var + LN_EPS) * g + b


def _dot(a, b, **kw):
    return jnp.dot(a, b, preferred_element_type=F32, **kw)


def _rows_back(win, d, halo, n):
    shifted = win if d == 0 else pltpu.roll(win, d, 0)
    return shifted[halo:halo + n, :]


def _dot_exact01(m01, x, x_on_left):
    total = None
    r = x
    for _ in range(3):
        part = r.astype(BF16)
        r = r - part.astype(F32)
        t = _dot(part, m01) if x_on_left else _dot(m01, part)
        total = t if total is None else total + t
    return total


def _dot_nt(a, b):
    return lax.dot_general(a, b, (((1,), (1,)), ((), ())), preferred_element_type=F32)


def _dot_tn(a, b):
    return lax.dot_general(a, b, (((0,), (0,)), ((), ())), preferred_element_type=F32)


def _ln_kernel(x_ref, g_ref, b_ref, hf_ref, hb_ref):
    h = _ln_rows(x_ref[...], g_ref[...], b_ref[...])
    hf_ref[...] = h
    hb_ref[...] = h.astype(BF16)


def _layernorm_in(x2, g, b, tm=512):
    T, D = x2.shape
    row = pl.BlockSpec((tm, D), lambda i: (i, 0))
    return pl.pallas_call(
        _ln_kernel,
        out_shape=(jax.ShapeDtypeStruct((T, D), F32), jax.ShapeDtypeStruct((T, D), BF16)),
        grid=(T // tm,),
        in_specs=[row, _const_spec((1, D)), _const_spec((1, D))],
        out_specs=(row, row),
        compiler_params=_cparams("parallel"),
    )(x2, g.reshape(1, D), b.reshape(1, D))


def _mm_kernel(a_ref, w_ref, o_ref):
    o_ref[...] = _dot(a_ref[...], w_ref[...]).astype(o_ref.dtype)


def _matmul(a, w, tn, tm=512, out_dtype=F32):
    M, K = a.shape
    N = w.shape[1]
    assert M % tm == 0 and N % tn == 0
    return pl.pallas_call(
        _mm_kernel,
        out_shape=jax.ShapeDtypeStruct((M, N), out_dtype),
        grid=(N // tn, M // tm),
        in_specs=[pl.BlockSpec((tm, K), lambda j, i: (i, 0)),
                  pl.BlockSpec((K, tn), lambda j, i: (0, j))],
        out_specs=pl.BlockSpec((tm, tn), lambda j, i: (i, j)),
        compiler_params=_cparams("parallel", "parallel"),
    )(a, w)


def _ssd_kernel(xbc_ref, z0_ref, z1_ref, dt_ref, convw_ref, convb_ref, dtb_ref, dtbT_ref,
                a_ref, aT_ref, dskip_ref, ng_ref, e_ref, y_ref, xpad, state, ybuf):
    c = pl.program_id(1)
    Q = xbc_ref.shape[0]
    G, N, P = SSD_GROUPS, SSD_STATE, SSD_HEADDIM
    GW = SSD_INNER // G
    HALO = SUBLANES

    @pl.when(c == 0)
    def _():
        xpad[0:HALO, :] = jnp.zeros((HALO, SSD_XBC), F32)
        state[...] = jnp.zeros(state.shape, F32)

    @pl.when(c > 0)
    def _():
        xpad[0:HALO, :] = xpad[Q:Q + HALO, :]

    xpad[HALO:HALO + Q, :] = xbc_ref[...].astype(F32)
    conv = convb_ref[...]
    xwin = xpad[0:HALO + Q, :]
    for k in range(SSD_CONV):
        conv = conv + convw_ref[k:k + 1, :] * _rows_back(xwin, SSD_CONV - 1 - k, HALO, Q)
    xc = _silu(conv)
    xs = xc[:, :SSD_INNER]

    dt_blk = dt_ref[...]
    dt = _softplus(dt_blk[:, :SSD_HEADS] + dtb_ref[...])
    a = dt * a_ref[...]
    dtT = _softplus(dt_blk.T[:SSD_HEADS, :] + dtbT_ref[...])
    aT = dtT * aT_ref[...]

    row = lax.broadcasted_iota(jnp.int32, (Q, Q), 0)
    col = lax.broadcasted_iota(jnp.int32, (Q, Q), 1)
    causal = row >= col
    tril = jnp.where(causal, 1.0, 0.0).astype(BF16)
    triu = jnp.where(row <= col, 1.0, 0.0).astype(BF16)
    acum = _dot_exact01(tril, a, False)
    acumT = _dot_exact01(triu, aT, True)

    e = e_ref[...]
    dt_e = _dot_exact01(e, dt, True)
    eac_e = _dot_exact01(e, jnp.exp(acum), True)
    ds_e = _dot_exact01(e, jnp.exp(acum[Q - 1:Q, :] - acum), True)
    X = xs * dt_e
    Xb = X.astype(BF16)
    Xd = (X * ds_e).astype(BF16)
    lane = lax.broadcasted_iota(jnp.int32, (Q, LANES), 1)
    lo_half = lane < P

    for g in range(G):
        Bg = xc[:, SSD_INNER + g * N:SSD_INNER + (g + 1) * N].astype(BF16)
        Cg = xc[:, SSD_INNER + G * N + g * N:SSD_INNER + G * N + (g + 1) * N].astype(BF16)
        cb = _dot_nt(Cg, Bg)
        st = state[g]
        eg = eac_e[:, g * GW:(g + 1) * GW]
        yoff = _dot(Cg, st.astype(BF16)) * eg
        new = _dot_tn(Bg, Xd[:, g * GW:(g + 1) * GW])
        state[g] = st * eg[Q - 1:Q, :] + new
        for jj in range(GW // LANES):
            p = g * (GW // LANES) + jj
            h0 = 2 * p
            Xp = Xb[:, p * LANES:(p + 1) * LANES]
            zero = jnp.zeros_like(Xp)
            top = jnp.where(lo_half, Xp, zero)
            bot = jnp.where(lo_half, zero, Xp)
            yd = None
            for hh, rhs in ((h0, top), (h0 + 1, bot)):
                seg = acum[:, hh:hh + 1] - acumT[hh:hh + 1, :]
                decay = jnp.exp(jnp.where(causal, seg, -jnp.inf))
                part = _dot((cb * decay).astype(BF16), rhs)
                yd = part if yd is None else yd + part
            ybuf[:, p * LANES:(p + 1) * LANES] = yd + yoff[:, jj * LANES:(jj + 1) * LANES]

    y = ybuf[...] + xs * dskip_ref[...]
    for g in range(G):
        zg = (z0_ref if g < G // 2 else z1_ref)[:, (g % (G // 2)) * GW:(g % (G // 2) + 1) * GW]
        v = y[:, g * GW:(g + 1) * GW] * _silu(zg.astype(F32))
        v = v * lax.rsqrt(jnp.mean(v * v, -1, keepdims=True) + LN_EPS)
        y_ref[:, g * GW:(g + 1) * GW] = (v * ng_ref[:, g * GW:(g + 1) * GW]).astype(BF16)


def _ssd_branch(xz, small, conv_w, conv_b, dt_bias, A_log, D_skip, norm_g, B, L):
    T = B * L
    Q = SSD_Q
    nc = L // Q
    H = SSD_HEADS
    A = -jnp.exp(A_log.astype(F32))
    expand = jnp.asarray(np.kron(np.eye(H, dtype=np.float32), np.ones((1, SSD_HEADDIM), np.float32)), BF16)
    d_e = jnp.repeat(D_skip.astype(F32), SSD_HEADDIM).reshape(1, SSD_INNER)
    zw = SSD_INNER // 2
    tok = lambda w, j: pl.BlockSpec((Q, w), lambda b, c, j=j: (b * nc + c, j))
    return pl.pallas_call(
        _ssd_kernel,
        out_shape=jax.ShapeDtypeStruct((T, SSD_INNER), BF16),
        grid=(B, nc),
        in_specs=[tok(SSD_XBC, 0), tok(zw, SSD_XBC // zw), tok(zw, SSD_XBC // zw + 1), tok(LANES, 6),
                  _const_spec((SSD_CONV, SSD_XBC)), _const_spec((1, SSD_XBC)),
                  _const_spec((1, H)), _const_spec((H, 1)), _const_spec((1, H)), _const_spec((H, 1)),
                  _const_spec((1, SSD_INNER)), _const_spec((1, SSD_INNER)), _const_spec((H, SSD_INNER))],
        out_specs=pl.BlockSpec((Q, SSD_INNER), lambda b, c: (b * nc + c, 0)),
        scratch_shapes=[pltpu.VMEM((Q + 2 * SUBLANES, SSD_XBC), F32),
                        pltpu.VMEM((SSD_GROUPS, SSD_STATE, SSD_INNER // SSD_GROUPS), F32),
                        pltpu.VMEM((Q, SSD_INNER), F32)],
        compiler_params=_cparams("parallel", "arbitrary"),
    )(xz, xz, xz, small, conv_w, conv_b.reshape(1, -1), dt_bias.reshape(1, H), dt_bias.reshape(H, 1),
      A.reshape(1, H), A.reshape(H, 1), d_e, norm_g.reshape(1, -1), expand)


def _prep_kernel(ckv_ref, kidx_ref, kvg_ref, ig_ref, ib_ref, ckvn_ref, ckvt_ref, ckvt2_ref, kn_ref):
    c = ckv_ref[...].astype(F32)
    cn = c * lax.rsqrt(jnp.mean(c * c, -1, keepdims=True) + LN_EPS) * kvg_ref[...]
    ckvn_ref[...] = cn.astype(BF16)
    cnt = cn.T.astype(BF16)
    ckvt2_ref[0] = cnt
    ckvt_ref[0] = cnt[:, :ATT_TK]
    ckvt_ref[1] = cnt[:, ATT_TK:]
    k = kidx_ref[...]
    lane = lax.broadcasted_iota(jnp.int32, k.shape, 1)
    live = lane < IDX_HEADDIM
    mu = jnp.sum(k, -1, keepdims=True) * (1.0 / IDX_HEADDIM)
    kc = jnp.where(live, k - mu, 0.0)
    var = jnp.sum(kc * kc, -1, keepdims=True) * (1.0 / IDX_HEADDIM)
    kn = jnp.where(live, kc * lax.rsqrt(var + LN_EPS) * ig_ref[...] + ib_ref[...], 0.0)
    kn_ref[...] = kn.astype(BF16)


def _dsa_prep(qc, small, kv_g, idx_g, idx_b):
    T = qc.shape[0]
    tm = 2 * ATT_TK
    pad = lambda v: jnp.pad(v.astype(F32), (0, LANES - IDX_HEADDIM)).reshape(1, LANES)
    return pl.pallas_call(
        _prep_kernel,
        out_shape=(jax.ShapeDtypeStruct((T, ATT_LATENT), BF16),
                   jax.ShapeDtypeStruct((T // ATT_TK, ATT_LATENT, ATT_TK), BF16),
                   jax.ShapeDtypeStruct((T // tm, ATT_LATENT, tm), BF16),
                   jax.ShapeDtypeStruct((T, LANES), BF16)),
        grid=(T // tm,),
        in_specs=[pl.BlockSpec((tm, ATT_LATENT), lambda i: (i, 4)),
                  pl.BlockSpec((tm, LANES), lambda i: (i, 4)),
                  _const_spec((1, ATT_LATENT)), _const_spec((1, LANES)), _const_spec((1, LANES))],
        out_specs=(pl.BlockSpec((tm, ATT_LATENT), lambda i: (i, 0)),
                   pl.BlockSpec((2, ATT_LATENT, ATT_TK), lambda i: (i, 0, 0)),
                   pl.BlockSpec((1, ATT_LATENT, tm), lambda i: (i, 0, 0)),
                   pl.BlockSpec((tm, LANES), lambda i: (i, 0))),
        compiler_params=_cparams("parallel"),
    )(qc, small, kv_g.reshape(1, -1), pad(idx_g), pad(idx_b))


def _attn_kernel(q_ref, qidx_ref, widx_ref, ckv_ref, ckvt_ref, ckvt2_ref, kidx_ref, brow_ref, wuk_ref, wuvt_ref,
                 o_ref, bias_ref, keys, khi, klo, qt, qit, qlat, m_scr, l_scr, acc, ot, *, topk):
    i = pl.program_id(1)
    QB, TK, NH, C, DH, DI = ATT_QB, ATT_TK, ATT_HEADS, ATT_LATENT, ATT_HEADDIM, IDX_HEADDIM
    n_tiles = i + 1

    @pl.when(i == 0)
    def _():
        period = brow_ref.shape[1]
        for h in range(NH):
            base = jnp.broadcast_to(brow_ref[h:h + 1, :], (2 * TK, period))
            bias_ref[h] = pltpu.roll(base, 0, 1, stride=1, stride_axis=0)[:, :QB]

    qt[...] = q_ref[...].astype(F32).T.astype(BF16)
    for h in range(NH):
        ql = _dot(wuk_ref[h], qt[h * DH:(h + 1) * DH, :])
        qlat[h] = (ql * (DH ** -0.5 * LOG2E)).astype(BF16)

    qit[0:IDX_HEADS * DI, :] = qidx_ref[...].T.astype(BF16)
    qit[IDX_HEADS * DI:, :] = jnp.zeros((LANES - DI, QB), BF16)
    wt = widx_ref[...].T[:IDX_HEADS, :] * (IDX_HEADS ** -0.5 * DI ** -0.5)
    qchunk = (i * QB + lax.broadcasted_iota(jnp.int32, (TK, QB), 1)) // CHUNK
    kin = lax.broadcasted_iota(jnp.int32, (TK, QB), 0)

    def score_tile(j, diagonal):
        kt = kidx_ref[pl.ds(pl.multiple_of(j * TK, TK), TK), :]
        s = jnp.zeros((TK, QB), F32)
        for h in range(IDX_HEADS):
            sh = _dot(kt, qit[h * DI:h * DI + LANES, :])
            s = s + wt[h:h + 1, :] * jnp.maximum(sh, 0.0)
        bits = pltpu.bitcast(s, jnp.int32)
        skey = bits ^ ((bits >> 31) & 0x7FFFFFFF)
        if diagonal:
            adm = ((j * TK + kin) // CHUNK) <= qchunk
            skey = jnp.where(adm, skey, INT_MIN)
        keys[j] = skey
        khi[j] = (skey >> 16).astype(jnp.int16)
        klo[j] = ((skey & 0xFFFF) - HALF16).astype(jnp.int16)

    def score_earlier(j, _):
        score_tile(j, False)
        return 0

    lax.fori_loop(0, i, score_earlier, 0)
    score_tile(i, True)

    PK = 2 * SUBLANES

    def count16(ref, test):
        def body(j, cnt):
            hit = jnp.where(test(ref[j]), jnp.int16(1), jnp.int16(0))
            for r in range(TK // PK):
                cnt = cnt + hit[r * PK:(r + 1) * PK, :]
            return cnt
        cnt = lax.fori_loop(0, n_tiles, body, jnp.zeros((PK, QB), jnp.int16))
        return jnp.sum(cnt.astype(jnp.int32), axis=0, keepdims=True)

    def search16(ref, base_count):
        def bit_step(b, t):
            cand = t + jnp.left_shift(jnp.int32(1), 15 - b)
            c16 = (cand - HALF16).astype(jnp.int16)
            ok = base_count + count16(ref, lambda v: v >= c16) >= topk
            return jnp.where(ok, cand, t)
        return lax.fori_loop(0, 16, bit_step, jnp.zeros((1, QB), jnp.int32))

    zero = jnp.zeros((1, QB), jnp.int32)
    thi = search16(khi, zero)
    thi16 = (thi - HALF16).astype(jnp.int16)
    above = count16(khi, lambda v: v > thi16)

    def mark_equal(j, _):
        klo[j] = jnp.where(khi[j] == thi16, klo[j], jnp.int16(-HALF16))
        return 0

    lax.fori_loop(0, n_tiles, mark_equal, 0)
    tlo = search16(klo, above)
    tau = jnp.left_shift(thi - HALF16, 16) | tlo
    tau = jnp.maximum(tau, INT_MIN + 1)

    m_scr[...] = jnp.full(m_scr.shape, NEG_BIG, F32)
    l_scr[...] = jnp.zeros(l_scr.shape, F32)
    acc[...] = jnp.zeros(acc.shape, F32)

    def attend(kv, kvt, key_tiles, near_half):
        maskb = jnp.where(jnp.concatenate(key_tiles, axis=0) >= tau, 0.0, NEG_BIG)
        for h in range(NH):
            s = _dot(kv, qlat[h]) + maskb
            if near_half is not None:
                s = s + bias_ref[h, near_half * TK:(near_half + 1) * TK, :]
            m_old = m_scr[h]
            m_new = jnp.maximum(m_old, jnp.max(s, axis=0, keepdims=True))
            alpha = jnp.exp2(m_old - m_new)
            pr = jnp.exp2(s - m_new)
            l_scr[h] = alpha * l_scr[h] + jnp.sum(pr, axis=0, keepdims=True)
            acc[h] = alpha * acc[h] + _dot(kvt, pr.astype(BF16))
            m_scr[h] = m_new

    def attend_one(j, near_half):
        attend(ckv_ref[pl.ds(pl.multiple_of(j * TK, TK), TK), :], ckvt_ref[j], [keys[j]], near_half)

    def far_pair(jp, _):
        attend(ckv_ref[pl.ds(pl.multiple_of(jp * 2 * TK, 2 * TK), 2 * TK), :], ckvt2_ref[jp],
               [keys[2 * jp], keys[2 * jp + 1]], None)
        return 0

    n_far = jnp.maximum(i - 1, 0)
    lax.fori_loop(0, n_far // 2, far_pair, 0)

    @pl.when(n_far % 2 == 1)
    def _():
        attend_one(n_far - 1, None)

    @pl.when(i > 0)
    def _():
        attend_one(i - 1, 0)

    attend_one(i, 1)

    for h in range(NH):
        ol = (acc[h] / l_scr[h]).astype(BF16)
        ot[h * DH:(h + 1) * DH, :] = _dot(wuvt_ref[h], ol)
    o_ref[...] = ot[...].T.astype(o_ref.dtype)


def _t5_bucket_np(rel):
    nb = REL_BUCKETS // 2
    max_exact = nb // 2
    n = np.abs(rel).astype(np.int64)
    nn = np.maximum(n, 1)
    sq = nn * nn
    log2_sq = np.floor(np.log2(sq.astype(np.float64))).astype(np.int64)
    log2_sq = np.where(2 ** (log2_sq + 1) <= sq, log2_sq + 1, log2_sq)
    log2_sq = np.where(2 ** log2_sq > sq, log2_sq - 1, log2_sq)
    large = np.minimum(max_exact + (log2_sq - 6), nb - 1)
    return np.where(rel > 0, nb, 0) + np.where(n < max_exact, n, large)


def _dsa_attention(qc, small, ckv_n, ckv_t, ckv_t2, kidx_n, w_uk, w_uv, rel_bias, B, L):
    T = B * L
    QB, TK, NH, C, DH = ATT_QB, ATT_TK, ATT_HEADS, ATT_LATENT, ATT_HEADDIM
    nq = L // QB
    topk = min(TOPK_MAX, L // 4)
    n_rel = 2 * TK + QB - 1
    period = n_rel + 1
    rel_of = np.arange(n_rel) - (TK + QB - 1)
    far_bucket = int(_t5_bucket_np(np.array([-(TK + 1)]))[0])
    table = (rel_bias.astype(F32) - rel_bias[far_bucket].astype(F32)[None, :]) * LOG2E
    v = jnp.take(table, jnp.asarray(_t5_bucket_np(rel_of)), axis=0).T
    u = jnp.pad(v, ((0, 0), (0, 1)))[:, (QB - 1 - np.arange(period)) % period]
    wuk = w_uk.astype(BF16)
    wuvt = jnp.transpose(w_uv, (0, 2, 1)).astype(BF16)

    return pl.pallas_call(
        functools.partial(_attn_kernel, topk=topk),
        out_shape=jax.ShapeDtypeStruct((T, NH * DH), BF16),
        grid=(B, nq),
        in_specs=[
            pl.BlockSpec((QB, NH * DH), lambda b, i: (b * nq + i, 0)),
            pl.BlockSpec((QB, IDX_HEADS * IDX_HEADDIM), lambda b, i: (b * nq + i, 0)),
            pl.BlockSpec((QB, LANES), lambda b, i: (b * nq + i, 5)),
            pl.BlockSpec((L, C), lambda b, i: (b, 0)),
            pl.BlockSpec((L // TK, C, TK), lambda b, i: (b, 0, 0)),
            pl.BlockSpec((L // (2 * TK), C, 2 * TK), lambda b, i: (b, 0, 0)),
            pl.BlockSpec((L, LANES), lambda b, i: (b, 0)),
            _const_spec((NH, period)),
            _const_spec((NH, C, DH)),
            _const_spec((NH, DH, C)),
        ],
        out_specs=pl.BlockSpec((QB, NH * DH), lambda b, i: (b * nq + i, 0)),
        scratch_shapes=[pltpu.VMEM((NH, 2 * TK, QB), F32),
                        pltpu.VMEM((L // TK, TK, QB), jnp.int32),
                        pltpu.VMEM((L // TK, TK, QB), jnp.int16),
                        pltpu.VMEM((L // TK, TK, QB), jnp.int16),
                        pltpu.VMEM((NH * DH, QB), BF16),
                        pltpu.VMEM((IDX_HEADS * IDX_HEADDIM + LANES - IDX_HEADDIM, QB), BF16),
                        pltpu.VMEM((NH, C, QB), BF16),
                        pltpu.VMEM((NH, 1, QB), F32),
                        pltpu.VMEM((NH, 1, QB), F32),
                        pltpu.VMEM((NH, C, QB), F32),
                        pltpu.VMEM((NH * DH, QB), F32)],
        compiler_params=_cparams("parallel", "arbitrary"),
    )(qc, small, small, ckv_n, ckv_t, ckv_t2, kidx_n, u, wuk, wuvt)


def _merge_kernel(ys_ref, ya_ref, gs_ref, ga_ref, h_ref, wbs_ref, wba_ref, wo_ref, g_ref, b_ref,
                  hf_ref, hb_ref):
    sig = lambda v: 1.0 / (1.0 + jnp.exp(-v.astype(F32)))
    m = sig(gs_ref[...]) * _dot(ys_ref[...], wbs_ref[...]) + sig(ga_ref[...]) * _dot(ya_ref[...], wba_ref[...])
    mix = _dot(m.astype(BF16), wo_ref[...])
    h1 = _ln_rows(DN_ALPHA * h_ref[...] + mix, g_ref[...], b_ref[...])
    hf_ref[...] = h1
    hb_ref[...] = h1.astype(BF16)


def _merge(y_ssd, y_att, gates, h, w_br_ssd, w_br_att, w_out, g, b, tm=256):
    T, D = h.shape
    row = lambda w, j=0: pl.BlockSpec((tm, w), lambda i, j=j: (i, j))
    return pl.pallas_call(
        _merge_kernel,
        out_shape=(jax.ShapeDtypeStruct((T, D), F32), jax.ShapeDtypeStruct((T, D), BF16)),
        grid=(T // tm,),
        in_specs=[row(SSD_INNER), row(ATT_HEADS * ATT_HEADDIM), row(D, 0), row(D, 1), row(D),
                  _const_spec(w_br_ssd.shape), _const_spec(w_br_att.shape), _const_spec(w_out.shape),
                  _const_spec((1, D)), _const_spec((1, D))],
        out_specs=(row(D), row(D)),
        compiler_params=_cparams("parallel"),
    )(y_ssd, y_att, gates, gates, h, w_br_ssd, w_br_att, w_out, g.reshape(1, D), b.reshape(1, D))


FFN_COLS = 1408


def _ffn_up_kernel(h_ref, wup_ref, cw_ref, cb_ref, g_ref, upad, tail):
    c = pl.program_id(1)
    tm = h_ref.shape[0]
    HALO = SUBLANES

    @pl.when(c == 0)
    def _():
        tail[...] = jnp.zeros(tail.shape, F32)

    hb = h_ref[...]
    for cc in range(D_FF // FFN_COLS):
        halves = []
        for half in range(2):
            cols = slice(half * D_FF + cc * FFN_COLS, half * D_FF + (cc + 1) * FFN_COLS)
            upad[0:HALO, :] = tail[:, cols]
            upad[HALO:HALO + tm, :] = _dot(hb, wup_ref[:, cols])
            tail[:, cols] = upad[tm:tm + HALO, :]
            u = cb_ref[:, cols]
            uwin = upad[...]
            for k in range(FFN_CONV):
                u = u + cw_ref[k:k + 1, cols] * _rows_back(uwin, FFN_CONV - 1 - k, HALO, tm)
            halves.append(u)
        g_ref[:, cc * FFN_COLS:(cc + 1) * FFN_COLS] = (_silu(halves[0]) * halves[1]).astype(BF16)


def _ffn_up(h1b, w_up, conv_w, conv_b, B, L, tm=256):
    T, D = h1b.shape
    nt = L // tm
    return pl.pallas_call(
        _ffn_up_kernel,
        out_shape=jax.ShapeDtypeStruct((T, D_FF), BF16),
        grid=(B, nt),
        in_specs=[pl.BlockSpec((tm, D), lambda b, c: (b * nt + c, 0)),
                  _const_spec(w_up.shape), _const_spec(conv_w.shape), _const_spec((1, 2 * D_FF))],
        out_specs=pl.BlockSpec((tm, D_FF), lambda b, c: (b * nt + c, 0)),
        scratch_shapes=[pltpu.VMEM((tm + SUBLANES, FFN_COLS), F32),
                        pltpu.VMEM((SUBLANES, 2 * D_FF), F32)],
        compiler_params=_cparams("parallel", "arbitrary"),
    )(h1b, w_up, conv_w, conv_b.reshape(1, -1))


def _ffn_down_kernel(g_ref, h_ref, wd_ref, lg_ref, lb_ref, o_ref):
    f = _dot(g_ref[...], wd_ref[...])
    o_ref[...] = _ln_rows(DN_ALPHA * h_ref[...] + f, lg_ref[...], lb_ref[...])


def _ffn_down(g, h1, w_down, lg, lb, tm=256):
    T, D = h1.shape
    return pl.pallas_call(
        _ffn_down_kernel,
        out_shape=jax.ShapeDtypeStruct((T, D), F32),
        grid=(T // tm,),
        in_specs=[pl.BlockSpec((tm, D_FF), lambda i: (i, 0)), pl.BlockSpec((tm, D), lambda i: (i, 0)),
                  _const_spec(w_down.shape), _const_spec((1, D)), _const_spec((1, D))],
        out_specs=pl.BlockSpec((tm, D), lambda i: (i, 0)),
        compiler_params=_cparams("parallel"),
    )(g, h1, w_down, lg.reshape(1, D), lb.reshape(1, D))


def _split_w_in(w_in):
    sizes = (SSD_INNER, SSD_XBC, SSD_HEADS, ATT_HEADS * ATT_HEADDIM, ATT_LATENT,
             IDX_HEADS * IDX_HEADDIM, IDX_HEADDIM, IDX_HEADS, D_MODEL, D_MODEL)
    pts = np.cumsum(sizes)[:-1].tolist()
    return jnp.split(w_in, pts, axis=1)


def _pad_cols(w, width):
    return jnp.pad(w, ((0, 0), (0, width - w.shape[1])))


def kernel(x, ln_in_g, ln_in_b, w_in, ssd_conv_w, ssd_conv_b, ssd_dt_bias, ssd_A_log, ssd_D, ssd_norm_g, att_kv_norm_g, att_w_uk, att_w_uv, idx_k_norm_g, idx_k_norm_b, rel_bias, w_br_ssd, w_br_att, w_out, ln1_g, ln1_b, ffn_w_up, ffn_conv_w, ffn_conv_b, ffn_w_down, ln2_g, ln2_b):
    B, L, D = x.shape
    T = B * L
    l = 0
    wz, wxbc, wdt, wq, wckv, wqi, wki, wwi, wgs, wga = _split_w_in(w_in[l])
    w_xz = jnp.concatenate([wxbc, wz], axis=1).astype(BF16)
    w_qc = jnp.concatenate([wq, wckv], axis=1).astype(BF16)
    w_g = jnp.concatenate([wgs, wga], axis=1).astype(BF16)
    w_small = jnp.concatenate([wqi, _pad_cols(wki, LANES), _pad_cols(wwi, LANES), _pad_cols(wdt, LANES)],
                              axis=1).astype(BF16)

    h, hb = _layernorm_in(x.reshape(T, D), ln_in_g, ln_in_b)
    xz = _matmul(hb, w_xz, tn=1024, out_dtype=BF16)
    qc = _matmul(hb, w_qc, tn=1280, out_dtype=BF16)
    gates = _matmul(hb, w_g, tn=1024, out_dtype=BF16)
    small = _matmul(hb, w_small, tn=896)

    y_ssd = _ssd_branch(xz, small, ssd_conv_w[l], ssd_conv_b[l], ssd_dt_bias[l], ssd_A_log[l], ssd_D[l],
                        ssd_norm_g[l], B, L)
    ckv_n, ckv_t, ckv_t2, kidx_n = _dsa_prep(qc, small, att_kv_norm_g[l], idx_k_norm_g[l], idx_k_norm_b[l])
    y_att = _dsa_attention(qc, small, ckv_n, ckv_t, ckv_t2, kidx_n, att_w_uk[l], att_w_uv[l], rel_bias, B, L)

    h1, h1b = _merge(y_ssd, y_att, gates, h, w_br_ssd[l].astype(BF16), w_br_att[l].astype(BF16),
                     w_out[l].astype(BF16), ln1_g[l], ln1_b[l])
    g = _ffn_up(h1b, ffn_w_up[l].astype(BF16), ffn_conv_w[l], ffn_conv_b[l], B, L)
    out = _ffn_down(g, h1, ffn_w_down[l].astype(BF16), ln2_g[l], ln2_b[l])
    return out.reshape(B, L, D).astype(x.dtype)
```

```python
import functools
import math

import numpy as np
import jax
import jax.numpy as jnp
from jax import lax
from jax.experimental import pallas as pl
from jax.experimental.pallas import tpu as pltpu

F32 = jnp.float32
BF16 = jnp.bfloat16

D_MODEL = 1024
CHUNK = 64
SSD_INNER = 2048
SSD_HEADDIM = 64
SSD_HEADS = 32
SSD_GROUPS = 4
SSD_STATE = 128
SSD_CONV = 4
SSD_XBC = SSD_INNER + 2 * SSD_GROUPS * SSD_STATE
ATT_HEADS = 16
ATT_HEADDIM = 64
ATT_LATENT = 256
IDX_HEADS = 8
IDX_HEADDIM = 64
TOPK_MAX = 256
REL_BUCKETS = 32
REL_MAX_DIST = 128
D_FF = 2816
FFN_CONV = 3
DN_ALPHA = 2.0 ** 0.25
LN_EPS = 1e-5

LANES = 128
SUBLANES = 8
VMEM_LIMIT = 56 * 1024 * 1024

SSD_Q = 128
ATT_QB = 256
ATT_TK = 256
INT_MIN = -2 ** 31
HALF16 = 2 ** 15
NEG_BIG = -1e30
LOG2E = math.log2(math.e)


def _cparams(*sem):
    return pltpu.CompilerParams(dimension_semantics=sem, vmem_limit_bytes=VMEM_LIMIT)


def _const_spec(shape):
    nd = len(shape)
    return pl.BlockSpec(shape, lambda *_: (0,) * nd, pipeline_mode=pl.Buffered(1))


def _silu(x):
    return x / (1.0 + jnp.exp(-x))


def _softplus(x):
    return jnp.maximum(x, 0.0) + jnp.log1p(jnp.exp(-jnp.abs(x)))


def _ln_rows(x, g, b):
    mu = jnp.mean(x, -1, keepdims=True)
    xc = x - mu
    var = jnp.mean(xc * xc, -1, keepdims=True)
    return xc * lax.rsqrt(var + LN_EPS) * g + b


def _dot(a, b, **kw):
    return jnp.dot(a, b, preferred_element_type=F32, **kw)


def _rows_back(win, d, halo, n):
    shifted = win if d == 0 else pltpu.roll(win, d, 0)
    return shifted[halo:halo + n, :]


def _dot_exact01(m01, x, x_on_left):
    total = None
    r = x
    for _ in range(3):
        part = r.astype(BF16)
        r = r - part.astype(F32)
        t = _dot(part, m01) if x_on_left else _dot(m01, part)
        total = t if total is None else total + t
    return total


def _dot_nt(a, b):
    return lax.dot_general(a, b, (((1,), (1,)), ((), ())), preferred_element_type=F32)


def _dot_tn(a, b):
    return lax.dot_general(a, b, (((0,), (0,)), ((), ())), preferred_element_type=F32)


def _inproj_kernel(x_ref, g_ref, b_ref, *refs):
    n = (len(refs) - 1) // 2
    w_refs, h_ref, o_refs = refs[:n], refs[n], refs[n + 1:]
    h = _ln_rows(x_ref[...], g_ref[...], b_ref[...])
    h_ref[...] = h
    hb = h.astype(BF16)
    for w_ref, o_ref in zip(w_refs, o_refs):
        o_ref[...] = _dot(hb, w_ref[...]).astype(o_ref.dtype)


def _ln_inproj(x2, g, b, weights, out_dtypes, tm=256):
    T, D = x2.shape
    row = lambda w: pl.BlockSpec((tm, w), lambda i: (i, 0))
    widths = [w.shape[1] for w in weights]
    return pl.pallas_call(
        _inproj_kernel,
        out_shape=(jax.ShapeDtypeStruct((T, D), F32),
                   *[jax.ShapeDtypeStruct((T, n), dt) for n, dt in zip(widths, out_dtypes)]),
        grid=(T // tm,),
        in_specs=[row(D), _const_spec((1, D)), _const_spec((1, D)), *[_const_spec(w.shape) for w in weights]],
        out_specs=(row(D), *[row(n) for n in widths]),
        compiler_params=_cparams("parallel"),
    )(x2, g.reshape(1, D), b.reshape(1, D), *weights)


def _ssd_kernel(xbc_ref, z0_ref, z1_ref, dt_ref, convw_ref, convb_ref, dtb_ref, dtbT_ref,
                a_ref, aT_ref, dskip_ref, ng_ref, e_ref, y_ref, xpad, state, ybuf):
    c = pl.program_id(1)
    Q = xbc_ref.shape[0]
    G, N, P = SSD_GROUPS, SSD_STATE, SSD_HEADDIM
    GW = SSD_INNER // G
    HALO = SUBLANES

    @pl.when(c == 0)
    def _():
        xpad[0:HALO, :] = jnp.zeros((HALO, SSD_XBC), F32)
        state[...] = jnp.zeros(state.shape, F32)

    @pl.when(c > 0)
    def _():
        xpad[0:HALO, :] = xpad[Q:Q + HALO, :]

    xpad[HALO:HALO + Q, :] = xbc_ref[...].astype(F32)
    conv = convb_ref[...]
    xwin = xpad[0:HALO + Q, :]
    for k in range(SSD_CONV):
        conv = conv + convw_ref[k:k + 1, :] * _rows_back(xwin, SSD_CONV - 1 - k, HALO, Q)
    xc = _silu(conv)
    xs = xc[:, :SSD_INNER]

    dt_blk = dt_ref[...]
    dt = _softplus(dt_blk[:, :SSD_HEADS] + dtb_ref[...])
    a = dt * a_ref[...]
    dtT = _softplus(dt_blk.T[:SSD_HEADS, :] + dtbT_ref[...])
    aT = dtT * aT_ref[...]

    row = lax.broadcasted_iota(jnp.int32, (Q, Q), 0)
    col = lax.broadcasted_iota(jnp.int32, (Q, Q), 1)
    causal = row >= col
    tril = jnp.where(causal, 1.0, 0.0).astype(BF16)
    triu = jnp.where(row <= col, 1.0, 0.0).astype(BF16)
    acum = _dot_exact01(tril, a, False)
    acumT = _dot_exact01(triu, aT, True)

    e = e_ref[...]
    dt_e = _dot_exact01(e, dt, True)
    eac_e = _dot_exact01(e, jnp.exp(acum), True)
    ds_e = _dot_exact01(e, jnp.exp(acum[Q - 1:Q, :] - acum), True)
    X = xs * dt_e
    Xb = X.astype(BF16)
    Xd = (X * ds_e).astype(BF16)
    lane = lax.broadcasted_iota(jnp.int32, (Q, LANES), 1)
    lo_half = lane < P

    for g in range(G):
        Bg = xc[:, SSD_INNER + g * N:SSD_INNER + (g + 1) * N].astype(BF16)
        Cg = xc[:, SSD_INNER + G * N + g * N:SSD_INNER + G * N + (g + 1) * N].astype(BF16)
        cb = _dot_nt(Cg, Bg)
        st = state[g]
        eg = eac_e[:, g * GW:(g + 1) * GW]
        yoff = _dot(Cg, st.astype(BF16)) * eg
        new = _dot_tn(Bg, Xd[:, g * GW:(g + 1) * GW])
        state[g] = st * eg[Q - 1:Q, :] + new
        for jj in range(GW // LANES):
            p = g * (GW // LANES) + jj
            h0 = 2 * p
            Xp = Xb[:, p * LANES:(p + 1) * LANES]
            zero = jnp.zeros_like(Xp)
            top = jnp.where(lo_half, Xp, zero)
            bot = jnp.where(lo_half, zero, Xp)
            yd = None
            for hh, rhs in ((h0, top), (h0 + 1, bot)):
                seg = acum[:, hh:hh + 1] - acumT[hh:hh + 1, :]
                decay = jnp.exp(jnp.where(causal, seg, -jnp.inf))
                part = _dot((cb * decay).astype(BF16), rhs)
                yd = part if yd is None else yd + part
            ybuf[:, p * LANES:(p + 1) * LANES] = yd + yoff[:, jj * LANES:(jj + 1) * LANES]

    y = ybuf[...] + xs * dskip_ref[...]
    for g in range(G):
        zg = (z0_ref if g < G // 2 else z1_ref)[:, (g % (G // 2)) * GW:(g % (G // 2) + 1) * GW]
        v = y[:, g * GW:(g + 1) * GW] * _silu(zg.astype(F32))
        v = v * lax.rsqrt(jnp.mean(v * v, -1, keepdims=True) + LN_EPS)
        y_ref[:, g * GW:(g + 1) * GW] = (v * ng_ref[:, g * GW:(g + 1) * GW]).astype(BF16)


def _ssd_branch(xz, small, conv_w, conv_b, dt_bias, A_log, D_skip, norm_g, B, L):
    T = B * L
    Q = SSD_Q
    nc = L // Q
    H = SSD_HEADS
    A = -jnp.exp(A_log.astype(F32))
    expand = jnp.asarray(np.kron(np.eye(H, dtype=np.float32), np.ones((1, SSD_HEADDIM), np.float32)), BF16)
    d_e = jnp.repeat(D_skip.astype(F32), SSD_HEADDIM).reshape(1, SSD_INNER)
    zw = SSD_INNER // 2
    tok = lambda w, j: pl.BlockSpec((Q, w), lambda b, c, j=j: (b * nc + c, j))
    return pl.pallas_call(
        _ssd_kernel,
        out_shape=jax.ShapeDtypeStruct((T, SSD_INNER), BF16),
        grid=(B, nc),
        in_specs=[tok(SSD_XBC, 0), tok(zw, SSD_XBC // zw), tok(zw, SSD_XBC // zw + 1), tok(LANES, 6),
                  _const_spec((SSD_CONV, SSD_XBC)), _const_spec((1, SSD_XBC)),
                  _const_spec((1, H)), _const_spec((H, 1)), _const_spec((1, H)), _const_spec((H, 1)),
                  _const_spec((1, SSD_INNER)), _const_spec((1, SSD_INNER)), _const_spec((H, SSD_INNER))],
        out_specs=pl.BlockSpec((Q, SSD_INNER), lambda b, c: (b * nc + c, 0)),
        scratch_shapes=[pltpu.VMEM((Q + 2 * SUBLANES, SSD_XBC), F32),
                        pltpu.VMEM((SSD_GROUPS, SSD_STATE, SSD_INNER // SSD_GROUPS), F32),
                        pltpu.VMEM((Q, SSD_INNER), F32)],
        compiler_params=_cparams("parallel", "arbitrary"),
    )(xz, xz, xz, small, conv_w, conv_b.reshape(1, -1), dt_bias.reshape(1, H), dt_bias.reshape(H, 1),
      A.reshape(1, H), A.reshape(H, 1), d_e, norm_g.reshape(1, -1), expand)


def _prep_kernel(ckv_ref, kidx_ref, kvg_ref, ig_ref, ib_ref, ckvn_ref, ckvt_ref, ckvt2_ref, kn_ref):
    c = ckv_ref[...].astype(F32)
    cn = c * lax.rsqrt(jnp.mean(c * c, -1, keepdims=True) + LN_EPS) * kvg_ref[...]
    ckvn_ref[...] = cn.astype(BF16)
    cnt = cn.T.astype(BF16)
    ckvt2_ref[0] = cnt
    ckvt_ref[0] = cnt[:, :ATT_TK]
    ckvt_ref[1] = cnt[:, ATT_TK:]
    k = kidx_ref[...]
    lane = lax.broadcasted_iota(jnp.int32, k.shape, 1)
    live = lane < IDX_HEADDIM
    mu = jnp.sum(k, -1, keepdims=True) * (1.0 / IDX_HEADDIM)
    kc = jnp.where(live, k - mu, 0.0)
    var = jnp.sum(kc * kc, -1, keepdims=True) * (1.0 / IDX_HEADDIM)
    kn = jnp.where(live, kc * lax.rsqrt(var + LN_EPS) * ig_ref[...] + ib_ref[...], 0.0)
    kn_ref[...] = kn.astype(BF16)


def _dsa_prep(qc, small, kv_g, idx_g, idx_b):
    T = qc.shape[0]
    tm = 2 * ATT_TK
    pad = lambda v: jnp.pad(v.astype(F32), (0, LANES - IDX_HEADDIM)).reshape(1, LANES)
    return pl.pallas_call(
        _prep_kernel,
        out_shape=(jax.ShapeDtypeStruct((T, ATT_LATENT), BF16),
                   jax.ShapeDtypeStruct((T // ATT_TK, ATT_LATENT, ATT_TK), BF16),
                   jax.ShapeDtypeStruct((T // tm, ATT_LATENT, tm), BF16),
                   jax.ShapeDtypeStruct((T, LANES), BF16)),
        grid=(T // tm,),
        in_specs=[pl.BlockSpec((tm, ATT_LATENT), lambda i: (i, 4)),
                  pl.BlockSpec((tm, LANES), lambda i: (i, 4)),
                  _const_spec((1, ATT_LATENT)), _const_spec((1, LANES)), _const_spec((1, LANES))],
        out_specs=(pl.BlockSpec((tm, ATT_LATENT), lambda i: (i, 0)),
                   pl.BlockSpec((2, ATT_LATENT, ATT_TK), lambda i: (i, 0, 0)),
                   pl.BlockSpec((1, ATT_LATENT, tm), lambda i: (i, 0, 0)),
                   pl.BlockSpec((tm, LANES), lambda i: (i, 0))),
        compiler_params=_cparams("parallel"),
    )(qc, small, kv_g.reshape(1, -1), pad(idx_g), pad(idx_b))


def _attn_kernel(q_ref, qidx_ref, widx_ref, ckv_ref, ckvt_ref, ckvt2_ref, kidx_ref, brow_ref, wuk_ref, wuvt_ref,
                 o_ref, bias_ref, keys, khi, klo, qt, qit, qlat, m_scr, l_scr, acc, ot, *, topk):
    i = pl.program_id(1)
    QB, TK, NH, C, DH, DI = ATT_QB, ATT_TK, ATT_HEADS, ATT_LATENT, ATT_HEADDIM, IDX_HEADDIM
    n_tiles = i + 1

    @pl.when(i == 0)
    def _():
        period = brow_ref.shape[1]
        for h in range(NH):
            base = jnp.broadcast_to(brow_ref[h:h + 1, :], (2 * TK, period))
            bias_ref[h] = pltpu.roll(base, 0, 1, stride=1, stride_axis=0)[:, :QB]

    qt[...] = q_ref[...].astype(F32).T.astype(BF16)
    for h in range(NH):
        ql = _dot(wuk_ref[h], qt[h * DH:(h + 1) * DH, :])
        qlat[h] = (ql * (DH ** -0.5 * LOG2E)).astype(BF16)

    qit[0:IDX_HEADS * DI, :] = qidx_ref[...].T.astype(BF16)
    qit[IDX_HEADS * DI:, :] = jnp.zeros((LANES - DI, QB), BF16)
    wt = widx_ref[...].T[:IDX_HEADS, :] * (IDX_HEADS ** -0.5 * DI ** -0.5)
    qchunk = (i * QB + lax.broadcasted_iota(jnp.int32, (TK, QB), 1)) // CHUNK
    kin = lax.broadcasted_iota(jnp.int32, (TK, QB), 0)

    def score_tile(j, diagonal):
        kt = kidx_ref[pl.ds(pl.multiple_of(j * TK, TK), TK), :]
        s = jnp.zeros((TK, QB), F32)
        for h in range(IDX_HEADS):
            sh = _dot(kt, qit[h * DI:h * DI + LANES, :])
            s = s + wt[h:h + 1, :] * jnp.maximum(sh, 0.0)
        bits = pltpu.bitcast(s, jnp.int32)
        skey = bits ^ ((bits >> 31) & 0x7FFFFFFF)
        if diagonal:
            adm = ((j * TK + kin) // CHUNK) <= qchunk
            skey = jnp.where(adm, skey, INT_MIN)
        keys[j] = skey
        khi[j] = (skey >> 16).astype(jnp.int16)
        klo[j] = ((skey & 0xFFFF) - HALF16).astype(jnp.int16)

    def score_earlier(j, _):
        score_tile(j, False)
        return 0

    lax.fori_loop(0, i, score_earlier, 0)
    score_tile(i, True)

    PK = 2 * SUBLANES

    def count16(ref, test):
        def body(j, cnt):
            hit = jnp.where(test(ref[j]), jnp.int16(1), jnp.int16(0))
            for r in range(TK // PK):
                cnt = cnt + hit[r * PK:(r + 1) * PK, :]
            return cnt
        cnt = lax.fori_loop(0, n_tiles, body, jnp.zeros((PK, QB), jnp.int16))
        return jnp.sum(cnt.astype(jnp.int32), axis=0, keepdims=True)

    def search16(ref, base_count):
        def bit_step(b, t):
            cand = t + jnp.left_shift(jnp.int32(1), 15 - b)
            c16 = (cand - HALF16).astype(jnp.int16)
            ok = base_count + count16(ref, lambda v: v >= c16) >= topk
            return jnp.where(ok, cand, t)
        return lax.fori_loop(0, 16, bit_step, jnp.zeros((1, QB), jnp.int32))

    zero = jnp.zeros((1, QB), jnp.int32)
    thi = search16(khi, zero)
    thi16 = (thi - HALF16).astype(jnp.int16)
    above = count16(khi, lambda v: v > thi16)

    def mark_equal(j, _):
        klo[j] = jnp.where(khi[j] == thi16, klo[j], jnp.int16(-HALF16))
        return 0

    lax.fori_loop(0, n_tiles, mark_equal, 0)
    tlo = search16(klo, above)
    tau = jnp.left_shift(thi - HALF16, 16) | tlo
    tau = jnp.maximum(tau, INT_MIN + 1)

    m_scr[...] = jnp.full(m_scr.shape, NEG_BIG, F32)
    l_scr[...] = jnp.zeros(l_scr.shape, F32)
    acc[...] = jnp.zeros(acc.shape, F32)

    def attend(kv, kvt, key_tiles, near_half):
        maskb = jnp.where(jnp.concatenate(key_tiles, axis=0) >= tau, 0.0, NEG_BIG)
        for h in range(NH):
            s = _dot(kv, qlat[h]) + maskb
            if near_half is not None:
                s = s + bias_ref[h, near_half * TK:(near_half + 1) * TK, :]
            m_old = m_scr[h]
            m_new = jnp.maximum(m_old, jnp.max(s, axis=0, keepdims=True))
            alpha = jnp.exp2(m_old - m_new)
            pr = jnp.exp2(s - m_new)
            l_scr[h] = alpha * l_scr[h] + jnp.sum(pr, axis=0, keepdims=True)
            acc[h] = alpha * acc[h] + _dot(kvt, pr.astype(BF16))
            m_scr[h] = m_new

    def attend_one(j, near_half):
        attend(ckv_ref[pl.ds(pl.multiple_of(j * TK, TK), TK), :], ckvt_ref[j], [keys[j]], near_half)

    def far_pair(jp, _):
        attend(ckv_ref[pl.ds(pl.multiple_of(jp * 2 * TK, 2 * TK), 2 * TK), :], ckvt2_ref[jp],
               [keys[2 * jp], keys[2 * jp + 1]], None)
        return 0

    n_far = jnp.maximum(i - 1, 0)
    lax.fori_loop(0, n_far // 2, far_pair, 0)

    @pl.when(n_far % 2 == 1)
    def _():
        attend_one(n_far - 1, None)

    @pl.when(i > 0)
    def _():
        attend_one(i - 1, 0)

    attend_one(i, 1)

    for h in range(NH):
        ol = (acc[h] / l_scr[h]).astype(BF16)
        ot[h * DH:(h + 1) * DH, :] = _dot(wuvt_ref[h], ol)
    o_ref[...] = ot[...].T.astype(o_ref.dtype)


def _t5_bucket_np(rel):
    nb = REL_BUCKETS // 2
    max_exact = nb // 2
    n = np.abs(rel).astype(np.int64)
    nn = np.maximum(n, 1)
    sq = nn * nn
    log2_sq = np.floor(np.log2(sq.astype(np.float64))).astype(np.int64)
    log2_sq = np.where(2 ** (log2_sq + 1) <= sq, log2_sq + 1, log2_sq)
    log2_sq = np.where(2 ** log2_sq > sq, log2_sq - 1, log2_sq)
    large = np.minimum(max_exact + (log2_sq - 6), nb - 1)
    return np.where(rel > 0, nb, 0) + np.where(n < max_exact, n, large)


def _dsa_attention(qc, small, ckv_n, ckv_t, ckv_t2, kidx_n, w_uk, w_uv, rel_bias, B, L):
    T = B * L
    QB, TK, NH, C, DH = ATT_QB, ATT_TK, ATT_HEADS, ATT_LATENT, ATT_HEADDIM
    nq = L // QB
    topk = min(TOPK_MAX, L // 4)
    n_rel = 2 * TK + QB - 1
    period = n_rel + 1
    rel_of = np.arange(n_rel) - (TK + QB - 1)
    far_bucket = int(_t5_bucket_np(np.array([-(TK + 1)]))[0])
    table = (rel_bias.astype(F32) - rel_bias[far_bucket].astype(F32)[None, :]) * LOG2E
    v = jnp.take(table, jnp.asarray(_t5_bucket_np(rel_of)), axis=0).T
    u = jnp.pad(v, ((0, 0), (0, 1)))[:, (QB - 1 - np.arange(period)) % period]
    wuk = w_uk.astype(BF16)
    wuvt = jnp.transpose(w_uv, (0, 2, 1)).astype(BF16)

    return pl.pallas_call(
        functools.partial(_attn_kernel, topk=topk),
        out_shape=jax.ShapeDtypeStruct((T, NH * DH), BF16),
        grid=(B, nq),
        in_specs=[
            pl.BlockSpec((QB, NH * DH), lambda b, i: (b * nq + i, 0)),
            pl.BlockSpec((QB, IDX_HEADS * IDX_HEADDIM), lambda b, i: (b * nq + i, 0)),
            pl.BlockSpec((QB, LANES), lambda b, i: (b * nq + i, 5)),
            pl.BlockSpec((L, C), lambda b, i: (b, 0)),
            pl.BlockSpec((L // TK, C, TK), lambda b, i: (b, 0, 0)),
            pl.BlockSpec((L // (2 * TK), C, 2 * TK), lambda b, i: (b, 0, 0)),
            pl.BlockSpec((L, LANES), lambda b, i: (b, 0)),
            _const_spec((NH, period)),
            _const_spec((NH, C, DH)),
            _const_spec((NH, DH, C)),
        ],
        out_specs=pl.BlockSpec((QB, NH * DH), lambda b, i: (b * nq + i, 0)),
        scratch_shapes=[pltpu.VMEM((NH, 2 * TK, QB), F32),
                        pltpu.VMEM((L // TK, TK, QB), jnp.int32),
                        pltpu.VMEM((L // TK, TK, QB), jnp.int16),
                        pltpu.VMEM((L // TK, TK, QB), jnp.int16),
                        pltpu.VMEM((NH * DH, QB), BF16),
                        pltpu.VMEM((IDX_HEADS * IDX_HEADDIM + LANES - IDX_HEADDIM, QB), BF16),
                        pltpu.VMEM((NH, C, QB), BF16),
                        pltpu.VMEM((NH, 1, QB), F32),
                        pltpu.VMEM((NH, 1, QB), F32),
                        pltpu.VMEM((NH, C, QB), F32),
                        pltpu.VMEM((NH * DH, QB), F32)],
        compiler_params=_cparams("parallel", "arbitrary"),
    )(qc, small, small, ckv_n, ckv_t, ckv_t2, kidx_n, u, wuk, wuvt)


def _merge_kernel(ys_ref, ya_ref, gs_ref, ga_ref, h_ref, wbs_ref, wba_ref, wo_ref, g_ref, b_ref,
                  hf_ref, hb_ref):
    sig = lambda v: 1.0 / (1.0 + jnp.exp(-v.astype(F32)))
    m = sig(gs_ref[...]) * _dot(ys_ref[...], wbs_ref[...]) + sig(ga_ref[...]) * _dot(ya_ref[...], wba_ref[...])
    mix = _dot(m.astype(BF16), wo_ref[...])
    h1 = _ln_rows(DN_ALPHA * h_ref[...] + mix, g_ref[...], b_ref[...])
    hf_ref[...] = h1
    hb_ref[...] = h1.astype(BF16)


def _merge(y_ssd, y_att, gates, h, w_br_ssd, w_br_att, w_out, g, b, tm=256):
    T, D = h.shape
    row = lambda w, j=0: pl.BlockSpec((tm, w), lambda i, j=j: (i, j))
    return pl.pallas_call(
        _merge_kernel,
        out_shape=(jax.ShapeDtypeStruct((T, D), F32), jax.ShapeDtypeStruct((T, D), BF16)),
        grid=(T // tm,),
        in_specs=[row(SSD_INNER), row(ATT_HEADS * ATT_HEADDIM), row(D, 0), row(D, 1), row(D),
                  _const_spec(w_br_ssd.shape), _const_spec(w_br_att.shape), _const_spec(w_out.shape),
                  _const_spec((1, D)), _const_spec((1, D))],
        out_specs=(row(D), row(D)),
        compiler_params=_cparams("parallel"),
    )(y_ssd, y_att, gates, gates, h, w_br_ssd, w_br_att, w_out, g.reshape(1, D), b.reshape(1, D))


FFN_COLS = 1408


def _ffn_up_kernel(h_ref, wup_ref, cw_ref, cb_ref, g_ref, upad, tail):
    c = pl.program_id(1)
    tm = h_ref.shape[0]
    HALO = SUBLANES

    @pl.when(c == 0)
    def _():
        tail[...] = jnp.zeros(tail.shape, F32)

    hb = h_ref[...]
    for cc in range(D_FF // FFN_COLS):
        halves = []
        for half in range(2):
            cols = slice(half * D_FF + cc * FFN_COLS, half * D_FF + (cc + 1) * FFN_COLS)
            upad[0:HALO, :] = tail[:, cols]
            upad[HALO:HALO + tm, :] = _dot(hb, wup_ref[:, cols])
            tail[:, cols] = upad[tm:tm + HALO, :]
            u = cb_ref[:, cols]
            for k in range(FFN_CONV):
                u = u + cw_ref[k:k + 1, cols] * upad[pl.ds(HALO - (FFN_CONV - 1) + k, tm), :]
            halves.append(u)
        g_ref[:, cc * FFN_COLS:(cc + 1) * FFN_COLS] = (_silu(halves[0]) * halves[1]).astype(BF16)


def _ffn_up(h1b, w_up, conv_w, conv_b, B, L, tm=256):
    T, D = h1b.shape
    nt = L // tm
    return pl.pallas_call(
        _ffn_up_kernel,
        out_shape=jax.ShapeDtypeStruct((T, D_FF), BF16),
        grid=(B, nt),
        in_specs=[pl.BlockSpec((tm, D), lambda b, c: (b * nt + c, 0)),
                  _const_spec(w_up.shape), _const_spec(conv_w.shape), _const_spec((1, 2 * D_FF))],
        out_specs=pl.BlockSpec((tm, D_FF), lambda b, c: (b * nt + c, 0)),
        scratch_shapes=[pltpu.VMEM((tm + SUBLANES, FFN_COLS), F32),
                        pltpu.VMEM((SUBLANES, 2 * D_FF), F32)],
        compiler_params=_cparams("parallel", "arbitrary"),
    )(h1b, w_up, conv_w, conv_b.reshape(1, -1))


def _ffn_down_kernel(g_ref, h_ref, wd_ref, lg_ref, lb_ref, o_ref):
    f = _dot(g_ref[...], wd_ref[...])
    o_ref[...] = _ln_rows(DN_ALPHA * h_ref[...] + f, lg_ref[...], lb_ref[...])


def _ffn_down(g, h1, w_down, lg, lb, tm=256):
    T, D = h1.shape
    return pl.pallas_call(
        _ffn_down_kernel,
        out_shape=jax.ShapeDtypeStruct((T, D), F32),
        grid=(T // tm,),
        in_specs=[pl.BlockSpec((tm, D_FF), lambda i: (i, 0)), pl.BlockSpec((tm, D), lambda i: (i, 0)),
                  _const_spec(w_down.shape), _const_spec((1, D)), _const_spec((1, D))],
        out_specs=pl.BlockSpec((tm, D), lambda i: (i, 0)),
        compiler_params=_cparams("parallel"),
    )(g, h1, w_down, lg.reshape(1, D), lb.reshape(1, D))


def _split_w_in(w_in):
    sizes = (SSD_INNER, SSD_XBC, SSD_HEADS, ATT_HEADS * ATT_HEADDIM, ATT_LATENT,
             IDX_HEADS * IDX_HEADDIM, IDX_HEADDIM, IDX_HEADS, D_MODEL, D_MODEL)
    pts = np.cumsum(sizes)[:-1].tolist()
    return jnp.split(w_in, pts, axis=1)


def _pad_cols(w, width):
    return jnp.pad(w, ((0, 0), (0, width - w.shape[1])))


def kernel(x, ln_in_g, ln_in_b, w_in, ssd_conv_w, ssd_conv_b, ssd_dt_bias, ssd_A_log, ssd_D, ssd_norm_g, att_kv_norm_g, att_w_uk, att_w_uv, idx_k_norm_g, idx_k_norm_b, rel_bias, w_br_ssd, w_br_att, w_out, ln1_g, ln1_b, ffn_w_up, ffn_conv_w, ffn_conv_b, ffn_w_down, ln2_g, ln2_b):
    B, L, D = x.shape
    T = B * L
    l = 0
    wz, wxbc, wdt, wq, wckv, wqi, wki, wwi, wgs, wga = _split_w_in(w_in[l])
    w_xz = jnp.concatenate([wxbc, wz], axis=1).astype(BF16)
    w_qc = jnp.concatenate([wq, wckv], axis=1).astype(BF16)
    w_g = jnp.concatenate([wgs, wga], axis=1).astype(BF16)
    w_small = jnp.concatenate([wqi, _pad_cols(wki, LANES), _pad_cols(wwi, LANES), _pad_cols(wdt, LANES)],
                              axis=1).astype(BF16)

    h, xz, qc, gates, small = _ln_inproj(x.reshape(T, D), ln_in_g, ln_in_b, [w_xz, w_qc, w_g, w_small],
                                         [BF16, BF16, BF16, F32])

    y_ssd = _ssd_branch(xz, small, ssd_conv_w[l], ssd_conv_b[l], ssd_dt_bias[l], ssd_A_log[l], ssd_D[l],
                        ssd_norm_g[l], B, L)
    ckv_n, ckv_t, ckv_t2, kidx_n = _dsa_prep(qc, small, att_kv_norm_g[l], idx_k_norm_g[l], idx_k_norm_b[l])
    y_att = _dsa_attention(qc, small, ckv_n, ckv_t, ckv_t2, kidx_n, att_w_uk[l], att_w_uv[l], rel_bias, B, L)

    h1, h1b = _merge(y_ssd, y_att, gates, h, w_br_ssd[l].astype(BF16), w_br_att[l].astype(BF16),
                     w_out[l].astype(BF16), ln1_g[l], ln1_b[l])
    g = _ffn_up(h1b, ffn_w_up[l].astype(BF16), ffn_conv_w[l], ffn_conv_b[l], B, L)
    out = _ffn_down(g, h1, ffn_w_down[l].astype(BF16), ln2_g[l], ln2_b[l])
    return out.reshape(B, L, D).astype(x.dtype)
```

```python
import functools
import math

import numpy as np
import jax
import jax.numpy as jnp
from jax import lax
from jax.experimental import pallas as pl
from jax.experimental.pallas import tpu as pltpu

F32 = jnp.float32
BF16 = jnp.bfloat16

D_MODEL = 1024
CHUNK = 64
SSD_INNER = 2048
SSD_HEADDIM = 64
SSD_HEADS = 32
SSD_GROUPS = 4
SSD_STATE = 128
SSD_CONV = 4
SSD_XBC = SSD_INNER + 2 * SSD_GROUPS * SSD_STATE
ATT_HEADS = 16
ATT_HEADDIM = 64
ATT_LATENT = 256
IDX_HEADS = 8
IDX_HEADDIM = 64
TOPK_MAX = 256
REL_BUCKETS = 32
REL_MAX_DIST = 128
D_FF = 2816
FFN_CONV = 3
DN_ALPHA = 2.0 ** 0.25
LN_EPS = 1e-5

LANES = 128
SUBLANES = 8
VMEM_LIMIT = 56 * 1024 * 1024

SSD_Q = 128
ATT_QB = 256
ATT_TK = 256
INT_MIN = -2 ** 31
HALF16 = 2 ** 15
NEG_BIG = -1e30
LOG2E = math.log2(math.e)


def _cparams(*sem):
    return pltpu.CompilerParams(dimension_semantics=sem, vmem_limit_bytes=VMEM_LIMIT)


def _const_spec(shape):
    nd = len(shape)
    return pl.BlockSpec(shape, lambda *_: (0,) * nd, pipeline_mode=pl.Buffered(1))


def _silu(x):
    return x / (1.0 + jnp.exp(-x))


def _softplus(x):
    return jnp.maximum(x, 0.0) + jnp.log1p(jnp.exp(-jnp.abs(x)))


def _ln_rows(x, g, b):
    mu = jnp.mean(x, -1, keepdims=True)
    xc = x - mu
    var = jnp.mean(xc * xc, -1, keepdims=True)
    return xc * lax.rsqrt(var + LN_EPS) * g + b


def _dot(a, b, **kw):
    return jnp.dot(a, b, preferred_element_type=F32, **kw)


def _rows_back(win, d, halo, n):
    shifted = win if d == 0 else pltpu.roll(win, d, 0)
    return shifted[halo:halo + n, :]


def _dot_exact01(m01, x, x_on_left):
    total = None
    r = x
    for _ in range(3):
        part = r.astype(BF16)
        r = r - part.astype(F32)
        t = _dot(part, m01) if x_on_left else _dot(m01, part)
        total = t if total is None else total + t
    return total


def _dot_nt(a, b):
    return lax.dot_general(a, b, (((1,), (1,)), ((), ())), preferred_element_type=F32)


def _dot_tn(a, b):
    return lax.dot_general(a, b, (((0,), (0,)), ((), ())), preferred_element_type=F32)


def _inproj_kernel(x_ref, g_ref, b_ref, *refs):
    n = (len(refs) - 1) // 2
    w_refs, h_ref, o_refs = refs[:n], refs[n], refs[n + 1:]
    h = _ln_rows(x_ref[...], g_ref[...], b_ref[...])
    h_ref[...] = h
    hb = h.astype(BF16)
    for w_ref, o_ref in zip(w_refs, o_refs):
        o_ref[...] = _dot(hb, w_ref[...]).astype(o_ref.dtype)


def _ln_inproj(x2, g, b, weights, out_dtypes, tm=256):
    T, D = x2.shape
    row = lambda w: pl.BlockSpec((tm, w), lambda i: (i, 0))
    widths = [w.shape[1] for w in weights]
    return pl.pallas_call(
        _inproj_kernel,
        out_shape=(jax.ShapeDtypeStruct((T, D), F32),
                   *[jax.ShapeDtypeStruct((T, n), dt) for n, dt in zip(widths, out_dtypes)]),
        grid=(T // tm,),
        in_specs=[row(D), _const_spec((1, D)), _const_spec((1, D)), *[_const_spec(w.shape) for w in weights]],
        out_specs=(row(D), *[row(n) for n in widths]),
        compiler_params=_cparams("parallel"),
    )(x2, g.reshape(1, D), b.reshape(1, D), *weights)


def _ssd_kernel(xbc_ref, z0_ref, z1_ref, dt_ref, convw_ref, convb_ref, dtb_ref, dtbT_ref,
                a_ref, aT_ref, dskip_ref, ng_ref, e_ref, y_ref, xpad, state, ybuf):
    c = pl.program_id(1)
    Q = xbc_ref.shape[0]
    G, N, P = SSD_GROUPS, SSD_STATE, SSD_HEADDIM
    GW = SSD_INNER // G
    HALO = SUBLANES

    @pl.when(c == 0)
    def _():
        xpad[0:HALO, :] = jnp.zeros((HALO, SSD_XBC), F32)
        state[...] = jnp.zeros(state.shape, F32)

    @pl.when(c > 0)
    def _():
        xpad[0:HALO, :] = xpad[Q:Q + HALO, :]

    xpad[HALO:HALO + Q, :] = xbc_ref[...].astype(F32)
    conv = convb_ref[...]
    xwin = xpad[0:HALO + Q, :]
    for k in range(SSD_CONV):
        conv = conv + convw_ref[k:k + 1, :] * _rows_back(xwin, SSD_CONV - 1 - k, HALO, Q)
    xc = _silu(conv)
    xs = xc[:, :SSD_INNER]

    dt_blk = dt_ref[...]
    dt = _softplus(dt_blk[:, :SSD_HEADS] + dtb_ref[...])
    a = dt * a_ref[...]
    dtT = _softplus(dt_blk.T[:SSD_HEADS, :] + dtbT_ref[...])
    aT = dtT * aT_ref[...]

    row = lax.broadcasted_iota(jnp.int32, (Q, Q), 0)
    col = lax.broadcasted_iota(jnp.int32, (Q, Q), 1)
    causal = row >= col
    tril = jnp.where(causal, 1.0, 0.0).astype(BF16)
    triu = jnp.where(row <= col, 1.0, 0.0).astype(BF16)
    acum = _dot_exact01(tril, a, False)
    acumT = _dot_exact01(triu, aT, True)

    e = e_ref[...]
    dt_e = _dot_exact01(e, dt, True)
    eac_e = _dot_exact01(e, jnp.exp(acum), True)
    ds_e = _dot_exact01(e, jnp.exp(acum[Q - 1:Q, :] - acum), True)
    X = xs * dt_e
    Xb = X.astype(BF16)
    Xd = (X * ds_e).astype(BF16)
    lane = lax.broadcasted_iota(jnp.int32, (Q, LANES), 1)
    lo_half = lane < P

    for g in range(G):
        Bg = xc[:, SSD_INNER + g * N:SSD_INNER + (g + 1) * N].astype(BF16)
        Cg = xc[:, SSD_INNER + G * N + g * N:SSD_INNER + G * N + (g + 1) * N].astype(BF16)
        cb = _dot_nt(Cg, Bg)
        st = state[g]
        eg = eac_e[:, g * GW:(g + 1) * GW]
        yoff = _dot(Cg, st.astype(BF16)) * eg
        new = _dot_tn(Bg, Xd[:, g * GW:(g + 1) * GW])
        state[g] = st * eg[Q - 1:Q, :] + new
        for jj in range(GW // LANES):
            p = g * (GW // LANES) + jj
            h0 = 2 * p
            Xp = Xb[:, p * LANES:(p + 1) * LANES]
            zero = jnp.zeros_like(Xp)
            top = jnp.where(lo_half, Xp, zero)
            bot = jnp.where(lo_half, zero, Xp)
            yd = None
            for hh, rhs in ((h0, top), (h0 + 1, bot)):
                seg = acum[:, hh:hh + 1] - acumT[hh:hh + 1, :]
                decay = jnp.exp(jnp.where(causal, seg, -jnp.inf))
                part = _dot((cb * decay).astype(BF16), rhs)
                yd = part if yd is None else yd + part
            ybuf[:, p * LANES:(p + 1) * LANES] = yd + yoff[:, jj * LANES:(jj + 1) * LANES]

    y = ybuf[...] + xs * dskip_ref[...]
    for g in range(G):
        zg = (z0_ref if g < G // 2 else z1_ref)[:, (g % (G // 2)) * GW:(g % (G // 2) + 1) * GW]
        v = y[:, g * GW:(g + 1) * GW] * _silu(zg.astype(F32))
        v = v * lax.rsqrt(jnp.mean(v * v, -1, keepdims=True) + LN_EPS)
        y_ref[:, g * GW:(g + 1) * GW] = (v * ng_ref[:, g * GW:(g + 1) * GW]).astype(BF16)


def _ssd_branch(xz, small, conv_w, conv_b, dt_bias, A_log, D_skip, norm_g, B, L):
    T = B * L
    Q = SSD_Q
    nc = L // Q
    H = SSD_HEADS
    A = -jnp.exp(A_log.astype(F32))
    expand = jnp.asarray(np.kron(np.eye(H, dtype=np.float32), np.ones((1, SSD_HEADDIM), np.float32)), BF16)
    d_e = jnp.repeat(D_skip.astype(F32), SSD_HEADDIM).reshape(1, SSD_INNER)
    zw = SSD_INNER // 2
    tok = lambda w, j: pl.BlockSpec((Q, w), lambda b, c, j=j: (b * nc + c, j))
    return pl.pallas_call(
        _ssd_kernel,
        out_shape=jax.ShapeDtypeStruct((T, SSD_INNER), BF16),
        grid=(B, nc),
        in_specs=[tok(SSD_XBC, 0), tok(zw, SSD_XBC // zw), tok(zw, SSD_XBC // zw + 1), tok(LANES, 6),
                  _const_spec((SSD_CONV, SSD_XBC)), _const_spec((1, SSD_XBC)),
                  _const_spec((1, H)), _const_spec((H, 1)), _const_spec((1, H)), _const_spec((H, 1)),
                  _const_spec((1, SSD_INNER)), _const_spec((1, SSD_INNER)), _const_spec((H, SSD_INNER))],
        out_specs=pl.BlockSpec((Q, SSD_INNER), lambda b, c: (b * nc + c, 0)),
        scratch_shapes=[pltpu.VMEM((Q + 2 * SUBLANES, SSD_XBC), F32),
                        pltpu.VMEM((SSD_GROUPS, SSD_STATE, SSD_INNER // SSD_GROUPS), F32),
                        pltpu.VMEM((Q, SSD_INNER), F32)],
        compiler_params=_cparams("parallel", "arbitrary"),
    )(xz, xz, xz, small, conv_w, conv_b.reshape(1, -1), dt_bias.reshape(1, H), dt_bias.reshape(H, 1),
      A.reshape(1, H), A.reshape(H, 1), d_e, norm_g.reshape(1, -1), expand)


def _prep_kernel(ckv_ref, kidx_ref, kvg_ref, ig_ref, ib_ref, ckvn_ref, ckvt_ref, ckvt2_ref, kn_ref):
    c = ckv_ref[...].astype(F32)
    cn = c * lax.rsqrt(jnp.mean(c * c, -1, keepdims=True) + LN_EPS) * kvg_ref[...]
    ckvn_ref[...] = cn.astype(BF16)
    cnt = cn.T.astype(BF16)
    ckvt2_ref[0] = cnt
    ckvt_ref[0] = cnt[:, :ATT_TK]
    ckvt_ref[1] = cnt[:, ATT_TK:]
    k = kidx_ref[...]
    lane = lax.broadcasted_iota(jnp.int32, k.shape, 1)
    live = lane < IDX_HEADDIM
    mu = jnp.sum(k, -1, keepdims=True) * (1.0 / IDX_HEADDIM)
    kc = jnp.where(live, k - mu, 0.0)
    var = jnp.sum(kc * kc, -1, keepdims=True) * (1.0 / IDX_HEADDIM)
    kn = jnp.where(live, kc * lax.rsqrt(var + LN_EPS) * ig_ref[...] + ib_ref[...], 0.0)
    kn_ref[...] = kn.astype(BF16)


def _dsa_prep(qc, small, kv_g, idx_g, idx_b):
    T = qc.shape[0]
    tm = 2 * ATT_TK
    pad = lambda v: jnp.pad(v.astype(F32), (0, LANES - IDX_HEADDIM)).reshape(1, LANES)
    return pl.pallas_call(
        _prep_kernel,
        out_shape=(jax.ShapeDtypeStruct((T, ATT_LATENT), BF16),
                   jax.ShapeDtypeStruct((T // ATT_TK, ATT_LATENT, ATT_TK), BF16),
                   jax.ShapeDtypeStruct((T // tm, ATT_LATENT, tm), BF16),
                   jax.ShapeDtypeStruct((T, LANES), BF16)),
        grid=(T // tm,),
        in_specs=[pl.BlockSpec((tm, ATT_LATENT), lambda i: (i, 4)),
                  pl.BlockSpec((tm, LANES), lambda i: (i, 4)),
                  _const_spec((1, ATT_LATENT)), _const_spec((1, LANES)), _const_spec((1, LANES))],
        out_specs=(pl.BlockSpec((tm, ATT_LATENT), lambda i: (i, 0)),
                   pl.BlockSpec((2, ATT_LATENT, ATT_TK), lambda i: (i, 0, 0)),
                   pl.BlockSpec((1, ATT_LATENT, tm), lambda i: (i, 0, 0)),
                   pl.BlockSpec((tm, LANES), lambda i: (i, 0))),
        compiler_params=_cparams("parallel"),
    )(qc, small, kv_g.reshape(1, -1), pad(idx_g), pad(idx_b))


def _attn_kernel(q_ref, qidx_ref, widx_ref, ckv_ref, ckvt_ref, ckvt2_ref, kidx_ref, brow_ref, wuk_ref, wuvt_ref,
                 o_ref, bias_ref, keys, khi, klo, qt, qit, qlat, m_scr, l_scr, acc, ot, *, topk):
    i = pl.program_id(1)
    QB, TK, NH, C, DH, DI = ATT_QB, ATT_TK, ATT_HEADS, ATT_LATENT, ATT_HEADDIM, IDX_HEADDIM
    n_tiles = i + 1

    @pl.when(i == 0)
    def _():
        period = brow_ref.shape[1]
        for h in range(NH):
            base = jnp.broadcast_to(brow_ref[h:h + 1, :], (2 * TK, period))
            bias_ref[h] = pltpu.roll(base, 0, 1, stride=1, stride_axis=0)[:, :QB]

    qt[...] = q_ref[...].astype(F32).T.astype(BF16)
    for h in range(NH):
        ql = _dot(wuk_ref[h], qt[h * DH:(h + 1) * DH, :])
        qlat[h] = (ql * (DH ** -0.5 * LOG2E)).astype(BF16)

    qit[0:IDX_HEADS * DI, :] = qidx_ref[...].T.astype(BF16)
    qit[IDX_HEADS * DI:, :] = jnp.zeros((LANES - DI, QB), BF16)
    wt = widx_ref[...].T[:IDX_HEADS, :] * (IDX_HEADS ** -0.5 * DI ** -0.5)
    qchunk = (i * QB + lax.broadcasted_iota(jnp.int32, (TK, QB), 1)) // CHUNK
    kin = lax.broadcasted_iota(jnp.int32, (TK, QB), 0)

    def score_tile(j, diagonal):
        kt = kidx_ref[pl.ds(pl.multiple_of(j * TK, TK), TK), :]
        s = jnp.zeros((TK, QB), F32)
        for h in range(IDX_HEADS):
            sh = _dot(kt, qit[h * DI:h * DI + LANES, :])
            s = s + wt[h:h + 1, :] * jnp.maximum(sh, 0.0)
        bits = pltpu.bitcast(s, jnp.int32)
        skey = bits ^ ((bits >> 31) & 0x7FFFFFFF)
        if diagonal:
            adm = ((j * TK + kin) // CHUNK) <= qchunk
            skey = jnp.where(adm, skey, INT_MIN)
        keys[j] = skey
        khi[j] = (skey >> 16).astype(jnp.int16)
        klo[j] = ((skey & 0xFFFF) - HALF16).astype(jnp.int16)

    def score_earlier(j, _):
        score_tile(j, False)
        return 0

    lax.fori_loop(0, i, score_earlier, 0)
    score_tile(i, True)

    PK = 2 * SUBLANES

    @pl.when(n_tiles % 2 == 1)
    def _():
        filler = jnp.full((TK, QB), -HALF16, jnp.int16)
        khi[n_tiles] = filler
        klo[n_tiles] = filler

    def count16(ref, test):
        def body(p, cnt):
            for j in (2 * p, 2 * p + 1):
                hit = jnp.where(test(ref[j]), jnp.int16(1), jnp.int16(0))
                for r in range(TK // PK):
                    cnt = cnt + hit[r * PK:(r + 1) * PK, :]
            return cnt
        cnt = lax.fori_loop(0, (n_tiles + 1) // 2, body, jnp.zeros((PK, QB), jnp.int16))
        return jnp.sum(cnt.astype(jnp.int32), axis=0, keepdims=True)

    def search16(ref, base_count):
        def bit_step(b, t):
            cand = t + jnp.left_shift(jnp.int32(1), 15 - b)
            c16 = (cand - HALF16).astype(jnp.int16)
            ok = base_count + count16(ref, lambda v: v >= c16) >= topk
            return jnp.where(ok, cand, t)
        return lax.fori_loop(0, 16, bit_step, jnp.zeros((1, QB), jnp.int32))

    zero = jnp.zeros((1, QB), jnp.int32)
    thi = search16(khi, zero)
    thi16 = (thi - HALF16).astype(jnp.int16)
    above = count16(khi, lambda v: v > thi16)

    def mark_equal(j, _):
        klo[j] = jnp.where(khi[j] == thi16, klo[j], jnp.int16(-HALF16))
        return 0

    lax.fori_loop(0, n_tiles, mark_equal, 0)
    tlo = search16(klo, above)
    tau = jnp.left_shift(thi - HALF16, 16) | tlo
    tau = jnp.maximum(tau, INT_MIN + 1)

    m_scr[...] = jnp.full(m_scr.shape, NEG_BIG, F32)
    l_scr[...] = jnp.zeros(l_scr.shape, F32)
    acc[...] = jnp.zeros(acc.shape, F32)

    def attend(kv, kvts, key_tiles, bias_rows):
        maskb = jnp.where(jnp.concatenate(key_tiles, axis=0) >= tau, 0.0, NEG_BIG)
        for h in range(NH):
            s = _dot(kv, qlat[h]) + maskb
            if bias_rows is not None:
                s = s + bias_ref[h, bias_rows, :]
            m_old = m_scr[h]
            m_new = jnp.maximum(m_old, jnp.max(s, axis=0, keepdims=True))
            alpha = jnp.exp2(m_old - m_new)
            pr = jnp.exp2(s - m_new)
            l_scr[h] = alpha * l_scr[h] + jnp.sum(pr, axis=0, keepdims=True)
            pb = pr.astype(BF16)
            pv, row = None, 0
            for kvt in kvts:
                part = _dot(kvt, pb[row:row + kvt.shape[1], :])
                pv = part if pv is None else pv + part
                row += kvt.shape[1]
            acc[h] = alpha * acc[h] + pv
            m_scr[h] = m_new

    def kv_rows(first_tile, n):
        return ckv_ref[pl.ds(pl.multiple_of(first_tile * TK, TK), n * TK), :]

    def far_pair(jp, _):
        attend(kv_rows(2 * jp, 2), [ckvt2_ref[jp]], [keys[2 * jp], keys[2 * jp + 1]], None)
        return 0

    n_far = jnp.maximum(i - 1, 0)
    lax.fori_loop(0, n_far // 2, far_pair, 0)

    @pl.when(n_far % 2 == 1)
    def _():
        attend(kv_rows(n_far - 1, 1), [ckvt_ref[n_far - 1]], [keys[n_far - 1]], None)

    @pl.when(i > 0)
    def _():
        attend(kv_rows(i - 1, 2), [ckvt_ref[i - 1], ckvt_ref[i]], [keys[i - 1], keys[i]], slice(0, 2 * TK))

    @pl.when(i == 0)
    def _():
        attend(kv_rows(0, 1), [ckvt_ref[0]], [keys[0]], slice(TK, 2 * TK))

    for h in range(NH):
        ol = (acc[h] / l_scr[h]).astype(BF16)
        ot[h * DH:(h + 1) * DH, :] = _dot(wuvt_ref[h], ol)
    o_ref[...] = ot[...].T.astype(o_ref.dtype)


def _t5_bucket_np(rel):
    nb = REL_BUCKETS // 2
    max_exact = nb // 2
    n = np.abs(rel).astype(np.int64)
    nn = np.maximum(n, 1)
    sq = nn * nn
    log2_sq = np.floor(np.log2(sq.astype(np.float64))).astype(np.int64)
    log2_sq = np.where(2 ** (log2_sq + 1) <= sq, log2_sq + 1, log2_sq)
    log2_sq = np.where(2 ** log2_sq > sq, log2_sq - 1, log2_sq)
    large = np.minimum(max_exact + (log2_sq - 6), nb - 1)
    return np.where(rel > 0, nb, 0) + np.where(n < max_exact, n, large)


def _dsa_attention(qc, small, ckv_n, ckv_t, ckv_t2, kidx_n, w_uk, w_uv, rel_bias, B, L):
    T = B * L
    QB, TK, NH, C, DH = ATT_QB, ATT_TK, ATT_HEADS, ATT_LATENT, ATT_HEADDIM
    nq = L // QB
    topk = min(TOPK_MAX, L // 4)
    n_rel = 2 * TK + QB - 1
    period = n_rel + 1
    rel_of = np.arange(n_rel) - (TK + QB - 1)
    far_bucket = int(_t5_bucket_np(np.array([-(TK + 1)]))[0])
    table = (rel_bias.astype(F32) - rel_bias[far_bucket].astype(F32)[None, :]) * LOG2E
    v = jnp.take(table, jnp.asarray(_t5_bucket_np(rel_of)), axis=0).T
    u = jnp.pad(v, ((0, 0), (0, 1)))[:, (QB - 1 - np.arange(period)) % period]
    wuk = w_uk.astype(BF16)
    wuvt = jnp.transpose(w_uv, (0, 2, 1)).astype(BF16)

    return pl.pallas_call(
        functools.partial(_attn_kernel, topk=topk),
        out_shape=jax.ShapeDtypeStruct((T, NH * DH), BF16),
        grid=(B, nq),
        in_specs=[
            pl.BlockSpec((QB, NH * DH), lambda b, i: (b * nq + i, 0)),
            pl.BlockSpec((QB, IDX_HEADS * IDX_HEADDIM), lambda b, i: (b * nq + i, 0)),
            pl.BlockSpec((QB, LANES), lambda b, i: (b * nq + i, 5)),
            pl.BlockSpec((L, C), lambda b, i: (b, 0)),
            pl.BlockSpec((L // TK, C, TK), lambda b, i: (b, 0, 0)),
            pl.BlockSpec((L // (2 * TK), C, 2 * TK), lambda b, i: (b, 0, 0)),
            pl.BlockSpec((L, LANES), lambda b, i: (b, 0)),
            _const_spec((NH, period)),
            _const_spec((NH, C, DH)),
            _const_spec((NH, DH, C)),
        ],
        out_specs=pl.BlockSpec((QB, NH * DH), lambda b, i: (b * nq + i, 0)),
        scratch_shapes=[pltpu.VMEM((NH, 2 * TK, QB), F32),
                        pltpu.VMEM((L // TK, TK, QB), jnp.int32),
                        pltpu.VMEM((L // TK, TK, QB), jnp.int16),
                        pltpu.VMEM((L // TK, TK, QB), jnp.int16),
                        pltpu.VMEM((NH * DH, QB), BF16),
                        pltpu.VMEM((IDX_HEADS * IDX_HEADDIM + LANES - IDX_HEADDIM, QB), BF16),
                        pltpu.VMEM((NH, C, QB), BF16),
                        pltpu.VMEM((NH, 1, QB), F32),
                        pltpu.VMEM((NH, 1, QB), F32),
                        pltpu.VMEM((NH, C, QB), F32),
                        pltpu.VMEM((NH * DH, QB), F32)],
        compiler_params=_cparams("parallel", "arbitrary"),
    )(qc, small, small, ckv_n, ckv_t, ckv_t2, kidx_n, u, wuk, wuvt)


def _merge_kernel(ys_ref, ya_ref, gs_ref, ga_ref, h_ref, wbs_ref, wba_ref, wo_ref, g_ref, b_ref,
                  hf_ref, hb_ref):
    sig = lambda v: 1.0 / (1.0 + jnp.exp(-v.astype(F32)))
    m = sig(gs_ref[...]) * _dot(ys_ref[...], wbs_ref[...]) + sig(ga_ref[...]) * _dot(ya_ref[...], wba_ref[...])
    mix = _dot(m.astype(BF16), wo_ref[...])
    h1 = _ln_rows(DN_ALPHA * h_ref[...] + mix, g_ref[...], b_ref[...])
    hf_ref[...] = h1
    hb_ref[...] = h1.astype(BF16)


def _merge(y_ssd, y_att, gates, h, w_br_ssd, w_br_att, w_out, g, b, tm=256):
    T, D = h.shape
    row = lambda w, j=0: pl.BlockSpec((tm, w), lambda i, j=j: (i, j))
    return pl.pallas_call(
        _merge_kernel,
        out_shape=(jax.ShapeDtypeStruct((T, D), F32), jax.ShapeDtypeStruct((T, D), BF16)),
        grid=(T // tm,),
        in_specs=[row(SSD_INNER), row(ATT_HEADS * ATT_HEADDIM), row(D, 0), row(D, 1), row(D),
                  _const_spec(w_br_ssd.shape), _const_spec(w_br_att.shape), _const_spec(w_out.shape),
                  _const_spec((1, D)), _const_spec((1, D))],
        out_specs=(row(D), row(D)),
        compiler_params=_cparams("parallel"),
    )(y_ssd, y_att, gates, gates, h, w_br_ssd, w_br_att, w_out, g.reshape(1, D), b.reshape(1, D))


FFN_COLS = 1408


def _ffn_kernel(hb_ref, h_ref, wup_ref, cw_ref, cb_ref, wd_ref, lg_ref, lb_ref, o_ref, upad, tail):
    c = pl.program_id(1)
    tm = hb_ref.shape[0]
    HALO = SUBLANES

    @pl.when(c == 0)
    def _():
        tail[...] = jnp.zeros(tail.shape, F32)

    hb = hb_ref[...]
    f = None
    for cc in range(D_FF // FFN_COLS):
        halves = []
        for half in range(2):
            cols = slice(half * D_FF + cc * FFN_COLS, half * D_FF + (cc + 1) * FFN_COLS)
            upad[0:HALO, :] = tail[:, cols]
            upad[HALO:HALO + tm, :] = _dot(hb, wup_ref[:, cols])
            tail[:, cols] = upad[tm:tm + HALO, :]
            u = cb_ref[:, cols]
            for k in range(FFN_CONV):
                u = u + cw_ref[k:k + 1, cols] * upad[pl.ds(HALO - (FFN_CONV - 1) + k, tm), :]
            halves.append(u)
        g = (_silu(halves[0]) * halves[1]).astype(BF16)
        part = _dot(g, wd_ref[cc * FFN_COLS:(cc + 1) * FFN_COLS, :])
        f = part if f is None else f + part
    o_ref[...] = _ln_rows(DN_ALPHA * h_ref[...] + f, lg_ref[...], lb_ref[...])


def _conv_ffn(h1b, h1, w_up, conv_w, conv_b, w_down, lg, lb, B, L, tm=256):
    T, D = h1.shape
    nt = L // tm
    tok = pl.BlockSpec((tm, D), lambda b, c: (b * nt + c, 0))
    return pl.pallas_call(
        _ffn_kernel,
        out_shape=jax.ShapeDtypeStruct((T, D), F32),
        grid=(B, nt),
        in_specs=[tok, tok, _const_spec(w_up.shape), _const_spec(conv_w.shape), _const_spec((1, 2 * D_FF)),
                  _const_spec(w_down.shape), _const_spec((1, D)), _const_spec((1, D))],
        out_specs=tok,
        scratch_shapes=[pltpu.VMEM((tm + SUBLANES, FFN_COLS), F32),
                        pltpu.VMEM((SUBLANES, 2 * D_FF), F32)],
        compiler_params=_cparams("parallel", "arbitrary"),
    )(h1b, h1, w_up, conv_w, conv_b.reshape(1, -1), w_down, lg.reshape(1, D), lb.reshape(1, D))


def _split_w_in(w_in):
    sizes = (SSD_INNER, SSD_XBC, SSD_HEADS, ATT_HEADS * ATT_HEADDIM, ATT_LATENT,
             IDX_HEADS * IDX_HEADDIM, IDX_HEADDIM, IDX_HEADS, D_MODEL, D_MODEL)
    pts = np.cumsum(sizes)[:-1].tolist()
    return jnp.split(w_in, pts, axis=1)


def _pad_cols(w, width):
    return jnp.pad(w, ((0, 0), (0, width - w.shape[1])))


def kernel(x, ln_in_g, ln_in_b, w_in, ssd_conv_w, ssd_conv_b, ssd_dt_bias, ssd_A_log, ssd_D, ssd_norm_g, att_kv_norm_g, att_w_uk, att_w_uv, idx_k_norm_g, idx_k_norm_b, rel_bias, w_br_ssd, w_br_att, w_out, ln1_g, ln1_b, ffn_w_up, ffn_conv_w, ffn_conv_b, ffn_w_down, ln2_g, ln2_b):
    B, L, D = x.shape
    T = B * L
    l = 0
    wz, wxbc, wdt, wq, wckv, wqi, wki, wwi, wgs, wga = _split_w_in(w_in[l])
    w_xz = jnp.concatenate([wxbc, wz], axis=1).astype(BF16)
    w_qc = jnp.concatenate([wq, wckv], axis=1).astype(BF16)
    w_g = jnp.concatenate([wgs, wga], axis=1).astype(BF16)
    w_small = jnp.concatenate([wqi, _pad_cols(wki, LANES), _pad_cols(wwi, LANES), _pad_cols(wdt, LANES)],
                              axis=1).astype(BF16)

    h, xz, qc, gates, small = _ln_inproj(x.reshape(T, D), ln_in_g, ln_in_b, [w_xz, w_qc, w_g, w_small],
                                         [BF16, BF16, BF16, F32])

    y_ssd = _ssd_branch(xz, small, ssd_conv_w[l], ssd_conv_b[l], ssd_dt_bias[l], ssd_A_log[l], ssd_D[l],
                        ssd_norm_g[l], B, L)
    ckv_n, ckv_t, ckv_t2, kidx_n = _dsa_prep(qc, small, att_kv_norm_g[l], idx_k_norm_g[l], idx_k_norm_b[l])
    y_att = _dsa_attention(qc, small, ckv_n, ckv_t, ckv_t2, kidx_n, att_w_uk[l], att_w_uv[l], rel_bias, B, L)

    h1, h1b = _merge(y_ssd, y_att, gates, h, w_br_ssd[l].astype(BF16), w_br_att[l].astype(BF16),
                     w_out[l].astype(BF16), ln1_g[l], ln1_b[l])
    out = _conv_ffn(h1b, h1, ffn_w_up[l].astype(BF16), ffn_conv_w[l], ffn_conv_b[l],
                    ffn_w_down[l].astype(BF16), ln2_g[l], ln2_b[l], B, L)
    return out.reshape(B, L, D).astype(x.dtype)
```

```python
import functools
import math

import numpy as np
import jax
import jax.numpy as jnp
from jax import lax
from jax.experimental import pallas as pl
from jax.experimental.pallas import tpu as pltpu

F32 = jnp.float32
BF16 = jnp.bfloat16

D_MODEL = 1024
CHUNK = 64
SSD_INNER = 2048
SSD_HEADDIM = 64
SSD_HEADS = 32
SSD_GROUPS = 4
SSD_STATE = 128
SSD_CONV = 4
SSD_XBC = SSD_INNER + 2 * SSD_GROUPS * SSD_STATE
ATT_HEADS = 16
ATT_HEADDIM = 64
ATT_LATENT = 256
IDX_HEADS = 8
IDX_HEADDIM = 64
TOPK_MAX = 256
REL_BUCKETS = 32
REL_MAX_DIST = 128
D_FF = 2816
FFN_CONV = 3
DN_ALPHA = 2.0 ** 0.25
LN_EPS = 1e-5

LANES = 128
SUBLANES = 8
VMEM_LIMIT = 56 * 1024 * 1024

SSD_Q = 128
ATT_QB = 256
ATT_TK = 256
INT_MIN = -2 ** 31
HALF16 = 2 ** 15
NEG_BIG = -1e30
LOG2E = math.log2(math.e)


def _cparams(*sem):
    return pltpu.CompilerParams(dimension_semantics=sem, vmem_limit_bytes=VMEM_LIMIT)


def _const_spec(shape):
    nd = len(shape)
    return pl.BlockSpec(shape, lambda *_: (0,) * nd, pipeline_mode=pl.Buffered(1))


def _silu(x):
    return x / (1.0 + jnp.exp(-x))


def _softplus(x):
    return jnp.maximum(x, 0.0) + jnp.log1p(jnp.exp(-jnp.abs(x)))


def _ln_rows(x, g, b):
    mu = jnp.mean(x, -1, keepdims=True)
    xc = x - mu
    var = jnp.mean(xc * xc, -1, keepdims=True)
    return xc * lax.rsqrt(var + LN_EPS) * g + b


def _dot(a, b, **kw):
    return jnp.dot(a, b, preferred_element_type=F32, **kw)


def _rows_back(win, d, halo, n):
    shifted = win if d == 0 else pltpu.roll(win, d, 0)
    return shifted[halo:halo + n, :]


def _dot_exact01(m01, x, x_on_left):
    total = None
    r = x
    for _ in range(3):
        part = r.astype(BF16)
        r = r - part.astype(F32)
        t = _dot(part, m01) if x_on_left else _dot(m01, part)
        total = t if total is None else total + t
    return total


def _dot_nt(a, b):
    return lax.dot_general(a, b, (((1,), (1,)), ((), ())), preferred_element_type=F32)


def _dot_tn(a, b):
    return lax.dot_general(a, b, (((0,), (0,)), ((), ())), preferred_element_type=F32)


def _inproj_kernel(x_ref, g_ref, b_ref, *refs):
    n = (len(refs) - 1) // 2
    w_refs, h_ref, o_refs = refs[:n], refs[n], refs[n + 1:]
    h = _ln_rows(x_ref[...], g_ref[...], b_ref[...])
    h_ref[...] = h
    hb = h.astype(BF16)
    for w_ref, o_ref in zip(w_refs, o_refs):
        o_ref[...] = _dot(hb, w_ref[...]).astype(o_ref.dtype)


def _ln_inproj(x2, g, b, weights, out_dtypes, tm=256):
    T, D = x2.shape
    row = lambda w: pl.BlockSpec((tm, w), lambda i: (i, 0))
    widths = [w.shape[1] for w in weights]
    return pl.pallas_call(
        _inproj_kernel,
        out_shape=(jax.ShapeDtypeStruct((T, D), F32),
                   *[jax.ShapeDtypeStruct((T, n), dt) for n, dt in zip(widths, out_dtypes)]),
        grid=(T // tm,),
        in_specs=[row(D), _const_spec((1, D)), _const_spec((1, D)), *[_const_spec(w.shape) for w in weights]],
        out_specs=(row(D), *[row(n) for n in widths]),
        compiler_params=_cparams("parallel"),
    )(x2, g.reshape(1, D), b.reshape(1, D), *weights)


def _ssd_kernel(xbc_ref, z_ref, dt_ref, convw_ref, convb_ref, dtb_ref, dtbT_ref,
                a_ref, aT_ref, dskip_ref, ng_ref, e_ref, y_ref, xpad, state, ybuf):
    c = pl.program_id(1)
    Q = xbc_ref.shape[0]
    G, N, P = SSD_GROUPS, SSD_STATE, SSD_HEADDIM
    GW = SSD_INNER // G
    HALO = SUBLANES

    @pl.when(c == 0)
    def _():
        xpad[0:HALO, :] = jnp.zeros((HALO, SSD_XBC), F32)
        state[...] = jnp.zeros(state.shape, F32)

    @pl.when(c > 0)
    def _():
        xpad[0:HALO, :] = xpad[Q:Q + HALO, :]

    xpad[HALO:HALO + Q, :] = xbc_ref[...].astype(F32)
    conv = convb_ref[...]
    xwin = xpad[0:HALO + Q, :]
    for k in range(SSD_CONV):
        conv = conv + convw_ref[k:k + 1, :] * _rows_back(xwin, SSD_CONV - 1 - k, HALO, Q)
    xc = _silu(conv)
    xs = xc[:, :SSD_INNER]

    dt_blk = dt_ref[...]
    dt = _softplus(dt_blk[:, :SSD_HEADS] + dtb_ref[...])
    a = dt * a_ref[...]
    dtT = _softplus(dt_blk.T[:SSD_HEADS, :] + dtbT_ref[...])
    aT = dtT * aT_ref[...]

    row = lax.broadcasted_iota(jnp.int32, (Q, Q), 0)
    col = lax.broadcasted_iota(jnp.int32, (Q, Q), 1)
    causal = row >= col
    tril = jnp.where(causal, 1.0, 0.0).astype(BF16)
    triu = jnp.where(row <= col, 1.0, 0.0).astype(BF16)
    acum = _dot_exact01(tril, a, False)
    acumT = _dot_exact01(triu, aT, True)

    e = e_ref[...]
    dt_e = _dot_exact01(e, dt, True)
    eac_e = _dot_exact01(e, jnp.exp(acum), True)
    ds_e = _dot_exact01(e, jnp.exp(acum[Q - 1:Q, :] - acum), True)
    X = xs * dt_e
    Xb = X.astype(BF16)
    Xd = (X * ds_e).astype(BF16)
    lane = lax.broadcasted_iota(jnp.int32, (Q, LANES), 1)
    lo_half = lane < P

    for g in range(G):
        Bg = xc[:, SSD_INNER + g * N:SSD_INNER + (g + 1) * N].astype(BF16)
        Cg = xc[:, SSD_INNER + G * N + g * N:SSD_INNER + G * N + (g + 1) * N].astype(BF16)
        cb = _dot_nt(Cg, Bg)
        st = state[g]
        eg = eac_e[:, g * GW:(g + 1) * GW]
        yoff = _dot(Cg, st.astype(BF16)) * eg
        new = _dot_tn(Bg, Xd[:, g * GW:(g + 1) * GW])
        state[g] = st * eg[Q - 1:Q, :] + new
        for jj in range(GW // LANES):
            p = g * (GW // LANES) + jj
            h0 = 2 * p
            Xp = Xb[:, p * LANES:(p + 1) * LANES]
            zero = jnp.zeros_like(Xp)
            top = jnp.where(lo_half, Xp, zero)
            bot = jnp.where(lo_half, zero, Xp)
            yd = None
            for hh, rhs in ((h0, top), (h0 + 1, bot)):
                seg = acum[:, hh:hh + 1] - acumT[hh:hh + 1, :]
                decay = jnp.exp(jnp.where(causal, seg, -jnp.inf))
                part = _dot((cb * decay).astype(BF16), rhs)
                yd = part if yd is None else yd + part
            ybuf[:, p * LANES:(p + 1) * LANES] = yd + yoff[:, jj * LANES:(jj + 1) * LANES]

    y = ybuf[...] + xs * dskip_ref[...]
    for g in range(G):
        zg = z_ref[:, g * GW:(g + 1) * GW]
        v = y[:, g * GW:(g + 1) * GW] * _silu(zg.astype(F32))
        v = v * lax.rsqrt(jnp.mean(v * v, -1, keepdims=True) + LN_EPS)
        y_ref[:, g * GW:(g + 1) * GW] = (v * ng_ref[:, g * GW:(g + 1) * GW]).astype(BF16)


def _ssd_branch(xbc, z, small, conv_w, conv_b, dt_bias, A_log, D_skip, norm_g, B, L):
    T = B * L
    Q = SSD_Q
    nc = L // Q
    H = SSD_HEADS
    A = -jnp.exp(A_log.astype(F32))
    expand = jnp.asarray(np.kron(np.eye(H, dtype=np.float32), np.ones((1, SSD_HEADDIM), np.float32)), BF16)
    d_e = jnp.repeat(D_skip.astype(F32), SSD_HEADDIM).reshape(1, SSD_INNER)
    tok = lambda w, j: pl.BlockSpec((Q, w), lambda b, c, j=j: (b * nc + c, j))
    return pl.pallas_call(
        _ssd_kernel,
        out_shape=jax.ShapeDtypeStruct((T, SSD_INNER), BF16),
        grid=(B, nc),
        in_specs=[tok(SSD_XBC, 0), tok(SSD_INNER, 0), tok(LANES, 6),
                  _const_spec((SSD_CONV, SSD_XBC)), _const_spec((1, SSD_XBC)),
                  _const_spec((1, H)), _const_spec((H, 1)), _const_spec((1, H)), _const_spec((H, 1)),
                  _const_spec((1, SSD_INNER)), _const_spec((1, SSD_INNER)), _const_spec((H, SSD_INNER))],
        out_specs=pl.BlockSpec((Q, SSD_INNER), lambda b, c: (b * nc + c, 0)),
        scratch_shapes=[pltpu.VMEM((Q + 2 * SUBLANES, SSD_XBC), F32),
                        pltpu.VMEM((SSD_GROUPS, SSD_STATE, SSD_INNER // SSD_GROUPS), F32),
                        pltpu.VMEM((Q, SSD_INNER), F32)],
        compiler_params=_cparams("parallel", "arbitrary"),
    )(xbc, z, small, conv_w, conv_b.reshape(1, -1), dt_bias.reshape(1, H), dt_bias.reshape(H, 1),
      A.reshape(1, H), A.reshape(H, 1), d_e, norm_g.reshape(1, -1), expand)


def _prep_kernel(ckv_ref, kidx_ref, kvg_ref, ig_ref, ib_ref, ckvn_ref, ckvt_ref, ckvt2_ref, kn_ref):
    c = ckv_ref[...].astype(F32)
    cn = c * lax.rsqrt(jnp.mean(c * c, -1, keepdims=True) + LN_EPS) * kvg_ref[...]
    ckvn_ref[...] = cn.astype(BF16)
    cnt = cn.T.astype(BF16)
    ckvt2_ref[0] = cnt
    ckvt_ref[0] = cnt[:, :ATT_TK]
    ckvt_ref[1] = cnt[:, ATT_TK:]
    k = kidx_ref[...]
    lane = lax.broadcasted_iota(jnp.int32, k.shape, 1)
    live = lane < IDX_HEADDIM
    mu = jnp.sum(k, -1, keepdims=True) * (1.0 / IDX_HEADDIM)
    kc = jnp.where(live, k - mu, 0.0)
    var = jnp.sum(kc * kc, -1, keepdims=True) * (1.0 / IDX_HEADDIM)
    kn = jnp.where(live, kc * lax.rsqrt(var + LN_EPS) * ig_ref[...] + ib_ref[...], 0.0)
    kn_ref[...] = kn.astype(BF16)


def _dsa_prep(qc, small, kv_g, idx_g, idx_b):
    T = qc.shape[0]
    tm = 2 * ATT_TK
    pad = lambda v: jnp.pad(v.astype(F32), (0, LANES - IDX_HEADDIM)).reshape(1, LANES)
    return pl.pallas_call(
        _prep_kernel,
        out_shape=(jax.ShapeDtypeStruct((T, ATT_LATENT), BF16),
                   jax.ShapeDtypeStruct((T // ATT_TK, ATT_LATENT, ATT_TK), BF16),
                   jax.ShapeDtypeStruct((T // tm, ATT_LATENT, tm), BF16),
                   jax.ShapeDtypeStruct((T, LANES), BF16)),
        grid=(T // tm,),
        in_specs=[pl.BlockSpec((tm, ATT_LATENT), lambda i: (i, 0)),
                  pl.BlockSpec((tm, LANES), lambda i: (i, 4)),
                  _const_spec((1, ATT_LATENT)), _const_spec((1, LANES)), _const_spec((1, LANES))],
        out_specs=(pl.BlockSpec((tm, ATT_LATENT), lambda i: (i, 0)),
                   pl.BlockSpec((2, ATT_LATENT, ATT_TK), lambda i: (i, 0, 0)),
                   pl.BlockSpec((1, ATT_LATENT, tm), lambda i: (i, 0, 0)),
                   pl.BlockSpec((tm, LANES), lambda i: (i, 0))),
        compiler_params=_cparams("parallel"),
    )(qc, small, kv_g.reshape(1, -1), pad(idx_g), pad(idx_b))


def _attn_kernel(q_ref, qidx_ref, widx_ref, ckv_ref, ckvt_ref, ckvt2_ref, kidx_ref, brow_ref, wuk_ref, wuvt_ref,
                 o_ref, bias_ref, keys, khi, klo, qt, qit, qlat, m_scr, l_scr, acc, ot, *, topk):
    i = pl.program_id(1)
    QB, TK, NH, C, DH, DI = ATT_QB, ATT_TK, ATT_HEADS, ATT_LATENT, ATT_HEADDIM, IDX_HEADDIM
    n_tiles = i + 1

    @pl.when(i == 0)
    def _():
        period = brow_ref.shape[1]
        for h in range(NH):
            base = jnp.broadcast_to(brow_ref[h:h + 1, :], (2 * TK, period))
            bias_ref[h] = pltpu.roll(base, 0, 1, stride=1, stride_axis=0)[:, :QB]

    qt[...] = q_ref[...].astype(F32).T.astype(BF16)
    for h in range(NH):
        ql = _dot(wuk_ref[h], qt[h * DH:(h + 1) * DH, :])
        qlat[h] = (ql * (DH ** -0.5 * LOG2E)).astype(BF16)

    qit[0:IDX_HEADS * DI, :] = qidx_ref[...].T.astype(BF16)
    qit[IDX_HEADS * DI:, :] = jnp.zeros((LANES - DI, QB), BF16)
    wt = widx_ref[...].T[:IDX_HEADS, :] * (IDX_HEADS ** -0.5 * DI ** -0.5)
    qchunk = (i * QB + lax.broadcasted_iota(jnp.int32, (TK, QB), 1)) // CHUNK
    kin = lax.broadcasted_iota(jnp.int32, (TK, QB), 0)

    def score_tile(j, diagonal):
        kt = kidx_ref[pl.ds(pl.multiple_of(j * TK, TK), TK), :]
        s = jnp.zeros((TK, QB), F32)
        for h in range(IDX_HEADS):
            sh = _dot(kt, qit[h * DI:h * DI + LANES, :])
            s = s + wt[h:h + 1, :] * jnp.maximum(sh, 0.0)
        bits = pltpu.bitcast(s, jnp.int32)
        skey = bits ^ ((bits >> 31) & 0x7FFFFFFF)
        if diagonal:
            adm = ((j * TK + kin) // CHUNK) <= qchunk
            skey = jnp.where(adm, skey, INT_MIN)
        keys[j] = skey
        khi[j] = (skey >> 16).astype(jnp.int16)
        klo[j] = ((skey & 0xFFFF) - HALF16).astype(jnp.int16)

    def score_earlier(j, _):
        score_tile(j, False)
        return 0

    lax.fori_loop(0, i, score_earlier, 0)
    score_tile(i, True)

    PK = 2 * SUBLANES

    @pl.when(n_tiles % 2 == 1)
    def _():
        filler = jnp.full((TK, QB), -HALF16, jnp.int16)
        khi[n_tiles] = filler
        klo[n_tiles] = filler

    def count16(ref, test):
        def body(p, cnt):
            for j in (2 * p, 2 * p + 1):
                hit = jnp.where(test(ref[j]), jnp.int16(1), jnp.int16(0))
                for r in range(TK // PK):
                    cnt = cnt + hit[r * PK:(r + 1) * PK, :]
            return cnt
        cnt = lax.fori_loop(0, (n_tiles + 1) // 2, body, jnp.zeros((PK, QB), jnp.int16))
        return jnp.sum(cnt.astype(jnp.int32), axis=0, keepdims=True)

    def search16(ref, base_count):
        def bit_step(b, t):
            cand = t + jnp.left_shift(jnp.int32(1), 15 - b)
            c16 = (cand - HALF16).astype(jnp.int16)
            ok = base_count + count16(ref, lambda v: v >= c16) >= topk
            return jnp.where(ok, cand, t)
        return lax.fori_loop(0, 16, bit_step, jnp.zeros((1, QB), jnp.int32))

    zero = jnp.zeros((1, QB), jnp.int32)
    thi = search16(khi, zero)
    thi16 = (thi - HALF16).astype(jnp.int16)
    above = count16(khi, lambda v: v > thi16)

    def mark_equal(j, _):
        klo[j] = jnp.where(khi[j] == thi16, klo[j], jnp.int16(-HALF16))
        return 0

    lax.fori_loop(0, n_tiles, mark_equal, 0)
    tlo = search16(klo, above)
    tau = jnp.left_shift(thi - HALF16, 16) | tlo
    tau = jnp.maximum(tau, INT_MIN + 1)

    m_scr[...] = jnp.full(m_scr.shape, NEG_BIG, F32)
    l_scr[...] = jnp.zeros(l_scr.shape, F32)
    acc[...] = jnp.zeros(acc.shape, F32)

    def attend(kv, kvts, key_tiles, bias_rows):
        maskb = jnp.where(jnp.concatenate(key_tiles, axis=0) >= tau, 0.0, NEG_BIG)
        for h in range(NH):
            s = _dot(kv, qlat[h]) + maskb
            if bias_rows is not None:
                s = s + bias_ref[h, bias_rows, :]
            m_old = m_scr[h]
            m_new = jnp.maximum(m_old, jnp.max(s, axis=0, keepdims=True))
            alpha = jnp.exp2(m_old - m_new)
            pr = jnp.exp2(s - m_new)
            l_scr[h] = alpha * l_scr[h] + jnp.sum(pr, axis=0, keepdims=True)
            pb = pr.astype(BF16)
            pv, row = None, 0
            for kvt in kvts:
                part = _dot(kvt, pb[row:row + kvt.shape[1], :])
                pv = part if pv is None else pv + part
                row += kvt.shape[1]
            acc[h] = alpha * acc[h] + pv
            m_scr[h] = m_new

    def kv_rows(first_tile, n):
        return ckv_ref[pl.ds(pl.multiple_of(first_tile * TK, TK), n * TK), :]

    def far_pair(jp, _):
        attend(kv_rows(2 * jp, 2), [ckvt2_ref[jp]], [keys[2 * jp], keys[2 * jp + 1]], None)
        return 0

    n_far = jnp.maximum(i - 1, 0)
    lax.fori_loop(0, n_far // 2, far_pair, 0)

    @pl.when(n_far % 2 == 1)
    def _():
        attend(kv_rows(n_far - 1, 1), [ckvt_ref[n_far - 1]], [keys[n_far - 1]], None)

    @pl.when(i > 0)
    def _():
        attend(kv_rows(i - 1, 2), [ckvt_ref[i - 1], ckvt_ref[i]], [keys[i - 1], keys[i]], slice(0, 2 * TK))

    @pl.when(i == 0)
    def _():
        attend(kv_rows(0, 1), [ckvt_ref[0]], [keys[0]], slice(TK, 2 * TK))

    for h in range(NH):
        ol = (acc[h] / l_scr[h]).astype(BF16)
        ot[h * DH:(h + 1) * DH, :] = _dot(wuvt_ref[h], ol)
    o_ref[...] = ot[...].T.astype(o_ref.dtype)


def _t5_bucket_np(rel):
    nb = REL_BUCKETS // 2
    max_exact = nb // 2
    n = np.abs(rel).astype(np.int64)
    nn = np.maximum(n, 1)
    sq = nn * nn
    log2_sq = np.floor(np.log2(sq.astype(np.float64))).astype(np.int64)
    log2_sq = np.where(2 ** (log2_sq + 1) <= sq, log2_sq + 1, log2_sq)
    log2_sq = np.where(2 ** log2_sq > sq, log2_sq - 1, log2_sq)
    large = np.minimum(max_exact + (log2_sq - 6), nb - 1)
    return np.where(rel > 0, nb, 0) + np.where(n < max_exact, n, large)


def _dsa_attention(qc, small, ckv_n, ckv_t, ckv_t2, kidx_n, w_uk, w_uv, rel_bias, B, L):
    T = B * L
    QB, TK, NH, C, DH = ATT_QB, ATT_TK, ATT_HEADS, ATT_LATENT, ATT_HEADDIM
    nq = L // QB
    topk = min(TOPK_MAX, L // 4)
    n_rel = 2 * TK + QB - 1
    period = n_rel + 1
    rel_of = np.arange(n_rel) - (TK + QB - 1)
    far_bucket = int(_t5_bucket_np(np.array([-(TK + 1)]))[0])
    table = (rel_bias.astype(F32) - rel_bias[far_bucket].astype(F32)[None, :]) * LOG2E
    v = jnp.take(table, jnp.asarray(_t5_bucket_np(rel_of)), axis=0).T
    u = jnp.pad(v, ((0, 0), (0, 1)))[:, (QB - 1 - np.arange(period)) % period]
    wuk = w_uk.astype(BF16)
    wuvt = jnp.transpose(w_uv, (0, 2, 1)).astype(BF16)

    return pl.pallas_call(
        functools.partial(_attn_kernel, topk=topk),
        out_shape=jax.ShapeDtypeStruct((T, NH * DH), BF16),
        grid=(B, nq),
        in_specs=[
            pl.BlockSpec((QB, NH * DH), lambda b, i: (b * nq + i, 0)),
            pl.BlockSpec((QB, IDX_HEADS * IDX_HEADDIM), lambda b, i: (b * nq + i, 0)),
            pl.BlockSpec((QB, LANES), lambda b, i: (b * nq + i, 5)),
            pl.BlockSpec((L, C), lambda b, i: (b, 0)),
            pl.BlockSpec((L // TK, C, TK), lambda b, i: (b, 0, 0)),
            pl.BlockSpec((L // (2 * TK), C, 2 * TK), lambda b, i: (b, 0, 0)),
            pl.BlockSpec((L, LANES), lambda b, i: (b, 0)),
            _const_spec((NH, period)),
            _const_spec((NH, C, DH)),
            _const_spec((NH, DH, C)),
        ],
        out_specs=pl.BlockSpec((QB, NH * DH), lambda b, i: (b * nq + i, 0)),
        scratch_shapes=[pltpu.VMEM((NH, 2 * TK, QB), F32),
                        pltpu.VMEM((L // TK, TK, QB), jnp.int32),
                        pltpu.VMEM((L // TK, TK, QB), jnp.int16),
                        pltpu.VMEM((L // TK, TK, QB), jnp.int16),
                        pltpu.VMEM((NH * DH, QB), BF16),
                        pltpu.VMEM((IDX_HEADS * IDX_HEADDIM + LANES - IDX_HEADDIM, QB), BF16),
                        pltpu.VMEM((NH, C, QB), BF16),
                        pltpu.VMEM((NH, 1, QB), F32),
                        pltpu.VMEM((NH, 1, QB), F32),
                        pltpu.VMEM((NH, C, QB), F32),
                        pltpu.VMEM((NH * DH, QB), F32)],
        compiler_params=_cparams("parallel", "arbitrary"),
    )(qc, small, small, ckv_n, ckv_t, ckv_t2, kidx_n, u, wuk, wuvt)


def _merge_kernel(ys_ref, ya_ref, gs_ref, ga_ref, h_ref, wbs_ref, wba_ref, wo_ref, g_ref, b_ref,
                  hf_ref, hb_ref):
    sig = lambda v: 1.0 / (1.0 + jnp.exp(-v.astype(F32)))
    m = sig(gs_ref[...]) * _dot(ys_ref[...], wbs_ref[...]) + sig(ga_ref[...]) * _dot(ya_ref[...], wba_ref[...])
    mix = _dot(m.astype(BF16), wo_ref[...])
    h1 = _ln_rows(DN_ALPHA * h_ref[...] + mix, g_ref[...], b_ref[...])
    hf_ref[...] = h1
    hb_ref[...] = h1.astype(BF16)


def _merge(y_ssd, y_att, g_ssd, g_att, h, w_br_ssd, w_br_att, w_out, g, b, tm=512):
    T, D = h.shape
    row = lambda w, j=0: pl.BlockSpec((tm, w), lambda i, j=j: (i, j))
    return pl.pallas_call(
        _merge_kernel,
        out_shape=(jax.ShapeDtypeStruct((T, D), F32), jax.ShapeDtypeStruct((T, D), BF16)),
        grid=(T // tm,),
        in_specs=[row(SSD_INNER), row(ATT_HEADS * ATT_HEADDIM), row(D), row(D), row(D),
                  _const_spec(w_br_ssd.shape), _const_spec(w_br_att.shape), _const_spec(w_out.shape),
                  _const_spec((1, D)), _const_spec((1, D))],
        out_specs=(row(D), row(D)),
        compiler_params=_cparams("parallel"),
    )(y_ssd, y_att, g_ssd, g_att, h, w_br_ssd, w_br_att, w_out, g.reshape(1, D), b.reshape(1, D))


FFN_COLS = 1408


def _ffn_kernel(hb_ref, h_ref, wup_ref, cw_ref, cb_ref, wd_ref, lg_ref, lb_ref, o_ref, upad, tail):
    c = pl.program_id(1)
    tm = hb_ref.shape[0]
    HALO = SUBLANES

    @pl.when(c == 0)
    def _():
        tail[...] = jnp.zeros(tail.shape, F32)

    hb = hb_ref[...]
    f = None
    for cc in range(D_FF // FFN_COLS):
        halves = []
        for half in range(2):
            cols = slice(half * D_FF + cc * FFN_COLS, half * D_FF + (cc + 1) * FFN_COLS)
            upad[0:HALO, :] = tail[:, cols]
            upad[HALO:HALO + tm, :] = _dot(hb, wup_ref[:, cols])
            tail[:, cols] = upad[tm:tm + HALO, :]
            u = cb_ref[:, cols]
            for k in range(FFN_CONV):
                u = u + cw_ref[k:k + 1, cols] * upad[pl.ds(HALO - (FFN_CONV - 1) + k, tm), :]
            halves.append(u)
        g = (_silu(halves[0]) * halves[1]).astype(BF16)
        part = _dot(g, wd_ref[cc * FFN_COLS:(cc + 1) * FFN_COLS, :])
        f = part if f is None else f + part
    o_ref[...] = _ln_rows(DN_ALPHA * h_ref[...] + f, lg_ref[...], lb_ref[...])


def _conv_ffn(h1b, h1, w_up, conv_w, conv_b, w_down, lg, lb, B, L, tm=256):
    T, D = h1.shape
    nt = L // tm
    tok = pl.BlockSpec((tm, D), lambda b, c: (b * nt + c, 0))
    return pl.pallas_call(
        _ffn_kernel,
        out_shape=jax.ShapeDtypeStruct((T, D), F32),
        grid=(B, nt),
        in_specs=[tok, tok, _const_spec(w_up.shape), _const_spec(conv_w.shape), _const_spec((1, 2 * D_FF)),
                  _const_spec(w_down.shape), _const_spec((1, D)), _const_spec((1, D))],
        out_specs=tok,
        scratch_shapes=[pltpu.VMEM((tm + SUBLANES, FFN_COLS), F32),
                        pltpu.VMEM((SUBLANES, 2 * D_FF), F32)],
        compiler_params=_cparams("parallel", "arbitrary"),
    )(h1b, h1, w_up, conv_w, conv_b.reshape(1, -1), w_down, lg.reshape(1, D), lb.reshape(1, D))


def _split_w_in(w_in):
    sizes = (SSD_INNER, SSD_XBC, SSD_HEADS, ATT_HEADS * ATT_HEADDIM, ATT_LATENT,
             IDX_HEADS * IDX_HEADDIM, IDX_HEADDIM, IDX_HEADS, D_MODEL, D_MODEL)
    pts = np.cumsum(sizes)[:-1].tolist()
    return jnp.split(w_in, pts, axis=1)


def _pad_cols(w, width):
    return jnp.pad(w, ((0, 0), (0, width - w.shape[1])))


def kernel(x, ln_in_g, ln_in_b, w_in, ssd_conv_w, ssd_conv_b, ssd_dt_bias, ssd_A_log, ssd_D, ssd_norm_g, att_kv_norm_g, att_w_uk, att_w_uv, idx_k_norm_g, idx_k_norm_b, rel_bias, w_br_ssd, w_br_att, w_out, ln1_g, ln1_b, ffn_w_up, ffn_conv_w, ffn_conv_b, ffn_w_down, ln2_g, ln2_b):
    B, L, D = x.shape
    T = B * L
    l = 0
    wz, wxbc, wdt, wq, wckv, wqi, wki, wwi, wgs, wga = _split_w_in(w_in[l])
    w_small = jnp.concatenate([wqi, _pad_cols(wki, LANES), _pad_cols(wwi, LANES), _pad_cols(wdt, LANES)],
                              axis=1)
    weights = [w.astype(BF16) for w in (wxbc, wz, wq, wckv, wgs, wga, w_small)]
    h, xbc, z, q, ckv, g_ssd, g_att, small = _ln_inproj(
        x.reshape(T, D), ln_in_g, ln_in_b, weights, [BF16] * 6 + [F32])

    y_ssd = _ssd_branch(xbc, z, small, ssd_conv_w[l], ssd_conv_b[l], ssd_dt_bias[l], ssd_A_log[l], ssd_D[l],
                        ssd_norm_g[l], B, L)
    ckv_n, ckv_t, ckv_t2, kidx_n = _dsa_prep(ckv, small, att_kv_norm_g[l], idx_k_norm_g[l], idx_k_norm_b[l])
    y_att = _dsa_attention(q, small, ckv_n, ckv_t, ckv_t2, kidx_n, att_w_uk[l], att_w_uv[l], rel_bias, B, L)

    h1, h1b = _merge(y_ssd, y_att, g_ssd, g_att, h, w_br_ssd[l].astype(BF16), w_br_att[l].astype(BF16),
                     w_out[l].astype(BF16), ln1_g[l], ln1_b[l])
    out = _conv_ffn(h1b, h1, ffn_w_up[l].astype(BF16), ffn_conv_w[l], ffn_conv_b[l],
                    ffn_w_down[l].astype(BF16), ln2_g[l], ln2_b[l], B, L)
    return out.reshape(B, L, D).astype(x.dtype)
```

```python
import functools
import math

import numpy as np
import jax
import jax.numpy as jnp
from jax import lax
from jax.experimental import pallas as pl
from jax.experimental.pallas import tpu as pltpu

F32 = jnp.float32
BF16 = jnp.bfloat16

D_MODEL = 1024
CHUNK = 64
SSD_INNER = 2048
SSD_HEADDIM = 64
SSD_HEADS = 32
SSD_GROUPS = 4
SSD_STATE = 128
SSD_CONV = 4
SSD_XBC = SSD_INNER + 2 * SSD_GROUPS * SSD_STATE
ATT_HEADS = 16
ATT_HEADDIM = 64
ATT_LATENT = 256
IDX_HEADS = 8
IDX_HEADDIM = 64
TOPK_MAX = 256
REL_BUCKETS = 32
REL_MAX_DIST = 128
D_FF = 2816
FFN_CONV = 3
DN_ALPHA = 2.0 ** 0.25
LN_EPS = 1e-5

LANES = 128
SUBLANES = 8
VMEM_LIMIT = 56 * 1024 * 1024

SSD_Q = 128
ATT_QB = 256
ATT_TK = 256
INT_MIN = -2 ** 31
HALF16 = 2 ** 15
NEG_BIG = -1e30
LOG2E = math.log2(math.e)


def _cparams(*sem):
    return pltpu.CompilerParams(dimension_semantics=sem, vmem_limit_bytes=VMEM_LIMIT)


def _const_spec(shape):
    nd = len(shape)
    return pl.BlockSpec(shape, lambda *_: (0,) * nd, pipeline_mode=pl.Buffered(1))


def _silu(x):
    return x / (1.0 + jnp.exp(-x))


def _softplus(x):
    return jnp.maximum(x, 0.0) + jnp.log1p(jnp.exp(-jnp.abs(x)))


def _ln_rows(x, g, b):
    mu = jnp.mean(x, -1, keepdims=True)
    xc = x - mu
    var = jnp.mean(xc * xc, -1, keepdims=True)
    return xc * lax.rsqrt(var + LN_EPS) * g + b


def _dot(a, b, **kw):
    return jnp.dot(a, b, preferred_element_type=F32, **kw)


def _rows_back(win, d, halo, n):
    shifted = win if d == 0 else pltpu.roll(win, d, 0)
    return shifted[halo:halo + n, :]


def _dot_exact01(m01, x, x_on_left):
    total = None
    r = x
    for _ in range(3):
        part = r.astype(BF16)
        r = r - part.astype(F32)
        t = _dot(part, m01) if x_on_left else _dot(m01, part)
        total = t if total is None else total + t
    return total


def _dot_nt(a, b):
    return lax.dot_general(a, b, (((1,), (1,)), ((), ())), preferred_element_type=F32)


def _dot_tn(a, b):
    return lax.dot_general(a, b, (((0,), (0,)), ((), ())), preferred_element_type=F32)


def _inproj_kernel(x_ref, g_ref, b_ref, *refs):
    n = (len(refs) - 1) // 2
    w_refs, h_ref, o_refs = refs[:n], refs[n], refs[n + 1:]
    h = _ln_rows(x_ref[...], g_ref[...], b_ref[...])
    h_ref[...] = h
    hb = h.astype(BF16)
    for w_ref, o_ref in zip(w_refs, o_refs):
        o_ref[...] = _dot(hb, w_ref[...]).astype(o_ref.dtype)


def _ln_inproj(x2, g, b, weights, out_dtypes, tm=512):
    T, D = x2.shape
    row = lambda w: pl.BlockSpec((tm, w), lambda i: (i, 0))
    widths = [w.shape[1] for w in weights]
    return pl.pallas_call(
        _inproj_kernel,
        out_shape=(jax.ShapeDtypeStruct((T, D), F32),
                   *[jax.ShapeDtypeStruct((T, n), dt) for n, dt in zip(widths, out_dtypes)]),
        grid=(T // tm,),
        in_specs=[row(D), _const_spec((1, D)), _const_spec((1, D)), *[_const_spec(w.shape) for w in weights]],
        out_specs=(row(D), *[row(n) for n in widths]),
        compiler_params=_cparams("parallel"),
    )(x2, g.reshape(1, D), b.reshape(1, D), *weights)


def _ssd_kernel(xbc_ref, z_ref, dt_ref, convw_ref, convb_ref, dtb_ref, dtbT_ref,
                a_ref, aT_ref, dskip_ref, ng_ref, e_ref, y_ref, xpad, state, ybuf):
    c = pl.program_id(1)
    Q = xbc_ref.shape[0]
    G, N, P = SSD_GROUPS, SSD_STATE, SSD_HEADDIM
    GW = SSD_INNER // G
    HALO = SUBLANES

    @pl.when(c == 0)
    def _():
        xpad[0:HALO, :] = jnp.zeros((HALO, SSD_XBC), F32)
        state[...] = jnp.zeros(state.shape, F32)

    @pl.when(c > 0)
    def _():
        xpad[0:HALO, :] = xpad[Q:Q + HALO, :]

    xpad[HALO:HALO + Q, :] = xbc_ref[...].astype(F32)
    conv = convb_ref[...]
    xwin = xpad[0:HALO + Q, :]
    for k in range(SSD_CONV):
        conv = conv + convw_ref[k:k + 1, :] * _rows_back(xwin, SSD_CONV - 1 - k, HALO, Q)
    xc = _silu(conv)
    xs = xc[:, :SSD_INNER]

    dt_blk = dt_ref[...]
    dt = _softplus(dt_blk[:, :SSD_HEADS] + dtb_ref[...])
    a = dt * a_ref[...]
    dtT = _softplus(dt_blk.T[:SSD_HEADS, :] + dtbT_ref[...])
    aT = dtT * aT_ref[...]

    row = lax.broadcasted_iota(jnp.int32, (Q, Q), 0)
    col = lax.broadcasted_iota(jnp.int32, (Q, Q), 1)
    causal = row >= col
    tril = jnp.where(causal, 1.0, 0.0).astype(BF16)
    triu = jnp.where(row <= col, 1.0, 0.0).astype(BF16)
    acum = _dot_exact01(tril, a, False)
    acumT = _dot_exact01(triu, aT, True)

    e = e_ref[...]
    dt_e = _dot_exact01(e, dt, True)
    eac_e = _dot_exact01(e, jnp.exp(acum), True)
    ds_e = _dot_exact01(e, jnp.exp(acum[Q - 1:Q, :] - acum), True)
    X = xs * dt_e
    Xb = X.astype(BF16)
    Xd = (X * ds_e).astype(BF16)
    lane = lax.broadcasted_iota(jnp.int32, (Q, LANES), 1)
    lo_half = lane < P

    for g in range(G):
        Bg = xc[:, SSD_INNER + g * N:SSD_INNER + (g + 1) * N].astype(BF16)
        Cg = xc[:, SSD_INNER + G * N + g * N:SSD_INNER + G * N + (g + 1) * N].astype(BF16)
        cb = _dot_nt(Cg, Bg)
        st = state[g]
        eg = eac_e[:, g * GW:(g + 1) * GW]
        yoff = _dot(Cg, st.astype(BF16)) * eg
        new = _dot_tn(Bg, Xd[:, g * GW:(g + 1) * GW])
        state[g] = st * eg[Q - 1:Q, :] + new
        for jj in range(GW // LANES):
            p = g * (GW // LANES) + jj
            h0 = 2 * p
            Xp = Xb[:, p * LANES:(p + 1) * LANES]
            zero = jnp.zeros_like(Xp)
            top = jnp.where(lo_half, Xp, zero)
            bot = jnp.where(lo_half, zero, Xp)
            yd = None
            for hh, rhs in ((h0, top), (h0 + 1, bot)):
                seg = acum[:, hh:hh + 1] - acumT[hh:hh + 1, :]
                decay = jnp.exp(jnp.where(causal, seg, -jnp.inf))
                part = _dot((cb * decay).astype(BF16), rhs)
                yd = part if yd is None else yd + part
            ybuf[:, p * LANES:(p + 1) * LANES] = yd + yoff[:, jj * LANES:(jj + 1) * LANES]

    y = ybuf[...] + xs * dskip_ref[...]
    for g in range(G):
        zg = z_ref[:, g * GW:(g + 1) * GW]
        v = y[:, g * GW:(g + 1) * GW] * _silu(zg.astype(F32))
        v = v * lax.rsqrt(jnp.mean(v * v, -1, keepdims=True) + LN_EPS)
        y_ref[:, g * GW:(g + 1) * GW] = (v * ng_ref[:, g * GW:(g + 1) * GW]).astype(BF16)


def _ssd_branch(xbc, z, small, conv_w, conv_b, dt_bias, A_log, D_skip, norm_g, B, L):
    T = B * L
    Q = SSD_Q
    nc = L // Q
    H = SSD_HEADS
    A = -jnp.exp(A_log.astype(F32))
    expand = jnp.asarray(np.kron(np.eye(H, dtype=np.float32), np.ones((1, SSD_HEADDIM), np.float32)), BF16)
    d_e = jnp.repeat(D_skip.astype(F32), SSD_HEADDIM).reshape(1, SSD_INNER)
    tok = lambda w, j: pl.BlockSpec((Q, w), lambda b, c, j=j: (b * nc + c, j))
    return pl.pallas_call(
        _ssd_kernel,
        out_shape=jax.ShapeDtypeStruct((T, SSD_INNER), BF16),
        grid=(B, nc),
        in_specs=[tok(SSD_XBC, 0), tok(SSD_INNER, 0), tok(LANES, 6),
                  _const_spec((SSD_CONV, SSD_XBC)), _const_spec((1, SSD_XBC)),
                  _const_spec((1, H)), _const_spec((H, 1)), _const_spec((1, H)), _const_spec((H, 1)),
                  _const_spec((1, SSD_INNER)), _const_spec((1, SSD_INNER)), _const_spec((H, SSD_INNER))],
        out_specs=pl.BlockSpec((Q, SSD_INNER), lambda b, c: (b * nc + c, 0)),
        scratch_shapes=[pltpu.VMEM((Q + 2 * SUBLANES, SSD_XBC), F32),
                        pltpu.VMEM((SSD_GROUPS, SSD_STATE, SSD_INNER // SSD_GROUPS), F32),
                        pltpu.VMEM((Q, SSD_INNER), F32)],
        compiler_params=_cparams("parallel", "arbitrary"),
    )(xbc, z, small, conv_w, conv_b.reshape(1, -1), dt_bias.reshape(1, H), dt_bias.reshape(H, 1),
      A.reshape(1, H), A.reshape(H, 1), d_e, norm_g.reshape(1, -1), expand)


def _prep_kernel(ckv_ref, kidx_ref, kvg_ref, ig_ref, ib_ref, ckvn_ref, ckvt_ref, ckvt2_ref, kn_ref):
    c = ckv_ref[...].astype(F32)
    cn = c * lax.rsqrt(jnp.mean(c * c, -1, keepdims=True) + LN_EPS) * kvg_ref[...]
    ckvn_ref[...] = cn.astype(BF16)
    cnt = cn.T.astype(BF16)
    ckvt2_ref[0] = cnt
    ckvt_ref[0] = cnt[:, :ATT_TK]
    ckvt_ref[1] = cnt[:, ATT_TK:]
    k = kidx_ref[...]
    lane = lax.broadcasted_iota(jnp.int32, k.shape, 1)
    live = lane < IDX_HEADDIM
    mu = jnp.sum(k, -1, keepdims=True) * (1.0 / IDX_HEADDIM)
    kc = jnp.where(live, k - mu, 0.0)
    var = jnp.sum(kc * kc, -1, keepdims=True) * (1.0 / IDX_HEADDIM)
    kn = jnp.where(live, kc * lax.rsqrt(var + LN_EPS) * ig_ref[...] + ib_ref[...], 0.0)
    kn_ref[...] = kn.astype(BF16)


def _dsa_prep(qc, small, kv_g, idx_g, idx_b):
    T = qc.shape[0]
    tm = 2 * ATT_TK
    pad = lambda v: jnp.pad(v.astype(F32), (0, LANES - IDX_HEADDIM)).reshape(1, LANES)
    return pl.pallas_call(
        _prep_kernel,
        out_shape=(jax.ShapeDtypeStruct((T, ATT_LATENT), BF16),
                   jax.ShapeDtypeStruct((T // ATT_TK, ATT_LATENT, ATT_TK), BF16),
                   jax.ShapeDtypeStruct((T // tm, ATT_LATENT, tm), BF16),
                   jax.ShapeDtypeStruct((T, LANES), BF16)),
        grid=(T // tm,),
        in_specs=[pl.BlockSpec((tm, ATT_LATENT), lambda i: (i, 0)),
                  pl.BlockSpec((tm, LANES), lambda i: (i, 4)),
                  _const_spec((1, ATT_LATENT)), _const_spec((1, LANES)), _const_spec((1, LANES))],
        out_specs=(pl.BlockSpec((tm, ATT_LATENT), lambda i: (i, 0)),
                   pl.BlockSpec((2, ATT_LATENT, ATT_TK), lambda i: (i, 0, 0)),
                   pl.BlockSpec((1, ATT_LATENT, tm), lambda i: (i, 0, 0)),
                   pl.BlockSpec((tm, LANES), lambda i: (i, 0))),
        compiler_params=_cparams("parallel"),
    )(qc, small, kv_g.reshape(1, -1), pad(idx_g), pad(idx_b))


def _attn_kernel(q_ref, qidx_ref, widx_ref, ckv_ref, ckvt_ref, ckvt2_ref, kidx_ref, brow_ref, wuk_ref, wuvt_ref,
                 o_ref, bias_ref, keys, khi, klo, qt, qit, qlat, m_scr, l_scr, acc, ot, *, topk):
    i = pl.program_id(1)
    QB, TK, NH, C, DH, DI = ATT_QB, ATT_TK, ATT_HEADS, ATT_LATENT, ATT_HEADDIM, IDX_HEADDIM
    n_tiles = i + 1

    @pl.when(i == 0)
    def _():
        period = brow_ref.shape[1]
        for h in range(NH):
            base = jnp.broadcast_to(brow_ref[h:h + 1, :], (2 * TK, period))
            bias_ref[h] = pltpu.roll(base, 0, 1, stride=1, stride_axis=0)[:, :QB]

    qt[...] = q_ref[...].astype(F32).T.astype(BF16)
    for h in range(NH):
        ql = _dot(wuk_ref[h], qt[h * DH:(h + 1) * DH, :])
        qlat[h] = (ql * (DH ** -0.5 * LOG2E)).astype(BF16)

    qit[0:IDX_HEADS * DI, :] = qidx_ref[...].T.astype(BF16)
    qit[IDX_HEADS * DI:, :] = jnp.zeros((LANES - DI, QB), BF16)
    wt = widx_ref[...].T[:IDX_HEADS, :] * (IDX_HEADS ** -0.5 * DI ** -0.5)
    qchunk = (i * QB + lax.broadcasted_iota(jnp.int32, (TK, QB), 1)) // CHUNK
    kin = lax.broadcasted_iota(jnp.int32, (TK, QB), 0)

    def score_tile(j, diagonal):
        kt = kidx_ref[pl.ds(pl.multiple_of(j * TK, TK), TK), :]
        s = jnp.zeros((TK, QB), F32)
        for h in range(IDX_HEADS):
            sh = _dot(kt, qit[h * DI:h * DI + LANES, :])
            s = s + wt[h:h + 1, :] * jnp.maximum(sh, 0.0)
        bits = pltpu.bitcast(s, jnp.int32)
        skey = bits ^ ((bits >> 31) & 0x7FFFFFFF)
        if diagonal:
            adm = ((j * TK + kin) // CHUNK) <= qchunk
            skey = jnp.where(adm, skey, INT_MIN)
        keys[j] = skey
        khi[j] = (skey >> 16).astype(jnp.int16)
        klo[j] = ((skey & 0xFFFF) - HALF16).astype(jnp.int16)

    def score_earlier(j, _):
        score_tile(j, False)
        return 0

    lax.fori_loop(0, i, score_earlier, 0)
    score_tile(i, True)

    PK = 2 * SUBLANES

    @pl.when(n_tiles % 2 == 1)
    def _():
        filler = jnp.full((TK, QB), -HALF16, jnp.int16)
        khi[n_tiles] = filler
        klo[n_tiles] = filler

    def count16(ref, test):
        def body(p, cnt):
            for j in (2 * p, 2 * p + 1):
                hit = jnp.where(test(ref[j]), jnp.int16(1), jnp.int16(0))
                for r in range(TK // PK):
                    cnt = cnt + hit[r * PK:(r + 1) * PK, :]
            return cnt
        cnt = lax.fori_loop(0, (n_tiles + 1) // 2, body, jnp.zeros((PK, QB), jnp.int16))
        return jnp.sum(cnt.astype(jnp.int32), axis=0, keepdims=True)

    def search16(ref, base_count):
        def bit_step(b, t):
            cand = t + jnp.left_shift(jnp.int32(1), 15 - b)
            c16 = (cand - HALF16).astype(jnp.int16)
            ok = base_count + count16(ref, lambda v: v >= c16) >= topk
            return jnp.where(ok, cand, t)
        return lax.fori_loop(0, 16, bit_step, jnp.zeros((1, QB), jnp.int32))

    zero = jnp.zeros((1, QB), jnp.int32)
    thi = search16(khi, zero)
    thi16 = (thi - HALF16).astype(jnp.int16)
    above = count16(khi, lambda v: v > thi16)

    def mark_equal(j, _):
        klo[j] = jnp.where(khi[j] == thi16, klo[j], jnp.int16(-HALF16))
        return 0

    lax.fori_loop(0, n_tiles, mark_equal, 0)
    tlo = search16(klo, above)
    tau = jnp.left_shift(thi - HALF16, 16) | tlo
    tau = jnp.maximum(tau, INT_MIN + 1)

    m_scr[...] = jnp.full(m_scr.shape, NEG_BIG, F32)
    l_scr[...] = jnp.zeros(l_scr.shape, F32)
    acc[...] = jnp.zeros(acc.shape, F32)

    def attend(kv, kvts, key_tiles, bias_rows):
        maskb = jnp.where(jnp.concatenate(key_tiles, axis=0) >= tau, 0.0, NEG_BIG)
        for h in range(NH):
            s = _dot(kv, qlat[h]) + maskb
            if bias_rows is not None:
                s = s + bias_ref[h, bias_rows, :]
            m_old = m_scr[h]
            m_new = jnp.maximum(m_old, jnp.max(s, axis=0, keepdims=True))
            alpha = jnp.exp2(m_old - m_new)
            pr = jnp.exp2(s - m_new)
            l_scr[h] = alpha * l_scr[h] + jnp.sum(pr, axis=0, keepdims=True)
            pb = pr.astype(BF16)
            pv, row = None, 0
            for kvt in kvts:
                part = _dot(kvt, pb[row:row + kvt.shape[1], :])
                pv = part if pv is None else pv + part
                row += kvt.shape[1]
            acc[h] = alpha * acc[h] + pv
            m_scr[h] = m_new

    def kv_rows(first_tile, n):
        return ckv_ref[pl.ds(pl.multiple_of(first_tile * TK, TK), n * TK), :]

    def far_pair(jp, _):
        attend(kv_rows(2 * jp, 2), [ckvt2_ref[jp]], [keys[2 * jp], keys[2 * jp + 1]], None)
        return 0

    n_far = jnp.maximum(i - 1, 0)
    lax.fori_loop(0, n_far // 2, far_pair, 0)

    @pl.when(n_far % 2 == 1)
    def _():
        attend(kv_rows(n_far - 1, 1), [ckvt_ref[n_far - 1]], [keys[n_far - 1]], None)

    @pl.when(i > 0)
    def _():
        attend(kv_rows(i - 1, 2), [ckvt_ref[i - 1], ckvt_ref[i]], [keys[i - 1], keys[i]], slice(0, 2 * TK))

    @pl.when(i == 0)
    def _():
        attend(kv_rows(0, 1), [ckvt_ref[0]], [keys[0]], slice(TK, 2 * TK))

    for h in range(NH):
        ol = (acc[h] / l_scr[h]).astype(BF16)
        ot[h * DH:(h + 1) * DH, :] = _dot(wuvt_ref[h], ol)
    o_ref[...] = ot[...].T.astype(o_ref.dtype)


def _t5_bucket_np(rel):
    nb = REL_BUCKETS // 2
    max_exact = nb // 2
    n = np.abs(rel).astype(np.int64)
    nn = np.maximum(n, 1)
    sq = nn * nn
    log2_sq = np.floor(np.log2(sq.astype(np.float64))).astype(np.int64)
    log2_sq = np.where(2 ** (log2_sq + 1) <= sq, log2_sq + 1, log2_sq)
    log2_sq = np.where(2 ** log2_sq > sq, log2_sq - 1, log2_sq)
    large = np.minimum(max_exact + (log2_sq - 6), nb - 1)
    return np.where(rel > 0, nb, 0) + np.where(n < max_exact, n, large)


def _dsa_attention(qc, small, ckv_n, ckv_t, ckv_t2, kidx_n, w_uk, w_uv, rel_bias, B, L):
    T = B * L
    QB, TK, NH, C, DH = ATT_QB, ATT_TK, ATT_HEADS, ATT_LATENT, ATT_HEADDIM
    nq = L // QB
    topk = min(TOPK_MAX, L // 4)
    n_rel = 2 * TK + QB - 1
    period = n_rel + 1
    rel_of = np.arange(n_rel) - (TK + QB - 1)
    far_bucket = int(_t5_bucket_np(np.array([-(TK + 1)]))[0])
    table = (rel_bias.astype(F32) - rel_bias[far_bucket].astype(F32)[None, :]) * LOG2E
    v = jnp.take(table, jnp.asarray(_t5_bucket_np(rel_of)), axis=0).T
    u = jnp.pad(v, ((0, 0), (0, 1)))[:, (QB - 1 - np.arange(period)) % period]
    wuk = w_uk.astype(BF16)
    wuvt = jnp.transpose(w_uv, (0, 2, 1)).astype(BF16)

    return pl.pallas_call(
        functools.partial(_attn_kernel, topk=topk),
        out_shape=jax.ShapeDtypeStruct((T, NH * DH), BF16),
        grid=(B, nq),
        in_specs=[
            pl.BlockSpec((QB, NH * DH), lambda b, i: (b * nq + i, 0)),
            pl.BlockSpec((QB, IDX_HEADS * IDX_HEADDIM), lambda b, i: (b * nq + i, 0)),
            pl.BlockSpec((QB, LANES), lambda b, i: (b * nq + i, 5)),
            pl.BlockSpec((L, C), lambda b, i: (b, 0)),
            pl.BlockSpec((L // TK, C, TK), lambda b, i: (b, 0, 0)),
            pl.BlockSpec((L // (2 * TK), C, 2 * TK), lambda b, i: (b, 0, 0)),
            pl.BlockSpec((L, LANES), lambda b, i: (b, 0)),
            _const_spec((NH, period)),
            _const_spec((NH, C, DH)),
            _const_spec((NH, DH, C)),
        ],
        out_specs=pl.BlockSpec((QB, NH * DH), lambda b, i: (b * nq + i, 0)),
        scratch_shapes=[pltpu.VMEM((NH, 2 * TK, QB), F32),
                        pltpu.VMEM((L // TK, TK, QB), jnp.int32),
                        pltpu.VMEM((L // TK, TK, QB), jnp.int16),
                        pltpu.VMEM((L // TK, TK, QB), jnp.int16),
                        pltpu.VMEM((NH * DH, QB), BF16),
                        pltpu.VMEM((IDX_HEADS * IDX_HEADDIM + LANES - IDX_HEADDIM, QB), BF16),
                        pltpu.VMEM((NH, C, QB), BF16),
                        pltpu.VMEM((NH, 1, QB), F32),
                        pltpu.VMEM((NH, 1, QB), F32),
                        pltpu.VMEM((NH, C, QB), F32),
                        pltpu.VMEM((NH * DH, QB), F32)],
        compiler_params=_cparams("parallel", "arbitrary"),
    )(qc, small, small, ckv_n, ckv_t, ckv_t2, kidx_n, u, wuk, wuvt)


def _merge_kernel(ys_ref, ya_ref, gs_ref, ga_ref, h_ref, wbs_ref, wba_ref, wo_ref, g_ref, b_ref,
                  hf_ref, hb_ref):
    sig = lambda v: 1.0 / (1.0 + jnp.exp(-v.astype(F32)))
    m = sig(gs_ref[...]) * _dot(ys_ref[...], wbs_ref[...]) + sig(ga_ref[...]) * _dot(ya_ref[...], wba_ref[...])
    mix = _dot(m.astype(BF16), wo_ref[...])
    h1 = _ln_rows(DN_ALPHA * h_ref[...] + mix, g_ref[...], b_ref[...])
    hf_ref[...] = h1
    hb_ref[...] = h1.astype(BF16)


def _merge(y_ssd, y_att, g_ssd, g_att, h, w_br_ssd, w_br_att, w_out, g, b, tm=512):
    T, D = h.shape
    row = lambda w, j=0: pl.BlockSpec((tm, w), lambda i, j=j: (i, j))
    return pl.pallas_call(
        _merge_kernel,
        out_shape=(jax.ShapeDtypeStruct((T, D), F32), jax.ShapeDtypeStruct((T, D), BF16)),
        grid=(T // tm,),
        in_specs=[row(SSD_INNER), row(ATT_HEADS * ATT_HEADDIM), row(D), row(D), row(D),
                  _const_spec(w_br_ssd.shape), _const_spec(w_br_att.shape), _const_spec(w_out.shape),
                  _const_spec((1, D)), _const_spec((1, D))],
        out_specs=(row(D), row(D)),
        compiler_params=_cparams("parallel"),
    )(y_ssd, y_att, g_ssd, g_att, h, w_br_ssd, w_br_att, w_out, g.reshape(1, D), b.reshape(1, D))


FFN_COLS = 1408


def _ffn_kernel(hb_ref, h_ref, wup_ref, cw_ref, cb_ref, wd_ref, lg_ref, lb_ref, o_ref, upad, tail):
    c = pl.program_id(1)
    tm = hb_ref.shape[0]
    HALO = SUBLANES

    @pl.when(c == 0)
    def _():
        tail[...] = jnp.zeros(tail.shape, F32)

    hb = hb_ref[...]
    f = None
    for cc in range(D_FF // FFN_COLS):
        halves = []
        for half in range(2):
            cols = slice(half * D_FF + cc * FFN_COLS, half * D_FF + (cc + 1) * FFN_COLS)
            upad[0:HALO, :] = tail[:, cols]
            upad[HALO:HALO + tm, :] = _dot(hb, wup_ref[:, cols])
            tail[:, cols] = upad[tm:tm + HALO, :]
            u = cb_ref[:, cols]
            for k in range(FFN_CONV):
                u = u + cw_ref[k:k + 1, cols] * upad[pl.ds(HALO - (FFN_CONV - 1) + k, tm), :]
            halves.append(u)
        g = (_silu(halves[0]) * halves[1]).astype(BF16)
        part = _dot(g, wd_ref[cc * FFN_COLS:(cc + 1) * FFN_COLS, :])
        f = part if f is None else f + part
    o_ref[...] = _ln_rows(DN_ALPHA * h_ref[...] + f, lg_ref[...], lb_ref[...])


def _conv_ffn(h1b, h1, w_up, conv_w, conv_b, w_down, lg, lb, B, L, tm=512):
    T, D = h1.shape
    nt = L // tm
    tok = pl.BlockSpec((tm, D), lambda b, c: (b * nt + c, 0))
    return pl.pallas_call(
        _ffn_kernel,
        out_shape=jax.ShapeDtypeStruct((T, D), F32),
        grid=(B, nt),
        in_specs=[tok, tok, _const_spec(w_up.shape), _const_spec(conv_w.shape), _const_spec((1, 2 * D_FF)),
                  _const_spec(w_down.shape), _const_spec((1, D)), _const_spec((1, D))],
        out_specs=tok,
        scratch_shapes=[pltpu.VMEM((tm + SUBLANES, FFN_COLS), F32),
                        pltpu.VMEM((SUBLANES, 2 * D_FF), F32)],
        compiler_params=_cparams("parallel", "arbitrary"),
    )(h1b, h1, w_up, conv_w, conv_b.reshape(1, -1), w_down, lg.reshape(1, D), lb.reshape(1, D))


def _split_w_in(w_in):
    sizes = (SSD_INNER, SSD_XBC, SSD_HEADS, ATT_HEADS * ATT_HEADDIM, ATT_LATENT,
             IDX_HEADS * IDX_HEADDIM, IDX_HEADDIM, IDX_HEADS, D_MODEL, D_MODEL)
    pts = np.cumsum(sizes)[:-1].tolist()
    return jnp.split(w_in, pts, axis=1)


def _pad_cols(w, width):
    return jnp.pad(w, ((0, 0), (0, width - w.shape[1])))


def kernel(x, ln_in_g, ln_in_b, w_in, ssd_conv_w, ssd_conv_b, ssd_dt_bias, ssd_A_log, ssd_D, ssd_norm_g, att_kv_norm_g, att_w_uk, att_w_uv, idx_k_norm_g, idx_k_norm_b, rel_bias, w_br_ssd, w_br_att, w_out, ln1_g, ln1_b, ffn_w_up, ffn_conv_w, ffn_conv_b, ffn_w_down, ln2_g, ln2_b):
    B, L, D = x.shape
    T = B * L
    l = 0
    wz, wxbc, wdt, wq, wckv, wqi, wki, wwi, wgs, wga = _split_w_in(w_in[l])
    w_small = jnp.concatenate([wqi, _pad_cols(wki, LANES), _pad_cols(wwi, LANES), _pad_cols(wdt, LANES)],
                              axis=1)
    weights = [w.astype(BF16) for w in (wxbc, wz, wq, wckv, wgs, wga, w_small)]
    h, xbc, z, q, ckv, g_ssd, g_att, small = _ln_inproj(
        x.reshape(T, D), ln_in_g, ln_in_b, weights, [BF16] * 6 + [F32])

    y_ssd = _ssd_branch(xbc, z, small, ssd_conv_w[l], ssd_conv_b[l], ssd_dt_bias[l], ssd_A_log[l], ssd_D[l],
                        ssd_norm_g[l], B, L)
    ckv_n, ckv_t, ckv_t2, kidx_n = _dsa_prep(ckv, small, att_kv_norm_g[l], idx_k_norm_g[l], idx_k_norm_b[l])
    y_att = _dsa_attention(q, small, ckv_n, ckv_t, ckv_t2, kidx_n, att_w_uk[l], att_w_uv[l], rel_bias, B, L)

    h1, h1b = _merge(y_ssd, y_att, g_ssd, g_att, h, w_br_ssd[l].astype(BF16), w_br_att[l].astype(BF16),
                     w_out[l].astype(BF16), ln1_g[l], ln1_b[l])
    out = _conv_ffn(h1b, h1, ffn_w_up[l].astype(BF16), ffn_conv_w[l], ffn_conv_b[l],
                    ffn_w_down[l].astype(BF16), ln2_g[l], ln2_b[l], B, L)
    return out.reshape(B, L, D).astype(x.dtype)
```

```python
import functools
import math

import numpy as np
import jax
import jax.numpy as jnp
from jax import lax
from jax.experimental import pallas as pl
from jax.experimental.pallas import tpu as pltpu

F32 = jnp.float32
BF16 = jnp.bfloat16

D_MODEL = 1024
CHUNK = 64
SSD_INNER = 2048
SSD_HEADDIM = 64
SSD_HEADS = 32
SSD_GROUPS = 4
SSD_STATE = 128
SSD_CONV = 4
SSD_XBC = SSD_INNER + 2 * SSD_GROUPS * SSD_STATE
ATT_HEADS = 16
ATT_HEADDIM = 64
ATT_LATENT = 256
IDX_HEADS = 8
IDX_HEADDIM = 64
TOPK_MAX = 256
REL_BUCKETS = 32
REL_MAX_DIST = 128
D_FF = 2816
FFN_CONV = 3
DN_ALPHA = 2.0 ** 0.25
LN_EPS = 1e-5

LANES = 128
SUBLANES = 8
VMEM_LIMIT = 56 * 1024 * 1024

SSD_Q = 128
ATT_QB = 256
ATT_TK = 256
INT_MIN = -2 ** 31
HALF16 = 2 ** 15
NEG_BIG = -1e30
LOG2E = math.log2(math.e)


def _cparams(*sem):
    return pltpu.CompilerParams(dimension_semantics=sem, vmem_limit_bytes=VMEM_LIMIT)


def _const_spec(shape):
    nd = len(shape)
    return pl.BlockSpec(shape, lambda *_: (0,) * nd, pipeline_mode=pl.Buffered(1))


def _silu(x):
    return x / (1.0 + jnp.exp(-x))


def _softplus(x):
    return jnp.maximum(x, 0.0) + jnp.log1p(jnp.exp(-jnp.abs(x)))


def _ln_rows(x, g, b):
    mu = jnp.mean(x, -1, keepdims=True)
    xc = x - mu
    var = jnp.mean(xc * xc, -1, keepdims=True)
    return xc * lax.rsqrt(var + LN_EPS) * g + b


def _dot(a, b, **kw):
    return jnp.dot(a, b, preferred_element_type=F32, **kw)


def _rows_back(win, d, halo, n):
    shifted = win if d == 0 else pltpu.roll(win, d, 0)
    return shifted[halo:halo + n, :]


def _dot_exact01(m01, x, x_on_left):
    total = None
    r = x
    for _ in range(3):
        part = r.astype(BF16)
        r = r - part.astype(F32)
        t = _dot(part, m01) if x_on_left else _dot(m01, part)
        total = t if total is None else total + t
    return total


def _dot_nt(a, b):
    return lax.dot_general(a, b, (((1,), (1,)), ((), ())), preferred_element_type=F32)


def _dot_tn(a, b):
    return lax.dot_general(a, b, (((0,), (0,)), ((), ())), preferred_element_type=F32)


def _inproj_kernel(x_ref, g_ref, b_ref, *refs):
    n = (len(refs) - 1) // 2
    w_refs, h_ref, o_refs = refs[:n], refs[n], refs[n + 1:]
    h = _ln_rows(x_ref[...], g_ref[...], b_ref[...])
    h_ref[...] = h
    hb = h.astype(BF16)
    for w_ref, o_ref in zip(w_refs, o_refs):
        o_ref[...] = _dot(hb, w_ref[...]).astype(o_ref.dtype)


def _ln_inproj(x2, g, b, weights, out_dtypes, tm=512):
    T, D = x2.shape
    row = lambda w: pl.BlockSpec((tm, w), lambda i: (i, 0))
    widths = [w.shape[1] for w in weights]
    return pl.pallas_call(
        _inproj_kernel,
        out_shape=(jax.ShapeDtypeStruct((T, D), F32),
                   *[jax.ShapeDtypeStruct((T, n), dt) for n, dt in zip(widths, out_dtypes)]),
        grid=(T // tm,),
        in_specs=[row(D), _const_spec((1, D)), _const_spec((1, D)), *[_const_spec(w.shape) for w in weights]],
        out_specs=(row(D), *[row(n) for n in widths]),
        compiler_params=_cparams("parallel"),
    )(x2, g.reshape(1, D), b.reshape(1, D), *weights)


def _ssd_kernel(xbc_ref, z_ref, dt_ref, convw_ref, convb_ref, dtb_ref, dtbT_ref,
                a_ref, aT_ref, dskip_ref, ng_ref, e_ref, y_ref, xpad, state, ybuf):
    c = pl.program_id(1)
    Q = xbc_ref.shape[0]
    G, N, P = SSD_GROUPS, SSD_STATE, SSD_HEADDIM
    GW = SSD_INNER // G
    HALO = SUBLANES

    @pl.when(c == 0)
    def _():
        xpad[0:HALO, :] = jnp.zeros((HALO, SSD_XBC), F32)
        state[...] = jnp.zeros(state.shape, F32)

    @pl.when(c > 0)
    def _():
        xpad[0:HALO, :] = xpad[Q:Q + HALO, :]

    xpad[HALO:HALO + Q, :] = xbc_ref[...].astype(F32)
    conv = convb_ref[...]
    xwin = xpad[0:HALO + Q, :]
    for k in range(SSD_CONV):
        conv = conv + convw_ref[k:k + 1, :] * _rows_back(xwin, SSD_CONV - 1 - k, HALO, Q)
    xc = _silu(conv)
    xs = xc[:, :SSD_INNER]

    dt_blk = dt_ref[...]
    dt = _softplus(dt_blk[:, :SSD_HEADS] + dtb_ref[...])
    a = dt * a_ref[...]
    dtT = _softplus(dt_blk.T[:SSD_HEADS, :] + dtbT_ref[...])
    aT = dtT * aT_ref[...]

    row = lax.broadcasted_iota(jnp.int32, (Q, Q), 0)
    col = lax.broadcasted_iota(jnp.int32, (Q, Q), 1)
    causal = row >= col
    tril = jnp.where(causal, 1.0, 0.0).astype(BF16)
    triu = jnp.where(row <= col, 1.0, 0.0).astype(BF16)
    acum = _dot_exact01(tril, a, False)
    acumT = _dot_exact01(triu, aT, True)

    e = e_ref[...]
    dt_e = _dot_exact01(e, dt, True)
    eac_e = _dot_exact01(e, jnp.exp(acum), True)
    ds_e = _dot_exact01(e, jnp.exp(acum[Q - 1:Q, :] - acum), True)
    X = xs * dt_e
    Xb = X.astype(BF16)
    Xd = (X * ds_e).astype(BF16)
    lane = lax.broadcasted_iota(jnp.int32, (Q, LANES), 1)
    lo_half = lane < P

    for g in range(G):
        Bg = xc[:, SSD_INNER + g * N:SSD_INNER + (g + 1) * N].astype(BF16)
        Cg = xc[:, SSD_INNER + G * N + g * N:SSD_INNER + G * N + (g + 1) * N].astype(BF16)
        cb = _dot_nt(Cg, Bg)
        st = state[g]
        eg = eac_e[:, g * GW:(g + 1) * GW]
        yoff = _dot(Cg, st.astype(BF16)) * eg
        new = _dot_tn(Bg, Xd[:, g * GW:(g + 1) * GW])
        state[g] = st * eg[Q - 1:Q, :] + new
        for jj in range(GW // LANES):
            p = g * (GW // LANES) + jj
            h0 = 2 * p
            Xp = Xb[:, p * LANES:(p + 1) * LANES]
            zero = jnp.zeros_like(Xp)
            top = jnp.where(lo_half, Xp, zero)
            bot = jnp.where(lo_half, zero, Xp)
            yd = None
            for hh, rhs in ((h0, top), (h0 + 1, bot)):
                seg = acum[:, hh:hh + 1] - acumT[hh:hh + 1, :]
                decay = jnp.exp(jnp.where(causal, seg, -jnp.inf))
                part = _dot((cb * decay).astype(BF16), rhs)
                yd = part if yd is None else yd + part
            ybuf[:, p * LANES:(p + 1) * LANES] = yd + yoff[:, jj * LANES:(jj + 1) * LANES]

    y = ybuf[...] + xs * dskip_ref[...]
    for g in range(G):
        zg = z_ref[:, g * GW:(g + 1) * GW]
        v = y[:, g * GW:(g + 1) * GW] * _silu(zg.astype(F32))
        v = v * lax.rsqrt(jnp.mean(v * v, -1, keepdims=True) + LN_EPS)
        y_ref[:, g * GW:(g + 1) * GW] = (v * ng_ref[:, g * GW:(g + 1) * GW]).astype(BF16)


def _ssd_branch(xbc, z, small, conv_w, conv_b, dt_bias, A_log, D_skip, norm_g, B, L):
    T = B * L
    Q = SSD_Q
    nc = L // Q
    H = SSD_HEADS
    A = -jnp.exp(A_log.astype(F32))
    expand = jnp.asarray(np.kron(np.eye(H, dtype=np.float32), np.ones((1, SSD_HEADDIM), np.float32)), BF16)
    d_e = jnp.repeat(D_skip.astype(F32), SSD_HEADDIM).reshape(1, SSD_INNER)
    tok = lambda w, j: pl.BlockSpec((Q, w), lambda b, c, j=j: (b * nc + c, j))
    return pl.pallas_call(
        _ssd_kernel,
        out_shape=jax.ShapeDtypeStruct((T, SSD_INNER), BF16),
        grid=(B, nc),
        in_specs=[tok(SSD_XBC, 0), tok(SSD_INNER, 0), tok(LANES, 6),
                  _const_spec((SSD_CONV, SSD_XBC)), _const_spec((1, SSD_XBC)),
                  _const_spec((1, H)), _const_spec((H, 1)), _const_spec((1, H)), _const_spec((H, 1)),
                  _const_spec((1, SSD_INNER)), _const_spec((1, SSD_INNER)), _const_spec((H, SSD_INNER))],
        out_specs=pl.BlockSpec((Q, SSD_INNER), lambda b, c: (b * nc + c, 0)),
        scratch_shapes=[pltpu.VMEM((Q + 2 * SUBLANES, SSD_XBC), F32),
                        pltpu.VMEM((SSD_GROUPS, SSD_STATE, SSD_INNER // SSD_GROUPS), F32),
                        pltpu.VMEM((Q, SSD_INNER), F32)],
        compiler_params=_cparams("parallel", "arbitrary"),
    )(xbc, z, small, conv_w, conv_b.reshape(1, -1), dt_bias.reshape(1, H), dt_bias.reshape(H, 1),
      A.reshape(1, H), A.reshape(H, 1), d_e, norm_g.reshape(1, -1), expand)


def _prep_kernel(ckv_ref, kidx_ref, kvg_ref, ig_ref, ib_ref, ckvn_ref, ckvt_ref, ckvt2_ref, kn_ref):
    c = ckv_ref[...].astype(F32)
    cn = c * lax.rsqrt(jnp.mean(c * c, -1, keepdims=True) + LN_EPS) * kvg_ref[...]
    ckvn_ref[...] = cn.astype(BF16)
    cnt = cn.T.astype(BF16)
    ckvt2_ref[0] = cnt
    ckvt_ref[0] = cnt[:, :ATT_TK]
    ckvt_ref[1] = cnt[:, ATT_TK:]
    k = kidx_ref[...]
    lane = lax.broadcasted_iota(jnp.int32, k.shape, 1)
    live = lane < IDX_HEADDIM
    mu = jnp.sum(k, -1, keepdims=True) * (1.0 / IDX_HEADDIM)
    kc = jnp.where(live, k - mu, 0.0)
    var = jnp.sum(kc * kc, -1, keepdims=True) * (1.0 / IDX_HEADDIM)
    kn = jnp.where(live, kc * lax.rsqrt(var + LN_EPS) * ig_ref[...] + ib_ref[...], 0.0)
    kn_ref[...] = kn.astype(BF16)


def _dsa_prep(qc, small, kv_g, idx_g, idx_b):
    T = qc.shape[0]
    tm = 2 * ATT_TK
    pad = lambda v: jnp.pad(v.astype(F32), (0, LANES - IDX_HEADDIM)).reshape(1, LANES)
    return pl.pallas_call(
        _prep_kernel,
        out_shape=(jax.ShapeDtypeStruct((T, ATT_LATENT), BF16),
                   jax.ShapeDtypeStruct((T // ATT_TK, ATT_LATENT, ATT_TK), BF16),
                   jax.ShapeDtypeStruct((T // tm, ATT_LATENT, tm), BF16),
                   jax.ShapeDtypeStruct((T, LANES), BF16)),
        grid=(T // tm,),
        in_specs=[pl.BlockSpec((tm, ATT_LATENT), lambda i: (i, 0)),
                  pl.BlockSpec((tm, LANES), lambda i: (i, 4)),
                  _const_spec((1, ATT_LATENT)), _const_spec((1, LANES)), _const_spec((1, LANES))],
        out_specs=(pl.BlockSpec((tm, ATT_LATENT), lambda i: (i, 0)),
                   pl.BlockSpec((2, ATT_LATENT, ATT_TK), lambda i: (i, 0, 0)),
                   pl.BlockSpec((1, ATT_LATENT, tm), lambda i: (i, 0, 0)),
                   pl.BlockSpec((tm, LANES), lambda i: (i, 0))),
        compiler_params=_cparams("parallel"),
    )(qc, small, kv_g.reshape(1, -1), pad(idx_g), pad(idx_b))


def _attn_kernel(q_ref, qidx_ref, widx_ref, ckv_ref, ckvt_ref, ckvt2_ref, kidx_ref, brow_ref, wuk_ref, wuvt_ref,
                 o_ref, bias_ref, keys, khi, klo, qt, qit, qlat, m_scr, l_scr, acc, ot, *, topk):
    i = pl.program_id(1)
    QB, TK, NH, C, DH, DI = ATT_QB, ATT_TK, ATT_HEADS, ATT_LATENT, ATT_HEADDIM, IDX_HEADDIM
    n_tiles = i + 1

    @pl.when(i == 0)
    def _():
        period = brow_ref.shape[1]
        for h in range(NH):
            base = jnp.broadcast_to(brow_ref[h:h + 1, :], (2 * TK, period))
            bias_ref[h] = pltpu.roll(base, 0, 1, stride=1, stride_axis=0)[:, :QB]

    qt[...] = q_ref[...].astype(F32).T.astype(BF16)
    for h in range(NH):
        ql = _dot(wuk_ref[h], qt[h * DH:(h + 1) * DH, :])
        qlat[h] = (ql * (DH ** -0.5 * LOG2E)).astype(BF16)

    qit[0:IDX_HEADS * DI, :] = qidx_ref[...].T.astype(BF16)
    qit[IDX_HEADS * DI:, :] = jnp.zeros((LANES - DI, QB), BF16)
    wt = widx_ref[...].T[:IDX_HEADS, :] * (IDX_HEADS ** -0.5 * DI ** -0.5)
    qchunk = (i * QB + lax.broadcasted_iota(jnp.int32, (TK, QB), 1)) // CHUNK
    kin = lax.broadcasted_iota(jnp.int32, (TK, QB), 0)

    def score_tile(j, diagonal):
        kt = kidx_ref[pl.ds(pl.multiple_of(j * TK, TK), TK), :]
        s = jnp.zeros((TK, QB), F32)
        for h in range(IDX_HEADS):
            sh = _dot(kt, qit[h * DI:h * DI + LANES, :])
            s = s + wt[h:h + 1, :] * jnp.maximum(sh, 0.0)
        bits = pltpu.bitcast(s, jnp.int32)
        skey = bits ^ ((bits >> 31) & 0x7FFFFFFF)
        if diagonal:
            adm = ((j * TK + kin) // CHUNK) <= qchunk
            skey = jnp.where(adm, skey, INT_MIN)
        keys[j] = skey
        khi[j] = (skey >> 16).astype(jnp.int16)
        klo[j] = ((skey & 0xFFFF) - HALF16).astype(jnp.int16)

    def score_earlier(j, _):
        score_tile(j, False)
        return 0

    lax.fori_loop(0, i, score_earlier, 0)
    score_tile(i, True)

    PK = 2 * SUBLANES

    @pl.when(n_tiles % 2 == 1)
    def _():
        filler = jnp.full((TK, QB), -HALF16, jnp.int16)
        khi[n_tiles] = filler
        klo[n_tiles] = filler

    def count16(ref, test):
        def body(p, cnt):
            for j in (2 * p, 2 * p + 1):
                hit = jnp.where(test(ref[j]), jnp.int16(1), jnp.int16(0))
                for r in range(TK // PK):
                    cnt = cnt + hit[r * PK:(r + 1) * PK, :]
            return cnt
        cnt = lax.fori_loop(0, (n_tiles + 1) // 2, body, jnp.zeros((PK, QB), jnp.int16))
        return jnp.sum(cnt.astype(jnp.int32), axis=0, keepdims=True)

    def search16(ref, base_count):
        def bit_step(b, t):
            cand = t + jnp.left_shift(jnp.int32(1), 15 - b)
            c16 = (cand - HALF16).astype(jnp.int16)
            ok = base_count + count16(ref, lambda v: v >= c16) >= topk
            return jnp.where(ok, cand, t)
        return lax.fori_loop(0, 16, bit_step, jnp.zeros((1, QB), jnp.int32))

    zero = jnp.zeros((1, QB), jnp.int32)
    thi = search16(khi, zero)
    thi16 = (thi - HALF16).astype(jnp.int16)
    above = count16(khi, lambda v: v > thi16)

    def mark_equal(j, _):
        klo[j] = jnp.where(khi[j] == thi16, klo[j], jnp.int16(-HALF16))
        return 0

    lax.fori_loop(0, n_tiles, mark_equal, 0)
    tlo = search16(klo, above)
    tau = jnp.left_shift(thi - HALF16, 16) | tlo
    tau = jnp.maximum(tau, INT_MIN + 1)

    def count32(test):
        def body(j, cnt):
            hit = jnp.where(test(keys[j]), 1, 0)
            return cnt + jnp.sum(hit.reshape(TK // SUBLANES, SUBLANES, QB), axis=0)
        cnt = lax.fori_loop(0, n_tiles, body, jnp.zeros((SUBLANES, QB), jnp.int32))
        return jnp.sum(cnt, axis=0, keepdims=True)

    n_ge = count32(lambda v: v >= tau)

    @pl.when(jnp.max(n_ge) > topk)
    def _():
        need = (topk - count32(lambda v: v > tau)).astype(F32)
        r = lax.broadcasted_iota(jnp.int32, (TK, TK), 0)
        c = lax.broadcasted_iota(jnp.int32, (TK, TK), 1)
        tril = jnp.where(r >= c, 1.0, 0.0).astype(BF16)

        def demote(j, seen):
            kj = keys[j]
            tie = kj == tau
            rank = _dot(tril, jnp.where(tie, 1.0, 0.0).astype(BF16)) + seen
            keys[j] = jnp.where(tie & (rank > need), INT_MIN, kj)
            return rank[TK - 1:TK, :]

        lax.fori_loop(0, n_tiles, demote, jnp.zeros((1, QB), F32))

    m_scr[...] = jnp.full(m_scr.shape, NEG_BIG, F32)
    l_scr[...] = jnp.zeros(l_scr.shape, F32)
    acc[...] = jnp.zeros(acc.shape, F32)

    def attend(kv, kvts, key_tiles, bias_rows):
        maskb = jnp.where(jnp.concatenate(key_tiles, axis=0) >= tau, 0.0, NEG_BIG)
        for h in range(NH):
            s = _dot(kv, qlat[h]) + maskb
            if bias_rows is not None:
                s = s + bias_ref[h, bias_rows, :]
            m_old = m_scr[h]
            m_new = jnp.maximum(m_old, jnp.max(s, axis=0, keepdims=True))
            alpha = jnp.exp2(m_old - m_new)
            pr = jnp.exp2(s - m_new)
            l_scr[h] = alpha * l_scr[h] + jnp.sum(pr, axis=0, keepdims=True)
            pb = pr.astype(BF16)
            pv, row = None, 0
            for kvt in kvts:
                part = _dot(kvt, pb[row:row + kvt.shape[1], :])
                pv = part if pv is None else pv + part
                row += kvt.shape[1]
            acc[h] = alpha * acc[h] + pv
            m_scr[h] = m_new

    def kv_rows(first_tile, n):
        return ckv_ref[pl.ds(pl.multiple_of(first_tile * TK, TK), n * TK), :]

    def far_pair(jp, _):
        attend(kv_rows(2 * jp, 2), [ckvt2_ref[jp]], [keys[2 * jp], keys[2 * jp + 1]], None)
        return 0

    n_far = jnp.maximum(i - 1, 0)
    lax.fori_loop(0, n_far // 2, far_pair, 0)

    @pl.when(n_far % 2 == 1)
    def _():
        attend(kv_rows(n_far - 1, 1), [ckvt_ref[n_far - 1]], [keys[n_far - 1]], None)

    @pl.when(i > 0)
    def _():
        attend(kv_rows(i - 1, 2), [ckvt_ref[i - 1], ckvt_ref[i]], [keys[i - 1], keys[i]], slice(0, 2 * TK))

    @pl.when(i == 0)
    def _():
        attend(kv_rows(0, 1), [ckvt_ref[0]], [keys[0]], slice(TK, 2 * TK))

    for h in range(NH):
        ol = (acc[h] / l_scr[h]).astype(BF16)
        ot[h * DH:(h + 1) * DH, :] = _dot(wuvt_ref[h], ol)
    o_ref[...] = ot[...].T.astype(o_ref.dtype)


def _t5_bucket_np(rel):
    nb = REL_BUCKETS // 2
    max_exact = nb // 2
    n = np.abs(rel).astype(np.int64)
    nn = np.maximum(n, 1)
    sq = nn * nn
    log2_sq = np.floor(np.log2(sq.astype(np.float64))).astype(np.int64)
    log2_sq = np.where(2 ** (log2_sq + 1) <= sq, log2_sq + 1, log2_sq)
    log2_sq = np.where(2 ** log2_sq > sq, log2_sq - 1, log2_sq)
    large = np.minimum(max_exact + (log2_sq - 6), nb - 1)
    return np.where(rel > 0, nb, 0) + np.where(n < max_exact, n, large)


def _dsa_attention(qc, small, ckv_n, ckv_t, ckv_t2, kidx_n, w_uk, w_uv, rel_bias, B, L):
    T = B * L
    QB, TK, NH, C, DH = ATT_QB, ATT_TK, ATT_HEADS, ATT_LATENT, ATT_HEADDIM
    nq = L // QB
    topk = min(TOPK_MAX, L // 4)
    n_rel = 2 * TK + QB - 1
    period = n_rel + 1
    rel_of = np.arange(n_rel) - (TK + QB - 1)
    far_bucket = int(_t5_bucket_np(np.array([-(TK + 1)]))[0])
    table = (rel_bias.astype(F32) - rel_bias[far_bucket].astype(F32)[None, :]) * LOG2E
    v = jnp.take(table, jnp.asarray(_t5_bucket_np(rel_of)), axis=0).T
    u = jnp.pad(v, ((0, 0), (0, 1)))[:, (QB - 1 - np.arange(period)) % period]
    wuk = w_uk.astype(BF16)
    wuvt = jnp.transpose(w_uv, (0, 2, 1)).astype(BF16)

    return pl.pallas_call(
        functools.partial(_attn_kernel, topk=topk),
        out_shape=jax.ShapeDtypeStruct((T, NH * DH), BF16),
        grid=(B, nq),
        in_specs=[
            pl.BlockSpec((QB, NH * DH), lambda b, i: (b * nq + i, 0)),
            pl.BlockSpec((QB, IDX_HEADS * IDX_HEADDIM), lambda b, i: (b * nq + i, 0)),
            pl.BlockSpec((QB, LANES), lambda b, i: (b * nq + i, 5)),
            pl.BlockSpec((L, C), lambda b, i: (b, 0)),
            pl.BlockSpec((L // TK, C, TK), lambda b, i: (b, 0, 0)),
            pl.BlockSpec((L // (2 * TK), C, 2 * TK), lambda b, i: (b, 0, 0)),
            pl.BlockSpec((L, LANES), lambda b, i: (b, 0)),
            _const_spec((NH, period)),
            _const_spec((NH, C, DH)),
            _const_spec((NH, DH, C)),
        ],
        out_specs=pl.BlockSpec((QB, NH * DH), lambda b, i: (b * nq + i, 0)),
        scratch_shapes=[pltpu.VMEM((NH, 2 * TK, QB), F32),
                        pltpu.VMEM((L // TK, TK, QB), jnp.int32),
                        pltpu.VMEM((L // TK, TK, QB), jnp.int16),
                        pltpu.VMEM((L // TK, TK, QB), jnp.int16),
                        pltpu.VMEM((NH * DH, QB), BF16),
                        pltpu.VMEM((IDX_HEADS * IDX_HEADDIM + LANES - IDX_HEADDIM, QB), BF16),
                        pltpu.VMEM((NH, C, QB), BF16),
                        pltpu.VMEM((NH, 1, QB), F32),
                        pltpu.VMEM((NH, 1, QB), F32),
                        pltpu.VMEM((NH, C, QB), F32),
                        pltpu.VMEM((NH * DH, QB), F32)],
        compiler_params=_cparams("parallel", "arbitrary"),
    )(qc, small, small, ckv_n, ckv_t, ckv_t2, kidx_n, u, wuk, wuvt)


def _merge_kernel(ys_ref, ya_ref, gs_ref, ga_ref, h_ref, wbs_ref, wba_ref, wo_ref, g_ref, b_ref,
                  hf_ref, hb_ref):
    sig = lambda v: 1.0 / (1.0 + jnp.exp(-v.astype(F32)))
    m = sig(gs_ref[...]) * _dot(ys_ref[...], wbs_ref[...]) + sig(ga_ref[...]) * _dot(ya_ref[...], wba_ref[...])
    mix = _dot(m.astype(BF16), wo_ref[...])
    h1 = _ln_rows(DN_ALPHA * h_ref[...] + mix, g_ref[...], b_ref[...])
    hf_ref[...] = h1
    hb_ref[...] = h1.astype(BF16)


def _merge(y_ssd, y_att, g_ssd, g_att, h, w_br_ssd, w_br_att, w_out, g, b, tm=512):
    T, D = h.shape
    row = lambda w, j=0: pl.BlockSpec((tm, w), lambda i, j=j: (i, j))
    return pl.pallas_call(
        _merge_kernel,
        out_shape=(jax.ShapeDtypeStruct((T, D), F32), jax.ShapeDtypeStruct((T, D), BF16)),
        grid=(T // tm,),
        in_specs=[row(SSD_INNER), row(ATT_HEADS * ATT_HEADDIM), row(D), row(D), row(D),
                  _const_spec(w_br_ssd.shape), _const_spec(w_br_att.shape), _const_spec(w_out.shape),
                  _const_spec((1, D)), _const_spec((1, D))],
        out_specs=(row(D), row(D)),
        compiler_params=_cparams("parallel"),
    )(y_ssd, y_att, g_ssd, g_att, h, w_br_ssd, w_br_att, w_out, g.reshape(1, D), b.reshape(1, D))


FFN_COLS = 1408


def _ffn_kernel(hb_ref, h_ref, wup_ref, cw_ref, cb_ref, wd_ref, lg_ref, lb_ref, o_ref, upad, tail):
    c = pl.program_id(1)
    tm = hb_ref.shape[0]
    HALO = SUBLANES

    @pl.when(c == 0)
    def _():
        tail[...] = jnp.zeros(tail.shape, F32)

    hb = hb_ref[...]
    f = None
    for cc in range(D_FF // FFN_COLS):
        halves = []
        for half in range(2):
            cols = slice(half * D_FF + cc * FFN_COLS, half * D_FF + (cc + 1) * FFN_COLS)
            upad[0:HALO, :] = tail[:, cols]
            upad[HALO:HALO + tm, :] = _dot(hb, wup_ref[:, cols])
            tail[:, cols] = upad[tm:tm + HALO, :]
            u = cb_ref[:, cols]
            for k in range(FFN_CONV):
                u = u + cw_ref[k:k + 1, cols] * upad[pl.ds(HALO - (FFN_CONV - 1) + k, tm), :]
            halves.append(u)
        g = (_silu(halves[0]) * halves[1]).astype(BF16)
        part = _dot(g, wd_ref[cc * FFN_COLS:(cc + 1) * FFN_COLS, :])
        f = part if f is None else f + part
    o_ref[...] = _ln_rows(DN_ALPHA * h_ref[...] + f, lg_ref[...], lb_ref[...])


def _conv_ffn(h1b, h1, w_up, conv_w, conv_b, w_down, lg, lb, B, L, tm=512):
    T, D = h1.shape
    nt = L // tm
    tok = pl.BlockSpec((tm, D), lambda b, c: (b * nt + c, 0))
    return pl.pallas_call(
        _ffn_kernel,
        out_shape=jax.ShapeDtypeStruct((T, D), F32),
        grid=(B, nt),
        in_specs=[tok, tok, _const_spec(w_up.shape), _const_spec(conv_w.shape), _const_spec((1, 2 * D_FF)),
                  _const_spec(w_down.shape), _const_spec((1, D)), _const_spec((1, D))],
        out_specs=tok,
        scratch_shapes=[pltpu.VMEM((tm + SUBLANES, FFN_COLS), F32),
                        pltpu.VMEM((SUBLANES, 2 * D_FF), F32)],
        compiler_params=_cparams("parallel", "arbitrary"),
    )(h1b, h1, w_up, conv_w, conv_b.reshape(1, -1), w_down, lg.reshape(1, D), lb.reshape(1, D))


def _split_w_in(w_in):
    sizes = (SSD_INNER, SSD_XBC, SSD_HEADS, ATT_HEADS * ATT_HEADDIM, ATT_LATENT,
             IDX_HEADS * IDX_HEADDIM, IDX_HEADDIM, IDX_HEADS, D_MODEL, D_MODEL)
    pts = np.cumsum(sizes)[:-1].tolist()
    return jnp.split(w_in, pts, axis=1)


def _pad_cols(w, width):
    return jnp.pad(w, ((0, 0), (0, width - w.shape[1])))


def kernel(x, ln_in_g, ln_in_b, w_in, ssd_conv_w, ssd_conv_b, ssd_dt_bias, ssd_A_log, ssd_D, ssd_norm_g, att_kv_norm_g, att_w_uk, att_w_uv, idx_k_norm_g, idx_k_norm_b, rel_bias, w_br_ssd, w_br_att, w_out, ln1_g, ln1_b, ffn_w_up, ffn_conv_w, ffn_conv_b, ffn_w_down, ln2_g, ln2_b):
    B, L, D = x.shape
    T = B * L
    l = 0
    wz, wxbc, wdt, wq, wckv, wqi, wki, wwi, wgs, wga = _split_w_in(w_in[l])
    w_small = jnp.concatenate([wqi, _pad_cols(wki, LANES), _pad_cols(wwi, LANES), _pad_cols(wdt, LANES)],
                              axis=1)
    weights = [w.astype(BF16) for w in (wxbc, wz, wq, wckv, wgs, wga, w_small)]
    h, xbc, z, q, ckv, g_ssd, g_att, small = _ln_inproj(
        x.reshape(T, D), ln_in_g, ln_in_b, weights, [BF16] * 6 + [F32])

    y_ssd = _ssd_branch(xbc, z, small, ssd_conv_w[l], ssd_conv_b[l], ssd_dt_bias[l], ssd_A_log[l], ssd_D[l],
                        ssd_norm_g[l], B, L)
    ckv_n, ckv_t, ckv_t2, kidx_n = _dsa_prep(ckv, small, att_kv_norm_g[l], idx_k_norm_g[l], idx_k_norm_b[l])
    y_att = _dsa_attention(q, small, ckv_n, ckv_t, ckv_t2, kidx_n, att_w_uk[l], att_w_uv[l], rel_bias, B, L)

    h1, h1b = _merge(y_ssd, y_att, g_ssd, g_att, h, w_br_ssd[l].astype(BF16), w_br_att[l].astype(BF16),
                     w_out[l].astype(BF16), ln1_g[l], ln1_b[l])
    out = _conv_ffn(h1b, h1, ffn_w_up[l].astype(BF16), ffn_conv_w[l], ffn_conv_b[l],
                    ffn_w_down[l].astype(BF16), ln2_g[l], ln2_b[l], B, L)
    return out.reshape(B, L, D).astype(x.dtype)
```

```python
import functools
import math

import numpy as np
import jax
import jax.numpy as jnp
from jax import lax
from jax.experimental import pallas as pl
from jax.experimental.pallas import tpu as pltpu

F32 = jnp.float32
BF16 = jnp.bfloat16

D_MODEL = 1024
CHUNK = 64
SSD_INNER = 2048
SSD_HEADDIM = 64
SSD_HEADS = 32
SSD_GROUPS = 4
SSD_STATE = 128
SSD_CONV = 4
SSD_XBC = SSD_INNER + 2 * SSD_GROUPS * SSD_STATE
ATT_HEADS = 16
ATT_HEADDIM = 64
ATT_LATENT = 256
IDX_HEADS = 8
IDX_HEADDIM = 64
TOPK_MAX = 256
REL_BUCKETS = 32
REL_MAX_DIST = 128
D_FF = 2816
FFN_CONV = 3
DN_ALPHA = 2.0 ** 0.25
LN_EPS = 1e-5

LANES = 128
SUBLANES = 8
VMEM_LIMIT = 56 * 1024 * 1024

SSD_Q = 128
ATT_QB = 256
ATT_TK = 256
INT_MIN = -2 ** 31
HALF16 = 2 ** 15
NEG_BIG = -1e30
LOG2E = math.log2(math.e)


def _cparams(*sem):
    return pltpu.CompilerParams(dimension_semantics=sem, vmem_limit_bytes=VMEM_LIMIT)


def _const_spec(shape):
    nd = len(shape)
    return pl.BlockSpec(shape, lambda *_: (0,) * nd, pipeline_mode=pl.Buffered(1))


def _silu(x):
    return x / (1.0 + jnp.exp(-x))


def _softplus(x):
    return jnp.maximum(x, 0.0) + jnp.log1p(jnp.exp(-jnp.abs(x)))


def _ln_rows(x, g, b):
    mu = jnp.mean(x, -1, keepdims=True)
    xc = x - mu
    var = jnp.mean(xc * xc, -1, keepdims=True)
    return xc * lax.rsqrt(var + LN_EPS) * g + b


def _dot(a, b, **kw):
    return jnp.dot(a, b, preferred_element_type=F32, **kw)


def _rows_back(win, d, halo, n):
    shifted = win if d == 0 else pltpu.roll(win, d, 0)
    return shifted[halo:halo + n, :]


def _dot_exact01(m01, x, x_on_left):
    total = None
    r = x
    for _ in range(3):
        part = r.astype(BF16)
        r = r - part.astype(F32)
        t = _dot(part, m01) if x_on_left else _dot(m01, part)
        total = t if total is None else total + t
    return total


def _dot_nt(a, b):
    return lax.dot_general(a, b, (((1,), (1,)), ((), ())), preferred_element_type=F32)


def _dot_tn(a, b):
    return lax.dot_general(a, b, (((0,), (0,)), ((), ())), preferred_element_type=F32)


def _inproj_kernel(x_ref, g_ref, b_ref, *refs):
    n = (len(refs) - 1) // 2
    w_refs, h_ref, o_refs = refs[:n], refs[n], refs[n + 1:]
    h = _ln_rows(x_ref[...], g_ref[...], b_ref[...])
    h_ref[...] = h
    hb = h.astype(BF16)
    for w_ref, o_ref in zip(w_refs, o_refs):
        o_ref[...] = _dot(hb, w_ref[...]).astype(o_ref.dtype)


def _ln_inproj(x2, g, b, weights, out_dtypes, tm=512):
    T, D = x2.shape
    row = lambda w: pl.BlockSpec((tm, w), lambda i: (i, 0))
    widths = [w.shape[1] for w in weights]
    return pl.pallas_call(
        _inproj_kernel,
        out_shape=(jax.ShapeDtypeStruct((T, D), F32),
                   *[jax.ShapeDtypeStruct((T, n), dt) for n, dt in zip(widths, out_dtypes)]),
        grid=(T // tm,),
        in_specs=[row(D), _const_spec((1, D)), _const_spec((1, D)), *[_const_spec(w.shape) for w in weights]],
        out_specs=(row(D), *[row(n) for n in widths]),
        compiler_params=_cparams("parallel"),
    )(x2, g.reshape(1, D), b.reshape(1, D), *weights)


def _ssd_kernel(xbc_ref, z_ref, dt_ref, convw_ref, convb_ref, dtb_ref, dtbT_ref,
                a_ref, aT_ref, dskip_ref, ng_ref, e_ref, y_ref, xpad, state, ybuf):
    c = pl.program_id(1)
    Q = xbc_ref.shape[0]
    G, N, P = SSD_GROUPS, SSD_STATE, SSD_HEADDIM
    GW = SSD_INNER // G
    HALO = SUBLANES

    @pl.when(c == 0)
    def _():
        xpad[0:HALO, :] = jnp.zeros((HALO, SSD_XBC), F32)
        state[...] = jnp.zeros(state.shape, F32)

    @pl.when(c > 0)
    def _():
        xpad[0:HALO, :] = xpad[Q:Q + HALO, :]

    xpad[HALO:HALO + Q, :] = xbc_ref[...].astype(F32)
    conv = convb_ref[...]
    xwin = xpad[0:HALO + Q, :]
    for k in range(SSD_CONV):
        conv = conv + convw_ref[k:k + 1, :] * _rows_back(xwin, SSD_CONV - 1 - k, HALO, Q)
    xc = _silu(conv)
    xs = xc[:, :SSD_INNER]

    dt_blk = dt_ref[...]
    dt = _softplus(dt_blk[:, :SSD_HEADS] + dtb_ref[...])
    a = dt * a_ref[...]
    dtT = _softplus(dt_blk.T[:SSD_HEADS, :] + dtbT_ref[...])
    aT = dtT * aT_ref[...]

    row = lax.broadcasted_iota(jnp.int32, (Q, Q), 0)
    col = lax.broadcasted_iota(jnp.int32, (Q, Q), 1)
    causal = row >= col
    tril = jnp.where(causal, 1.0, 0.0).astype(BF16)
    triu = jnp.where(row <= col, 1.0, 0.0).astype(BF16)
    acum = _dot_exact01(tril, a, False)
    acumT = _dot_exact01(triu, aT, True)

    e = e_ref[...]
    dt_e = _dot_exact01(e, dt, True)
    eac_e = _dot_exact01(e, jnp.exp(acum), True)
    ds_e = _dot_exact01(e, jnp.exp(acum[Q - 1:Q, :] - acum), True)
    X = xs * dt_e
    Xb = X.astype(BF16)
    Xd = (X * ds_e).astype(BF16)
    lane = lax.broadcasted_iota(jnp.int32, (Q, LANES), 1)
    lo_half = lane < P

    for g in range(G):
        Bg = xc[:, SSD_INNER + g * N:SSD_INNER + (g + 1) * N].astype(BF16)
        Cg = xc[:, SSD_INNER + G * N + g * N:SSD_INNER + G * N + (g + 1) * N].astype(BF16)
        cb = _dot_nt(Cg, Bg)
        st = state[g]
        eg = eac_e[:, g * GW:(g + 1) * GW]
        yoff = _dot(Cg, st.astype(BF16)) * eg
        new = _dot_tn(Bg, Xd[:, g * GW:(g + 1) * GW])
        state[g] = st * eg[Q - 1:Q, :] + new
        for jj in range(GW // LANES):
            p = g * (GW // LANES) + jj
            h0 = 2 * p
            Xp = Xb[:, p * LANES:(p + 1) * LANES]
            zero = jnp.zeros_like(Xp)
            top = jnp.where(lo_half, Xp, zero)
            bot = jnp.where(lo_half, zero, Xp)
            yd = None
            for hh, rhs in ((h0, top), (h0 + 1, bot)):
                seg = acum[:, hh:hh + 1] - acumT[hh:hh + 1, :]
                decay = jnp.exp(jnp.where(causal, seg, -jnp.inf))
                part = _dot((cb * decay).astype(BF16), rhs)
                yd = part if yd is None else yd + part
            ybuf[:, p * LANES:(p + 1) * LANES] = yd + yoff[:, jj * LANES:(jj + 1) * LANES]

    y = ybuf[...] + xs * dskip_ref[...]
    for g in range(G):
        zg = z_ref[:, g * GW:(g + 1) * GW]
        v = y[:, g * GW:(g + 1) * GW] * _silu(zg.astype(F32))
        v = v * lax.rsqrt(jnp.mean(v * v, -1, keepdims=True) + LN_EPS)
        y_ref[:, g * GW:(g + 1) * GW] = (v * ng_ref[:, g * GW:(g + 1) * GW]).astype(BF16)


def _ssd_branch(xbc, z, small, conv_w, conv_b, dt_bias, A_log, D_skip, norm_g, B, L):
    T = B * L
    Q = SSD_Q
    nc = L // Q
    H = SSD_HEADS
    A = -jnp.exp(A_log.astype(F32))
    expand = jnp.asarray(np.kron(np.eye(H, dtype=np.float32), np.ones((1, SSD_HEADDIM), np.float32)), BF16)
    d_e = jnp.repeat(D_skip.astype(F32), SSD_HEADDIM).reshape(1, SSD_INNER)
    tok = lambda w, j: pl.BlockSpec((Q, w), lambda b, c, j=j: (b * nc + c, j))
    return pl.pallas_call(
        _ssd_kernel,
        out_shape=jax.ShapeDtypeStruct((T, SSD_INNER), BF16),
        grid=(B, nc),
        in_specs=[tok(SSD_XBC, 0), tok(SSD_INNER, 0), tok(LANES, 6),
                  _const_spec((SSD_CONV, SSD_XBC)), _const_spec((1, SSD_XBC)),
                  _const_spec((1, H)), _const_spec((H, 1)), _const_spec((1, H)), _const_spec((H, 1)),
                  _const_spec((1, SSD_INNER)), _const_spec((1, SSD_INNER)), _const_spec((H, SSD_INNER))],
        out_specs=pl.BlockSpec((Q, SSD_INNER), lambda b, c: (b * nc + c, 0)),
        scratch_shapes=[pltpu.VMEM((Q + 2 * SUBLANES, SSD_XBC), F32),
                        pltpu.VMEM((SSD_GROUPS, SSD_STATE, SSD_INNER // SSD_GROUPS), F32),
                        pltpu.VMEM((Q, SSD_INNER), F32)],
        compiler_params=_cparams("parallel", "arbitrary"),
    )(xbc, z, small, conv_w, conv_b.reshape(1, -1), dt_bias.reshape(1, H), dt_bias.reshape(H, 1),
      A.reshape(1, H), A.reshape(H, 1), d_e, norm_g.reshape(1, -1), expand)


def _prep_kernel(ckv_ref, kidx_ref, kvg_ref, ig_ref, ib_ref, ckvn_ref, ckvt_ref, ckvt2_ref, kn_ref):
    c = ckv_ref[...].astype(F32)
    cn = c * lax.rsqrt(jnp.mean(c * c, -1, keepdims=True) + LN_EPS) * kvg_ref[...]
    ckvn_ref[...] = cn.astype(BF16)
    cnt = cn.T.astype(BF16)
    ckvt2_ref[0] = cnt
    ckvt_ref[0] = cnt[:, :ATT_TK]
    ckvt_ref[1] = cnt[:, ATT_TK:]
    k = kidx_ref[...]
    lane = lax.broadcasted_iota(jnp.int32, k.shape, 1)
    live = lane < IDX_HEADDIM
    mu = jnp.sum(k, -1, keepdims=True) * (1.0 / IDX_HEADDIM)
    kc = jnp.where(live, k - mu, 0.0)
    var = jnp.sum(kc * kc, -1, keepdims=True) * (1.0 / IDX_HEADDIM)
    kn = jnp.where(live, kc * lax.rsqrt(var + LN_EPS) * ig_ref[...] + ib_ref[...], 0.0)
    kn_ref[...] = kn.astype(BF16)


def _dsa_prep(qc, small, kv_g, idx_g, idx_b):
    T = qc.shape[0]
    tm = 2 * ATT_TK
    pad = lambda v: jnp.pad(v.astype(F32), (0, LANES - IDX_HEADDIM)).reshape(1, LANES)
    return pl.pallas_call(
        _prep_kernel,
        out_shape=(jax.ShapeDtypeStruct((T, ATT_LATENT), BF16),
                   jax.ShapeDtypeStruct((T // ATT_TK, ATT_LATENT, ATT_TK), BF16),
                   jax.ShapeDtypeStruct((T // tm, ATT_LATENT, tm), BF16),
                   jax.ShapeDtypeStruct((T, LANES), BF16)),
        grid=(T // tm,),
        in_specs=[pl.BlockSpec((tm, ATT_LATENT), lambda i: (i, 0)),
                  pl.BlockSpec((tm, LANES), lambda i: (i, 4)),
                  _const_spec((1, ATT_LATENT)), _const_spec((1, LANES)), _const_spec((1, LANES))],
        out_specs=(pl.BlockSpec((tm, ATT_LATENT), lambda i: (i, 0)),
                   pl.BlockSpec((2, ATT_LATENT, ATT_TK), lambda i: (i, 0, 0)),
                   pl.BlockSpec((1, ATT_LATENT, tm), lambda i: (i, 0, 0)),
                   pl.BlockSpec((tm, LANES), lambda i: (i, 0))),
        compiler_params=_cparams("parallel"),
    )(qc, small, kv_g.reshape(1, -1), pad(idx_g), pad(idx_b))


def _attn_kernel(q_ref, qidx_ref, widx_ref, ckv_ref, ckvt_ref, ckvt2_ref, kidx_ref, brow_ref, wuk_ref, wuvt_ref,
                 o_ref, bias_ref, keys, khi, klo, qt, qit, qlat, m_scr, l_scr, acc, ot, *, topk):
    i = pl.program_id(1)
    QB, TK, NH, C, DH, DI = ATT_QB, ATT_TK, ATT_HEADS, ATT_LATENT, ATT_HEADDIM, IDX_HEADDIM
    n_tiles = i + 1

    @pl.when(i == 0)
    def _():
        period = brow_ref.shape[1]
        for h in range(NH):
            base = jnp.broadcast_to(brow_ref[h:h + 1, :], (2 * TK, period))
            bias_ref[h] = pltpu.roll(base, 0, 1, stride=1, stride_axis=0)[:, :QB]

    qt[...] = q_ref[...].astype(F32).T.astype(BF16)
    for h in range(NH):
        ql = _dot(wuk_ref[h], qt[h * DH:(h + 1) * DH, :])
        qlat[h] = (ql * (DH ** -0.5 * LOG2E)).astype(BF16)

    qit[0:IDX_HEADS * DI, :] = qidx_ref[...].T.astype(BF16)
    qit[IDX_HEADS * DI:, :] = jnp.zeros((LANES - DI, QB), BF16)
    wt = widx_ref[...].T[:IDX_HEADS, :] * (IDX_HEADS ** -0.5 * DI ** -0.5)
    qchunk = (i * QB + lax.broadcasted_iota(jnp.int32, (TK, QB), 1)) // CHUNK
    kin = lax.broadcasted_iota(jnp.int32, (TK, QB), 0)

    def score_tile(j, diagonal):
        kt = kidx_ref[pl.ds(pl.multiple_of(j * TK, TK), TK), :]
        s = jnp.zeros((TK, QB), F32)
        for h in range(IDX_HEADS):
            sh = _dot(kt, qit[h * DI:h * DI + LANES, :])
            s = s + wt[h:h + 1, :] * jnp.maximum(sh, 0.0)
        bits = pltpu.bitcast(s, jnp.int32)
        skey = bits ^ ((bits >> 31) & 0x7FFFFFFF)
        if diagonal:
            adm = ((j * TK + kin) // CHUNK) <= qchunk
            skey = jnp.where(adm, skey, INT_MIN)
        keys[j] = skey
        khi[j] = (skey >> 16).astype(jnp.int16)
        klo[j] = ((skey & 0xFFFF) - HALF16).astype(jnp.int16)

    def score_earlier(j, _):
        score_tile(j, False)
        return 0

    lax.fori_loop(0, i, score_earlier, 0)
    score_tile(i, True)

    PK = 2 * SUBLANES

    @pl.when(n_tiles % 2 == 1)
    def _():
        filler = jnp.full((TK, QB), -HALF16, jnp.int16)
        khi[n_tiles] = filler
        klo[n_tiles] = filler

    def count16(ref, test):
        def body(p, cnt):
            for j in (2 * p, 2 * p + 1):
                hit = jnp.where(test(ref[j]), jnp.int16(1), jnp.int16(0))
                for r in range(TK // PK):
                    cnt = cnt + hit[r * PK:(r + 1) * PK, :]
            return cnt
        cnt = lax.fori_loop(0, (n_tiles + 1) // 2, body, jnp.zeros((PK, QB), jnp.int16))
        return jnp.sum(cnt.astype(jnp.int32), axis=0, keepdims=True)

    def search16(ref, base_count, count_at_zero):
        def bit_step(b, carry):
            t, ct = carry
            cand = t + jnp.left_shift(jnp.int32(1), 15 - b)
            c16 = (cand - HALF16).astype(jnp.int16)
            c = base_count + count16(ref, lambda v: v >= c16)
            ok = c >= topk
            return jnp.where(ok, cand, t), jnp.where(ok, c, ct)
        return lax.fori_loop(0, 16, bit_step, (jnp.zeros((1, QB), jnp.int32), count_at_zero))

    zero = jnp.zeros((1, QB), jnp.int32)
    thi, n_ge_hi = search16(khi, zero, zero + n_tiles * TK)
    thi16 = (thi - HALF16).astype(jnp.int16)
    above = count16(khi, lambda v: v > thi16)

    def mark_equal(j, _):
        klo[j] = jnp.where(khi[j] == thi16, klo[j], jnp.int16(-HALF16))
        return 0

    lax.fori_loop(0, n_tiles, mark_equal, 0)
    tlo, n_ge = search16(klo, above, n_ge_hi)
    tau = jnp.left_shift(thi - HALF16, 16) | tlo
    tau = jnp.maximum(tau, INT_MIN + 1)

    def count32(test):
        def body(j, cnt):
            hit = jnp.where(test(keys[j]), 1, 0)
            return cnt + jnp.sum(hit.reshape(TK // SUBLANES, SUBLANES, QB), axis=0)
        cnt = lax.fori_loop(0, n_tiles, body, jnp.zeros((SUBLANES, QB), jnp.int32))
        return jnp.sum(cnt, axis=0, keepdims=True)

    @pl.when(jnp.max(n_ge) > topk)
    def _():
        need = (topk - count32(lambda v: v > tau)).astype(F32)
        r = lax.broadcasted_iota(jnp.int32, (TK, TK), 0)
        c = lax.broadcasted_iota(jnp.int32, (TK, TK), 1)
        tril = jnp.where(r >= c, 1.0, 0.0).astype(BF16)

        def demote(j, seen):
            kj = keys[j]
            tie = kj == tau
            rank = _dot(tril, jnp.where(tie, 1.0, 0.0).astype(BF16)) + seen
            keys[j] = jnp.where(tie & (rank > need), INT_MIN, kj)
            return rank[TK - 1:TK, :]

        lax.fori_loop(0, n_tiles, demote, jnp.zeros((1, QB), F32))

    m_scr[...] = jnp.full(m_scr.shape, NEG_BIG, F32)
    l_scr[...] = jnp.zeros(l_scr.shape, F32)
    acc[...] = jnp.zeros(acc.shape, F32)

    def attend(kv, kvts, key_tiles, bias_rows):
        maskb = jnp.where(jnp.concatenate(key_tiles, axis=0) >= tau, 0.0, NEG_BIG)
        for h in range(NH):
            s = _dot(kv, qlat[h]) + maskb
            if bias_rows is not None:
                s = s + bias_ref[h, bias_rows, :]
            m_old = m_scr[h]
            m_new = jnp.maximum(m_old, jnp.max(s, axis=0, keepdims=True))
            alpha = jnp.exp2(m_old - m_new)
            pr = jnp.exp2(s - m_new)
            l_scr[h] = alpha * l_scr[h] + jnp.sum(pr, axis=0, keepdims=True)
            pb = pr.astype(BF16)
            pv, row = None, 0
            for kvt in kvts:
                part = _dot(kvt, pb[row:row + kvt.shape[1], :])
                pv = part if pv is None else pv + part
                row += kvt.shape[1]
            acc[h] = alpha * acc[h] + pv
            m_scr[h] = m_new

    def kv_rows(first_tile, n):
        return ckv_ref[pl.ds(pl.multiple_of(first_tile * TK, TK), n * TK), :]

    def far_pair(jp, _):
        attend(kv_rows(2 * jp, 2), [ckvt2_ref[jp]], [keys[2 * jp], keys[2 * jp + 1]], None)
        return 0

    n_far = jnp.maximum(i - 1, 0)
    lax.fori_loop(0, n_far // 2, far_pair, 0)

    @pl.when(n_far % 2 == 1)
    def _():
        attend(kv_rows(n_far - 1, 1), [ckvt_ref[n_far - 1]], [keys[n_far - 1]], None)

    @pl.when(i > 0)
    def _():
        attend(kv_rows(i - 1, 2), [ckvt_ref[i - 1], ckvt_ref[i]], [keys[i - 1], keys[i]], slice(0, 2 * TK))

    @pl.when(i == 0)
    def _():
        attend(kv_rows(0, 1), [ckvt_ref[0]], [keys[0]], slice(TK, 2 * TK))

    for h in range(NH):
        ol = (acc[h] / l_scr[h]).astype(BF16)
        ot[h * DH:(h + 1) * DH, :] = _dot(wuvt_ref[h], ol)
    o_ref[...] = ot[...].T.astype(o_ref.dtype)


def _t5_bucket_np(rel):
    nb = REL_BUCKETS // 2
    max_exact = nb // 2
    n = np.abs(rel).astype(np.int64)
    nn = np.maximum(n, 1)
    sq = nn * nn
    log2_sq = np.floor(np.log2(sq.astype(np.float64))).astype(np.int64)
    log2_sq = np.where(2 ** (log2_sq + 1) <= sq, log2_sq + 1, log2_sq)
    log2_sq = np.where(2 ** log2_sq > sq, log2_sq - 1, log2_sq)
    large = np.minimum(max_exact + (log2_sq - 6), nb - 1)
    return np.where(rel > 0, nb, 0) + np.where(n < max_exact, n, large)


def _dsa_attention(qc, small, ckv_n, ckv_t, ckv_t2, kidx_n, w_uk, w_uv, rel_bias, B, L):
    T = B * L
    QB, TK, NH, C, DH = ATT_QB, ATT_TK, ATT_HEADS, ATT_LATENT, ATT_HEADDIM
    nq = L // QB
    topk = min(TOPK_MAX, L // 4)
    n_rel = 2 * TK + QB - 1
    period = n_rel + 1
    rel_of = np.arange(n_rel) - (TK + QB - 1)
    far_bucket = int(_t5_bucket_np(np.array([-(TK + 1)]))[0])
    table = (rel_bias.astype(F32) - rel_bias[far_bucket].astype(F32)[None, :]) * LOG2E
    v = jnp.take(table, jnp.asarray(_t5_bucket_np(rel_of)), axis=0).T
    u = jnp.pad(v, ((0, 0), (0, 1)))[:, (QB - 1 - np.arange(period)) % period]
    wuk = w_uk.astype(BF16)
    wuvt = jnp.transpose(w_uv, (0, 2, 1)).astype(BF16)

    return pl.pallas_call(
        functools.partial(_attn_kernel, topk=topk),
        out_shape=jax.ShapeDtypeStruct((T, NH * DH), BF16),
        grid=(B, nq),
        in_specs=[
            pl.BlockSpec((QB, NH * DH), lambda b, i: (b * nq + i, 0)),
            pl.BlockSpec((QB, IDX_HEADS * IDX_HEADDIM), lambda b, i: (b * nq + i, 0)),
            pl.BlockSpec((QB, LANES), lambda b, i: (b * nq + i, 5)),
            pl.BlockSpec((L, C), lambda b, i: (b, 0)),
            pl.BlockSpec((L // TK, C, TK), lambda b, i: (b, 0, 0)),
            pl.BlockSpec((L // (2 * TK), C, 2 * TK), lambda b, i: (b, 0, 0)),
            pl.BlockSpec((L, LANES), lambda b, i: (b, 0)),
            _const_spec((NH, period)),
            _const_spec((NH, C, DH)),
            _const_spec((NH, DH, C)),
        ],
        out_specs=pl.BlockSpec((QB, NH * DH), lambda b, i: (b * nq + i, 0)),
        scratch_shapes=[pltpu.VMEM((NH, 2 * TK, QB), F32),
                        pltpu.VMEM((L // TK, TK, QB), jnp.int32),
                        pltpu.VMEM((L // TK, TK, QB), jnp.int16),
                        pltpu.VMEM((L // TK, TK, QB), jnp.int16),
                        pltpu.VMEM((NH * DH, QB), BF16),
                        pltpu.VMEM((IDX_HEADS * IDX_HEADDIM + LANES - IDX_HEADDIM, QB), BF16),
                        pltpu.VMEM((NH, C, QB), BF16),
                        pltpu.VMEM((NH, 1, QB), F32),
                        pltpu.VMEM((NH, 1, QB), F32),
                        pltpu.VMEM((NH, C, QB), F32),
                        pltpu.VMEM((NH * DH, QB), F32)],
        compiler_params=_cparams("parallel", "arbitrary"),
    )(qc, small, small, ckv_n, ckv_t, ckv_t2, kidx_n, u, wuk, wuvt)


def _merge_kernel(ys_ref, ya_ref, gs_ref, ga_ref, h_ref, wbs_ref, wba_ref, wo_ref, g_ref, b_ref,
                  hf_ref, hb_ref):
    sig = lambda v: 1.0 / (1.0 + jnp.exp(-v.astype(F32)))
    m = sig(gs_ref[...]) * _dot(ys_ref[...], wbs_ref[...]) + sig(ga_ref[...]) * _dot(ya_ref[...], wba_ref[...])
    mix = _dot(m.astype(BF16), wo_ref[...])
    h1 = _ln_rows(DN_ALPHA * h_ref[...] + mix, g_ref[...], b_ref[...])
    hf_ref[...] = h1
    hb_ref[...] = h1.astype(BF16)


def _merge(y_ssd, y_att, g_ssd, g_att, h, w_br_ssd, w_br_att, w_out, g, b, tm=512):
    T, D = h.shape
    row = lambda w, j=0: pl.BlockSpec((tm, w), lambda i, j=j: (i, j))
    return pl.pallas_call(
        _merge_kernel,
        out_shape=(jax.ShapeDtypeStruct((T, D), F32), jax.ShapeDtypeStruct((T, D), BF16)),
        grid=(T // tm,),
        in_specs=[row(SSD_INNER), row(ATT_HEADS * ATT_HEADDIM), row(D), row(D), row(D),
                  _const_spec(w_br_ssd.shape), _const_spec(w_br_att.shape), _const_spec(w_out.shape),
                  _const_spec((1, D)), _const_spec((1, D))],
        out_specs=(row(D), row(D)),
        compiler_params=_cparams("parallel"),
    )(y_ssd, y_att, g_ssd, g_att, h, w_br_ssd, w_br_att, w_out, g.reshape(1, D), b.reshape(1, D))


FFN_COLS = 1408


def _ffn_kernel(hb_ref, h_ref, wup_ref, cw_ref, cb_ref, wd_ref, lg_ref, lb_ref, o_ref, upad, tail):
    c = pl.program_id(1)
    tm = hb_ref.shape[0]
    HALO = SUBLANES

    @pl.when(c == 0)
    def _():
        tail[...] = jnp.zeros(tail.shape, F32)

    hb = hb_ref[...]
    f = None
    for cc in range(D_FF // FFN_COLS):
        halves = []
        for half in range(2):
            cols = slice(half * D_FF + cc * FFN_COLS, half * D_FF + (cc + 1) * FFN_COLS)
            upad[0:HALO, :] = tail[:, cols]
            upad[HALO:HALO + tm, :] = _dot(hb, wup_ref[:, cols])
            tail[:, cols] = upad[tm:tm + HALO, :]
            u = cb_ref[:, cols]
            for k in range(FFN_CONV):
                u = u + cw_ref[k:k + 1, cols] * upad[pl.ds(HALO - (FFN_CONV - 1) + k, tm), :]
            halves.append(u)
        g = (_silu(halves[0]) * halves[1]).astype(BF16)
        part = _dot(g, wd_ref[cc * FFN_COLS:(cc + 1) * FFN_COLS, :])
        f = part if f is None else f + part
    o_ref[...] = _ln_rows(DN_ALPHA * h_ref[...] + f, lg_ref[...], lb_ref[...])


def _conv_ffn(h1b, h1, w_up, conv_w, conv_b, w_down, lg, lb, B, L, tm=512):
    T, D = h1.shape
    nt = L // tm
    tok = pl.BlockSpec((tm, D), lambda b, c: (b * nt + c, 0))
    return pl.pallas_call(
        _ffn_kernel,
        out_shape=jax.ShapeDtypeStruct((T, D), F32),
        grid=(B, nt),
        in_specs=[tok, tok, _const_spec(w_up.shape), _const_spec(conv_w.shape), _const_spec((1, 2 * D_FF)),
                  _const_spec(w_down.shape), _const_spec((1, D)), _const_spec((1, D))],
        out_specs=tok,
        scratch_shapes=[pltpu.VMEM((tm + SUBLANES, FFN_COLS), F32),
                        pltpu.VMEM((SUBLANES, 2 * D_FF), F32)],
        compiler_params=_cparams("parallel", "arbitrary"),
    )(h1b, h1, w_up, conv_w, conv_b.reshape(1, -1), w_down, lg.reshape(1, D), lb.reshape(1, D))


def _split_w_in(w_in):
    sizes = (SSD_INNER, SSD_XBC, SSD_HEADS, ATT_HEADS * ATT_HEADDIM, ATT_LATENT,
             IDX_HEADS * IDX_HEADDIM, IDX_HEADDIM, IDX_HEADS, D_MODEL, D_MODEL)
    pts = np.cumsum(sizes)[:-1].tolist()
    return jnp.split(w_in, pts, axis=1)


def _pad_cols(w, width):
    return jnp.pad(w, ((0, 0), (0, width - w.shape[1])))


def kernel(x, ln_in_g, ln_in_b, w_in, ssd_conv_w, ssd_conv_b, ssd_dt_bias, ssd_A_log, ssd_D, ssd_norm_g, att_kv_norm_g, att_w_uk, att_w_uv, idx_k_norm_g, idx_k_norm_b, rel_bias, w_br_ssd, w_br_att, w_out, ln1_g, ln1_b, ffn_w_up, ffn_conv_w, ffn_conv_b, ffn_w_down, ln2_g, ln2_b):
    B, L, D = x.shape
    T = B * L
    l = 0
    wz, wxbc, wdt, wq, wckv, wqi, wki, wwi, wgs, wga = _split_w_in(w_in[l])
    w_small = jnp.concatenate([wqi, _pad_cols(wki, LANES), _pad_cols(wwi, LANES), _pad_cols(wdt, LANES)],
                              axis=1)
    weights = [w.astype(BF16) for w in (wxbc, wz, wq, wckv, wgs, wga, w_small)]
    h, xbc, z, q, ckv, g_ssd, g_att, small = _ln_inproj(
        x.reshape(T, D), ln_in_g, ln_in_b, weights, [BF16] * 6 + [F32])

    y_ssd = _ssd_branch(xbc, z, small, ssd_conv_w[l], ssd_conv_b[l], ssd_dt_bias[l], ssd_A_log[l], ssd_D[l],
                        ssd_norm_g[l], B, L)
    ckv_n, ckv_t, ckv_t2, kidx_n = _dsa_prep(ckv, small, att_kv_norm_g[l], idx_k_norm_g[l], idx_k_norm_b[l])
    y_att = _dsa_attention(q, small, ckv_n, ckv_t, ckv_t2, kidx_n, att_w_uk[l], att_w_uv[l], rel_bias, B, L)

    h1, h1b = _merge(y_ssd, y_att, g_ssd, g_att, h, w_br_ssd[l].astype(BF16), w_br_att[l].astype(BF16),
                     w_out[l].astype(BF16), ln1_g[l], ln1_b[l])
    out = _conv_ffn(h1b, h1, ffn_w_up[l].astype(BF16), ffn_conv_w[l], ffn_conv_b[l],
                    ffn_w_down[l].astype(BF16), ln2_g[l], ln2_b[l], B, L)
    return out.reshape(B, L, D).astype(x.dtype)
```

```python
import functools
import math

import numpy as np
import jax
import jax.numpy as jnp
from jax import lax
from jax.experimental import pallas as pl
from jax.experimental.pallas import tpu as pltpu

F32 = jnp.float32
BF16 = jnp.bfloat16

D_MODEL = 1024
CHUNK = 64
SSD_INNER = 2048
SSD_HEADDIM = 64
SSD_HEADS = 32
SSD_GROUPS = 4
SSD_STATE = 128
SSD_CONV = 4
SSD_XBC = SSD_INNER + 2 * SSD_GROUPS * SSD_STATE
ATT_HEADS = 16
ATT_HEADDIM = 64
ATT_LATENT = 256
IDX_HEADS = 8
IDX_HEADDIM = 64
TOPK_MAX = 256
REL_BUCKETS = 32
REL_MAX_DIST = 128
D_FF = 2816
FFN_CONV = 3
DN_ALPHA = 2.0 ** 0.25
LN_EPS = 1e-5

LANES = 128
SUBLANES = 8
VMEM_LIMIT = 56 * 1024 * 1024

SSD_Q = 128
ATT_QB = 256
ATT_TK = 256
INT_MIN = -2 ** 31
HALF16 = 2 ** 15
NEG_BIG = -1e30
LOG2E = math.log2(math.e)


def _cparams(*sem):
    return pltpu.CompilerParams(dimension_semantics=sem, vmem_limit_bytes=VMEM_LIMIT)


def _const_spec(shape):
    nd = len(shape)
    return pl.BlockSpec(shape, lambda *_: (0,) * nd, pipeline_mode=pl.Buffered(1))


def _silu(x):
    return x / (1.0 + jnp.exp(-x))


def _softplus(x):
    return jnp.maximum(x, 0.0) + jnp.log1p(jnp.exp(-jnp.abs(x)))


def _ln_rows(x, g, b):
    mu = jnp.mean(x, -1, keepdims=True)
    xc = x - mu
    var = jnp.mean(xc * xc, -1, keepdims=True)
    return xc * lax.rsqrt(var + LN_EPS) * g + b


def _dot(a, b, **kw):
    return jnp.dot(a, b, preferred_element_type=F32, **kw)


def _rows_back(win, d, halo, n):
    shifted = win if d == 0 else pltpu.roll(win, d, 0)
    return shifted[halo:halo + n, :]


def _dot_exact01(m01, x, x_on_left):
    total = None
    r = x
    for _ in range(3):
        part = r.astype(BF16)
        r = r - part.astype(F32)
        t = _dot(part, m01) if x_on_left else _dot(m01, part)
        total = t if total is None else total + t
    return total


def _dot_nt(a, b):
    return lax.dot_general(a, b, (((1,), (1,)), ((), ())), preferred_element_type=F32)


def _dot_tn(a, b):
    return lax.dot_general(a, b, (((0,), (0,)), ((), ())), preferred_element_type=F32)


def _inproj_kernel(x_ref, g_ref, b_ref, *refs):
    n = (len(refs) - 1) // 2
    w_refs, h_ref, o_refs = refs[:n], refs[n], refs[n + 1:]
    h = _ln_rows(x_ref[...], g_ref[...], b_ref[...])
    h_ref[...] = h
    hb = h.astype(BF16)
    for w_ref, o_ref in zip(w_refs, o_refs):
        o_ref[...] = _dot(hb, w_ref[...]).astype(o_ref.dtype)


def _ln_inproj(x2, g, b, weights, out_dtypes, tm=512):
    T, D = x2.shape
    row = lambda w: pl.BlockSpec((tm, w), lambda i: (i, 0))
    widths = [w.shape[1] for w in weights]
    return pl.pallas_call(
        _inproj_kernel,
        out_shape=(jax.ShapeDtypeStruct((T, D), F32),
                   *[jax.ShapeDtypeStruct((T, n), dt) for n, dt in zip(widths, out_dtypes)]),
        grid=(T // tm,),
        in_specs=[row(D), _const_spec((1, D)), _const_spec((1, D)), *[_const_spec(w.shape) for w in weights]],
        out_specs=(row(D), *[row(n) for n in widths]),
        compiler_params=_cparams("parallel"),
    )(x2, g.reshape(1, D), b.reshape(1, D), *weights)


def _ssd_kernel(xbc_ref, z_ref, dt_ref, convw_ref, convb_ref, dtb_ref, dtbT_ref,
                a_ref, aT_ref, dskip_ref, ng_ref, e_ref, y_ref, xpad, state, ybuf):
    c = pl.program_id(1)
    Q = xbc_ref.shape[0]
    G, N, P = SSD_GROUPS, SSD_STATE, SSD_HEADDIM
    GW = SSD_INNER // G
    HALO = SUBLANES

    @pl.when(c == 0)
    def _():
        xpad[0:HALO, :] = jnp.zeros((HALO, SSD_XBC), F32)
        state[...] = jnp.zeros(state.shape, F32)

    @pl.when(c > 0)
    def _():
        xpad[0:HALO, :] = xpad[Q:Q + HALO, :]

    xpad[HALO:HALO + Q, :] = xbc_ref[...].astype(F32)
    conv = convb_ref[...]
    xwin = xpad[0:HALO + Q, :]
    for k in range(SSD_CONV):
        conv = conv + convw_ref[k:k + 1, :] * _rows_back(xwin, SSD_CONV - 1 - k, HALO, Q)
    xc = _silu(conv)
    xs = xc[:, :SSD_INNER]

    dt_blk = dt_ref[...]
    dt = _softplus(dt_blk[:, :SSD_HEADS] + dtb_ref[...])
    a = dt * a_ref[...]
    dtT = _softplus(dt_blk.T[:SSD_HEADS, :] + dtbT_ref[...])
    aT = dtT * aT_ref[...]

    row = lax.broadcasted_iota(jnp.int32, (Q, Q), 0)
    col = lax.broadcasted_iota(jnp.int32, (Q, Q), 1)
    causal = row >= col
    tril = jnp.where(causal, 1.0, 0.0).astype(BF16)
    triu = jnp.where(row <= col, 1.0, 0.0).astype(BF16)
    acum = _dot_exact01(tril, a, False)
    acumT = _dot_exact01(triu, aT, True)

    e = e_ref[...]
    dt_e = _dot_exact01(e, dt, True)
    eac_e = _dot_exact01(e, jnp.exp(acum), True)
    ds_e = _dot_exact01(e, jnp.exp(acum[Q - 1:Q, :] - acum), True)
    X = xs * dt_e
    Xb = X.astype(BF16)
    Xd = (X * ds_e).astype(BF16)
    lane = lax.broadcasted_iota(jnp.int32, (Q, LANES), 1)
    lo_half = lane < P

    for g in range(G):
        Bg = xc[:, SSD_INNER + g * N:SSD_INNER + (g + 1) * N].astype(BF16)
        Cg = xc[:, SSD_INNER + G * N + g * N:SSD_INNER + G * N + (g + 1) * N].astype(BF16)
        cb = _dot_nt(Cg, Bg)
        st = state[g]
        eg = eac_e[:, g * GW:(g + 1) * GW]
        yoff = _dot(Cg, st.astype(BF16)) * eg
        new = _dot_tn(Bg, Xd[:, g * GW:(g + 1) * GW])
        state[g] = st * eg[Q - 1:Q, :] + new
        for jj in range(GW // LANES):
            p = g * (GW // LANES) + jj
            h0 = 2 * p
            Xp = Xb[:, p * LANES:(p + 1) * LANES]
            zero = jnp.zeros_like(Xp)
            top = jnp.where(lo_half, Xp, zero)
            bot = jnp.where(lo_half, zero, Xp)
            yd = None
            for hh, rhs in ((h0, top), (h0 + 1, bot)):
                seg = acum[:, hh:hh + 1] - acumT[hh:hh + 1, :]
                decay = jnp.exp(jnp.where(causal, seg, -jnp.inf))
                part = _dot((cb * decay).astype(BF16), rhs)
                yd = part if yd is None else yd + part
            ybuf[:, p * LANES:(p + 1) * LANES] = yd + yoff[:, jj * LANES:(jj + 1) * LANES]

    y = ybuf[...] + xs * dskip_ref[...]
    for g in range(G):
        zg = z_ref[:, g * GW:(g + 1) * GW]
        v = y[:, g * GW:(g + 1) * GW] * _silu(zg.astype(F32))
        v = v * lax.rsqrt(jnp.mean(v * v, -1, keepdims=True) + LN_EPS)
        y_ref[:, g * GW:(g + 1) * GW] = (v * ng_ref[:, g * GW:(g + 1) * GW]).astype(BF16)


def _ssd_branch(xbc, z, small, conv_w, conv_b, dt_bias, A_log, D_skip, norm_g, B, L):
    T = B * L
    Q = SSD_Q
    nc = L // Q
    H = SSD_HEADS
    A = -jnp.exp(A_log.astype(F32))
    expand = jnp.asarray(np.kron(np.eye(H, dtype=np.float32), np.ones((1, SSD_HEADDIM), np.float32)), BF16)
    d_e = jnp.repeat(D_skip.astype(F32), SSD_HEADDIM).reshape(1, SSD_INNER)
    tok = lambda w, j: pl.BlockSpec((Q, w), lambda b, c, j=j: (b * nc + c, j))
    return pl.pallas_call(
        _ssd_kernel,
        out_shape=jax.ShapeDtypeStruct((T, SSD_INNER), BF16),
        grid=(B, nc),
        in_specs=[tok(SSD_XBC, 0), tok(SSD_INNER, 0), tok(LANES, 6),
                  _const_spec((SSD_CONV, SSD_XBC)), _const_spec((1, SSD_XBC)),
                  _const_spec((1, H)), _const_spec((H, 1)), _const_spec((1, H)), _const_spec((H, 1)),
                  _const_spec((1, SSD_INNER)), _const_spec((1, SSD_INNER)), _const_spec((H, SSD_INNER))],
        out_specs=pl.BlockSpec((Q, SSD_INNER), lambda b, c: (b * nc + c, 0)),
        scratch_shapes=[pltpu.VMEM((Q + 2 * SUBLANES, SSD_XBC), F32),
                        pltpu.VMEM((SSD_GROUPS, SSD_STATE, SSD_INNER // SSD_GROUPS), F32),
                        pltpu.VMEM((Q, SSD_INNER), F32)],
        compiler_params=_cparams("parallel", "arbitrary"),
    )(xbc, z, small, conv_w, conv_b.reshape(1, -1), dt_bias.reshape(1, H), dt_bias.reshape(H, 1),
      A.reshape(1, H), A.reshape(H, 1), d_e, norm_g.reshape(1, -1), expand)


def _prep_kernel(ckv_ref, kidx_ref, kvg_ref, ig_ref, ib_ref, ckvn_ref, ckvt_ref, ckvt2_ref, kn_ref):
    c = ckv_ref[...].astype(F32)
    cn = c * lax.rsqrt(jnp.mean(c * c, -1, keepdims=True) + LN_EPS) * kvg_ref[...]
    ckvn_ref[...] = cn.astype(BF16)
    cnt = cn.T.astype(BF16)
    ckvt2_ref[0] = cnt
    ckvt_ref[0] = cnt[:, :ATT_TK]
    ckvt_ref[1] = cnt[:, ATT_TK:]
    k = kidx_ref[...]
    lane = lax.broadcasted_iota(jnp.int32, k.shape, 1)
    live = lane < IDX_HEADDIM
    mu = jnp.sum(k, -1, keepdims=True) * (1.0 / IDX_HEADDIM)
    kc = jnp.where(live, k - mu, 0.0)
    var = jnp.sum(kc * kc, -1, keepdims=True) * (1.0 / IDX_HEADDIM)
    kn = jnp.where(live, kc * lax.rsqrt(var + LN_EPS) * ig_ref[...] + ib_ref[...], 0.0)
    kn_ref[...] = kn.astype(BF16)


def _dsa_prep(qc, small, kv_g, idx_g, idx_b):
    T = qc.shape[0]
    tm = 2 * ATT_TK
    pad = lambda v: jnp.pad(v.astype(F32), (0, LANES - IDX_HEADDIM)).reshape(1, LANES)
    return pl.pallas_call(
        _prep_kernel,
        out_shape=(jax.ShapeDtypeStruct((T, ATT_LATENT), BF16),
                   jax.ShapeDtypeStruct((T // ATT_TK, ATT_LATENT, ATT_TK), BF16),
                   jax.ShapeDtypeStruct((T // tm, ATT_LATENT, tm), BF16),
                   jax.ShapeDtypeStruct((T, LANES), BF16)),
        grid=(T // tm,),
        in_specs=[pl.BlockSpec((tm, ATT_LATENT), lambda i: (i, 0)),
                  pl.BlockSpec((tm, LANES), lambda i: (i, 4)),
                  _const_spec((1, ATT_LATENT)), _const_spec((1, LANES)), _const_spec((1, LANES))],
        out_specs=(pl.BlockSpec((tm, ATT_LATENT), lambda i: (i, 0)),
                   pl.BlockSpec((2, ATT_LATENT, ATT_TK), lambda i: (i, 0, 0)),
                   pl.BlockSpec((1, ATT_LATENT, tm), lambda i: (i, 0, 0)),
                   pl.BlockSpec((tm, LANES), lambda i: (i, 0))),
        compiler_params=_cparams("parallel"),
    )(qc, small, kv_g.reshape(1, -1), pad(idx_g), pad(idx_b))


def _attn_kernel(q_ref, qidx_ref, widx_ref, ckv_ref, ckvt_ref, ckvt2_ref, kidx_ref, brow_ref, wuk_ref, wuvt_ref,
                 o_ref, bias_ref, keys, khi, klo, qt, qit, qlat, m_scr, l_scr, acc, ot, *, topk):
    i = pl.program_id(1)
    QB, TK, NH, C, DH, DI = ATT_QB, ATT_TK, ATT_HEADS, ATT_LATENT, ATT_HEADDIM, IDX_HEADDIM
    n_tiles = i + 1

    @pl.when(i == 0)
    def _():
        period = brow_ref.shape[1]
        for h in range(NH):
            base = jnp.broadcast_to(brow_ref[h:h + 1, :], (2 * TK, period))
            bias_ref[h] = pltpu.roll(base, 0, 1, stride=1, stride_axis=0)[:, :QB]

    qt[...] = q_ref[...].astype(F32).T.astype(BF16)
    for h in range(NH):
        ql = _dot(wuk_ref[h], qt[h * DH:(h + 1) * DH, :])
        qlat[h] = (ql * (DH ** -0.5 * LOG2E)).astype(BF16)

    qit[0:IDX_HEADS * DI, :] = qidx_ref[...].T.astype(BF16)
    qit[IDX_HEADS * DI:, :] = jnp.zeros((LANES - DI, QB), BF16)
    wt = widx_ref[...].T[:IDX_HEADS, :] * (IDX_HEADS ** -0.5 * DI ** -0.5)
    qchunk = (i * QB + lax.broadcasted_iota(jnp.int32, (TK, QB), 1)) // CHUNK
    kin = lax.broadcasted_iota(jnp.int32, (TK, QB), 0)

    def score_tile(j, diagonal):
        kt = kidx_ref[pl.ds(pl.multiple_of(j * TK, TK), TK), :]
        s = jnp.zeros((TK, QB), F32)
        for h in range(IDX_HEADS):
            sh = _dot(kt, qit[h * DI:h * DI + LANES, :])
            s = s + wt[h:h + 1, :] * jnp.maximum(sh, 0.0)
        bits = pltpu.bitcast(s, jnp.int32)
        skey = bits ^ ((bits >> 31) & 0x7FFFFFFF)
        if diagonal:
            adm = ((j * TK + kin) // CHUNK) <= qchunk
            skey = jnp.where(adm, skey, INT_MIN)
        keys[j] = skey
        khi[j] = (skey >> 16).astype(jnp.int16)
        klo[j] = ((skey & 0xFFFF) - HALF16).astype(jnp.int16)

    def score_earlier(j, _):
        score_tile(j, False)
        return 0

    lax.fori_loop(0, i, score_earlier, 0)
    score_tile(i, True)

    PK = 2 * SUBLANES

    @pl.when(n_tiles % 2 == 1)
    def _():
        filler = jnp.full((TK, QB), -HALF16, jnp.int16)
        khi[n_tiles] = filler
        klo[n_tiles] = filler

    def count16(ref, test):
        def body(p, cnt):
            for j in (2 * p, 2 * p + 1):
                hit = jnp.where(test(ref[j]), jnp.int16(1), jnp.int16(0))
                for r in range(TK // PK):
                    cnt = cnt + hit[r * PK:(r + 1) * PK, :]
            return cnt
        cnt = lax.fori_loop(0, (n_tiles + 1) // 2, body, jnp.zeros((PK, QB), jnp.int16))
        return jnp.sum(cnt.astype(jnp.int32), axis=0, keepdims=True)

    def search16(ref, base_count, count_at_zero):
        def bit_step(b, carry):
            t, ct = carry
            cand = t + jnp.left_shift(jnp.int32(1), 15 - b)
            c16 = (cand - HALF16).astype(jnp.int16)
            c = base_count + count16(ref, lambda v: v >= c16)
            ok = c >= topk
            return jnp.where(ok, cand, t), jnp.where(ok, c, ct)
        return lax.fori_loop(0, 16, bit_step, (jnp.zeros((1, QB), jnp.int32), count_at_zero))

    zero = jnp.zeros((1, QB), jnp.int32)
    thi, n_ge_hi = search16(khi, zero, zero + n_tiles * TK)
    thi16 = (thi - HALF16).astype(jnp.int16)
    above = count16(khi, lambda v: v > thi16)

    def mark_equal(j, _):
        klo[j] = jnp.where(khi[j] == thi16, klo[j], jnp.int16(-HALF16))
        return 0

    lax.fori_loop(0, n_tiles, mark_equal, 0)
    tlo, n_ge = search16(klo, above, n_ge_hi)
    tau = jnp.left_shift(thi - HALF16, 16) | tlo
    tau = jnp.maximum(tau, INT_MIN + 1)

    def count32(test):
        def body(j, cnt):
            hit = jnp.where(test(keys[j]), 1, 0)
            return cnt + jnp.sum(hit.reshape(TK // SUBLANES, SUBLANES, QB), axis=0)
        cnt = lax.fori_loop(0, n_tiles, body, jnp.zeros((SUBLANES, QB), jnp.int32))
        return jnp.sum(cnt, axis=0, keepdims=True)

    surplus = jnp.max(n_ge) > topk
    m_scr[...] = jnp.full(m_scr.shape, NEG_BIG, F32)
    l_scr[...] = jnp.zeros(l_scr.shape, F32)
    acc[...] = jnp.zeros(acc.shape, F32)

    @pl.when(surplus)
    def _():
        need = (topk - count32(lambda v: v > tau)).astype(F32)
        r = lax.broadcasted_iota(jnp.int32, (TK, TK), 0)
        c = lax.broadcasted_iota(jnp.int32, (TK, TK), 1)
        tril = jnp.where(r >= c, 1.0, 0.0).astype(BF16)

        def demote(j, seen):
            kj = keys[j]
            tie = kj == tau
            rank = _dot(tril, jnp.where(tie, 1.0, 0.0).astype(BF16)) + seen
            keys[j] = jnp.where(tie & (rank > need), INT_MIN, kj)
            return rank[TK - 1:TK, :]

        lax.fori_loop(0, n_tiles, demote, jnp.zeros((1, QB), F32))

    def attend(kv, kvts, key_tiles, bias_rows):
        maskb = jnp.where(jnp.concatenate(key_tiles, axis=0) >= tau, 0.0, NEG_BIG)
        for h in range(NH):
            s = _dot(kv, qlat[h]) + maskb
            if bias_rows is not None:
                s = s + bias_ref[h, bias_rows, :]
            m_old = m_scr[h]
            m_new = jnp.maximum(m_old, jnp.max(s, axis=0, keepdims=True))
            alpha = jnp.exp2(m_old - m_new)
            pr = jnp.exp2(s - m_new)
            l_scr[h] = alpha * l_scr[h] + jnp.sum(pr, axis=0, keepdims=True)
            pb = pr.astype(BF16)
            pv, row = None, 0
            for kvt in kvts:
                part = _dot(kvt, pb[row:row + kvt.shape[1], :])
                pv = part if pv is None else pv + part
                row += kvt.shape[1]
            acc[h] = alpha * acc[h] + pv
            m_scr[h] = m_new

    def kv_rows(first_tile, n):
        return ckv_ref[pl.ds(pl.multiple_of(first_tile * TK, TK), n * TK), :]

    def far_pair(jp, _):
        attend(kv_rows(2 * jp, 2), [ckvt2_ref[jp]], [keys[2 * jp], keys[2 * jp + 1]], None)
        return 0

    n_far = jnp.maximum(i - 1, 0)
    lax.fori_loop(0, n_far // 2, far_pair, 0)

    @pl.when(n_far % 2 == 1)
    def _():
        attend(kv_rows(n_far - 1, 1), [ckvt_ref[n_far - 1]], [keys[n_far - 1]], None)

    @pl.when(i > 0)
    def _():
        attend(kv_rows(i - 1, 2), [ckvt_ref[i - 1], ckvt_ref[i]], [keys[i - 1], keys[i]], slice(0, 2 * TK))

    @pl.when(i == 0)
    def _():
        attend(kv_rows(0, 1), [ckvt_ref[0]], [keys[0]], slice(TK, 2 * TK))

    for h in range(NH):
        ol = (acc[h] / l_scr[h]).astype(BF16)
        ot[h * DH:(h + 1) * DH, :] = _dot(wuvt_ref[h], ol)
    o_ref[...] = ot[...].T.astype(o_ref.dtype)


def _t5_bucket_np(rel):
    nb = REL_BUCKETS // 2
    max_exact = nb // 2
    n = np.abs(rel).astype(np.int64)
    nn = np.maximum(n, 1)
    sq = nn * nn
    log2_sq = np.floor(np.log2(sq.astype(np.float64))).astype(np.int64)
    log2_sq = np.where(2 ** (log2_sq + 1) <= sq, log2_sq + 1, log2_sq)
    log2_sq = np.where(2 ** log2_sq > sq, log2_sq - 1, log2_sq)
    large = np.minimum(max_exact + (log2_sq - 6), nb - 1)
    return np.where(rel > 0, nb, 0) + np.where(n < max_exact, n, large)


def _dsa_attention(qc, small, ckv_n, ckv_t, ckv_t2, kidx_n, w_uk, w_uv, rel_bias, B, L):
    T = B * L
    QB, TK, NH, C, DH = ATT_QB, ATT_TK, ATT_HEADS, ATT_LATENT, ATT_HEADDIM
    nq = L // QB
    topk = min(TOPK_MAX, L // 4)
    n_rel = 2 * TK + QB - 1
    period = n_rel + 1
    rel_of = np.arange(n_rel) - (TK + QB - 1)
    far_bucket = int(_t5_bucket_np(np.array([-(TK + 1)]))[0])
    table = (rel_bias.astype(F32) - rel_bias[far_bucket].astype(F32)[None, :]) * LOG2E
    v = jnp.take(table, jnp.asarray(_t5_bucket_np(rel_of)), axis=0).T
    u = jnp.pad(v, ((0, 0), (0, 1)))[:, (QB - 1 - np.arange(period)) % period]
    wuk = w_uk.astype(BF16)
    wuvt = jnp.transpose(w_uv, (0, 2, 1)).astype(BF16)

    return pl.pallas_call(
        functools.partial(_attn_kernel, topk=topk),
        out_shape=jax.ShapeDtypeStruct((T, NH * DH), BF16),
        grid=(B, nq),
        in_specs=[
            pl.BlockSpec((QB, NH * DH), lambda b, i: (b * nq + i, 0)),
            pl.BlockSpec((QB, IDX_HEADS * IDX_HEADDIM), lambda b, i: (b * nq + i, 0)),
            pl.BlockSpec((QB, LANES), lambda b, i: (b * nq + i, 5)),
            pl.BlockSpec((L, C), lambda b, i: (b, 0)),
            pl.BlockSpec((L // TK, C, TK), lambda b, i: (b, 0, 0)),
            pl.BlockSpec((L // (2 * TK), C, 2 * TK), lambda b, i: (b, 0, 0)),
            pl.BlockSpec((L, LANES), lambda b, i: (b, 0)),
            _const_spec((NH, period)),
            _const_spec((NH, C, DH)),
            _const_spec((NH, DH, C)),
        ],
        out_specs=pl.BlockSpec((QB, NH * DH), lambda b, i: (b * nq + i, 0)),
        scratch_shapes=[pltpu.VMEM((NH, 2 * TK, QB), F32),
                        pltpu.VMEM((L // TK, TK, QB), jnp.int32),
                        pltpu.VMEM((L // TK, TK, QB), jnp.int16),
                        pltpu.VMEM((L // TK, TK, QB), jnp.int16),
                        pltpu.VMEM((NH * DH, QB), BF16),
                        pltpu.VMEM((IDX_HEADS * IDX_HEADDIM + LANES - IDX_HEADDIM, QB), BF16),
                        pltpu.VMEM((NH, C, QB), BF16),
                        pltpu.VMEM((NH, 1, QB), F32),
                        pltpu.VMEM((NH, 1, QB), F32),
                        pltpu.VMEM((NH, C, QB), F32),
                        pltpu.VMEM((NH * DH, QB), F32)],
        compiler_params=_cparams("parallel", "arbitrary"),
    )(qc, small, small, ckv_n, ckv_t, ckv_t2, kidx_n, u, wuk, wuvt)


def _merge_kernel(ys_ref, ya_ref, gs_ref, ga_ref, h_ref, wbs_ref, wba_ref, wo_ref, g_ref, b_ref,
                  hf_ref, hb_ref):
    sig = lambda v: 1.0 / (1.0 + jnp.exp(-v.astype(F32)))
    m = sig(gs_ref[...]) * _dot(ys_ref[...], wbs_ref[...]) + sig(ga_ref[...]) * _dot(ya_ref[...], wba_ref[...])
    mix = _dot(m.astype(BF16), wo_ref[...])
    h1 = _ln_rows(DN_ALPHA * h_ref[...] + mix, g_ref[...], b_ref[...])
    hf_ref[...] = h1
    hb_ref[...] = h1.astype(BF16)


def _merge(y_ssd, y_att, g_ssd, g_att, h, w_br_ssd, w_br_att, w_out, g, b, tm=512):
    T, D = h.shape
    row = lambda w, j=0: pl.BlockSpec((tm, w), lambda i, j=j: (i, j))
    return pl.pallas_call(
        _merge_kernel,
        out_shape=(jax.ShapeDtypeStruct((T, D), F32), jax.ShapeDtypeStruct((T, D), BF16)),
        grid=(T // tm,),
        in_specs=[row(SSD_INNER), row(ATT_HEADS * ATT_HEADDIM), row(D), row(D), row(D),
                  _const_spec(w_br_ssd.shape), _const_spec(w_br_att.shape), _const_spec(w_out.shape),
                  _const_spec((1, D)), _const_spec((1, D))],
        out_specs=(row(D), row(D)),
        compiler_params=_cparams("parallel"),
    )(y_ssd, y_att, g_ssd, g_att, h, w_br_ssd, w_br_att, w_out, g.reshape(1, D), b.reshape(1, D))


FFN_COLS = 1408


def _ffn_kernel(hb_ref, h_ref, wup_ref, cw_ref, cb_ref, wd_ref, lg_ref, lb_ref, o_ref, upad, tail):
    c = pl.program_id(1)
    tm = hb_ref.shape[0]
    HALO = SUBLANES

    @pl.when(c == 0)
    def _():
        tail[...] = jnp.zeros(tail.shape, F32)

    hb = hb_ref[...]
    f = None
    for cc in range(D_FF // FFN_COLS):
        halves = []
        for half in range(2):
            cols = slice(half * D_FF + cc * FFN_COLS, half * D_FF + (cc + 1) * FFN_COLS)
            upad[0:HALO, :] = tail[:, cols]
            upad[HALO:HALO + tm, :] = _dot(hb, wup_ref[:, cols])
            tail[:, cols] = upad[tm:tm + HALO, :]
            u = cb_ref[:, cols]
            for k in range(FFN_CONV):
                u = u + cw_ref[k:k + 1, cols] * upad[pl.ds(HALO - (FFN_CONV - 1) + k, tm), :]
            halves.append(u)
        g = (_silu(halves[0]) * halves[1]).astype(BF16)
        part = _dot(g, wd_ref[cc * FFN_COLS:(cc + 1) * FFN_COLS, :])
        f = part if f is None else f + part
    o_ref[...] = _ln_rows(DN_ALPHA * h_ref[...] + f, lg_ref[...], lb_ref[...])


def _conv_ffn(h1b, h1, w_up, conv_w, conv_b, w_down, lg, lb, B, L, tm=512):
    T, D = h1.shape
    nt = L // tm
    tok = pl.BlockSpec((tm, D), lambda b, c: (b * nt + c, 0))
    return pl.pallas_call(
        _ffn_kernel,
        out_shape=jax.ShapeDtypeStruct((T, D), F32),
        grid=(B, nt),
        in_specs=[tok, tok, _const_spec(w_up.shape), _const_spec(conv_w.shape), _const_spec((1, 2 * D_FF)),
                  _const_spec(w_down.shape), _const_spec((1, D)), _const_spec((1, D))],
        out_specs=tok,
        scratch_shapes=[pltpu.VMEM((tm + SUBLANES, FFN_COLS), F32),
                        pltpu.VMEM((SUBLANES, 2 * D_FF), F32)],
        compiler_params=_cparams("parallel", "arbitrary"),
    )(h1b, h1, w_up, conv_w, conv_b.reshape(1, -1), w_down, lg.reshape(1, D), lb.reshape(1, D))


def _split_w_in(w_in):
    sizes = (SSD_INNER, SSD_XBC, SSD_HEADS, ATT_HEADS * ATT_HEADDIM, ATT_LATENT,
             IDX_HEADS * IDX_HEADDIM, IDX_HEADDIM, IDX_HEADS, D_MODEL, D_MODEL)
    pts = np.cumsum(sizes)[:-1].tolist()
    return jnp.split(w_in, pts, axis=1)


def _pad_cols(w, width):
    return jnp.pad(w, ((0, 0), (0, width - w.shape[1])))


def kernel(x, ln_in_g, ln_in_b, w_in, ssd_conv_w, ssd_conv_b, ssd_dt_bias, ssd_A_log, ssd_D, ssd_norm_g, att_kv_norm_g, att_w_uk, att_w_uv, idx_k_norm_g, idx_k_norm_b, rel_bias, w_br_ssd, w_br_att, w_out, ln1_g, ln1_b, ffn_w_up, ffn_conv_w, ffn_conv_b, ffn_w_down, ln2_g, ln2_b):
    B, L, D = x.shape
    T = B * L
    l = 0
    wz, wxbc, wdt, wq, wckv, wqi, wki, wwi, wgs, wga = _split_w_in(w_in[l])
    w_small = jnp.concatenate([wqi, _pad_cols(wki, LANES), _pad_cols(wwi, LANES), _pad_cols(wdt, LANES)],
                              axis=1)
    weights = [w.astype(BF16) for w in (wxbc, wz, wq, wckv, wgs, wga, w_small)]
    h, xbc, z, q, ckv, g_ssd, g_att, small = _ln_inproj(
        x.reshape(T, D), ln_in_g, ln_in_b, weights, [BF16] * 6 + [F32])

    y_ssd = _ssd_branch(xbc, z, small, ssd_conv_w[l], ssd_conv_b[l], ssd_dt_bias[l], ssd_A_log[l], ssd_D[l],
                        ssd_norm_g[l], B, L)
    ckv_n, ckv_t, ckv_t2, kidx_n = _dsa_prep(ckv, small, att_kv_norm_g[l], idx_k_norm_g[l], idx_k_norm_b[l])
    y_att = _dsa_attention(q, small, ckv_n, ckv_t, ckv_t2, kidx_n, att_w_uk[l], att_w_uv[l], rel_bias, B, L)

    h1, h1b = _merge(y_ssd, y_att, g_ssd, g_att, h, w_br_ssd[l].astype(BF16), w_br_att[l].astype(BF16),
                     w_out[l].astype(BF16), ln1_g[l], ln1_b[l])
    out = _conv_ffn(h1b, h1, ffn_w_up[l].astype(BF16), ffn_conv_w[l], ffn_conv_b[l],
                    ffn_w_down[l].astype(BF16), ln2_g[l], ln2_b[l], B, L)
    return out.reshape(B, L, D).astype(x.dtype)
```

```python
import functools
import math

import numpy as np
import jax
import jax.numpy as jnp
from jax import lax
from jax.experimental import pallas as pl
from jax.experimental.pallas import tpu as pltpu

F32 = jnp.float32
BF16 = jnp.bfloat16

D_MODEL = 1024
CHUNK = 64
SSD_INNER = 2048
SSD_HEADDIM = 64
SSD_HEADS = 32
SSD_GROUPS = 4
SSD_STATE = 128
SSD_CONV = 4
SSD_XBC = SSD_INNER + 2 * SSD_GROUPS * SSD_STATE
ATT_HEADS = 16
ATT_HEADDIM = 64
ATT_LATENT = 256
IDX_HEADS = 8
IDX_HEADDIM = 64
TOPK_MAX = 256
REL_BUCKETS = 32
REL_MAX_DIST = 128
D_FF = 2816
FFN_CONV = 3
DN_ALPHA = 2.0 ** 0.25
LN_EPS = 1e-5

LANES = 128
SUBLANES = 8
VMEM_LIMIT = 56 * 1024 * 1024

SSD_Q = 128
ATT_QB = 256
ATT_TK = 256
INT_MIN = -2 ** 31
HALF16 = 2 ** 15
SMALL_KIDX_BLOCK = IDX_HEADS * IDX_HEADDIM // LANES
SMALL_WIDX_BLOCK = SMALL_KIDX_BLOCK + 1
SMALL_DT_BLOCK = SMALL_KIDX_BLOCK + 2
NEG_BIG = -1e30
LOG2E = math.log2(math.e)


def _cparams(*sem):
    return pltpu.CompilerParams(dimension_semantics=sem, vmem_limit_bytes=VMEM_LIMIT)


def _const_spec(shape):
    nd = len(shape)
    return pl.BlockSpec(shape, lambda *_: (0,) * nd, pipeline_mode=pl.Buffered(1))


def _silu(x):
    return x / (1.0 + jnp.exp(-x))


def _softplus(x):
    return jnp.maximum(x, 0.0) + jnp.log1p(jnp.exp(-jnp.abs(x)))


def _ln_rows(x, g, b):
    mu = jnp.mean(x, -1, keepdims=True)
    xc = x - mu
    var = jnp.mean(xc * xc, -1, keepdims=True)
    return xc * lax.rsqrt(var + LN_EPS) * g + b


def _dot(a, b, **kw):
    return jnp.dot(a, b, preferred_element_type=F32, **kw)


def _rows_back(win, d, halo, n):
    shifted = win if d == 0 else pltpu.roll(win, d, 0)
    return shifted[halo:halo + n, :]


def _dot_exact01(m01, x, x_on_left):
    total = None
    r = x
    for _ in range(3):
        part = r.astype(BF16)
        r = r - part.astype(F32)
        t = _dot(part, m01) if x_on_left else _dot(m01, part)
        total = t if total is None else total + t
    return total


def _dot_nt(a, b):
    return lax.dot_general(a, b, (((1,), (1,)), ((), ())), preferred_element_type=F32)


def _dot_tn(a, b):
    return lax.dot_general(a, b, (((0,), (0,)), ((), ())), preferred_element_type=F32)


def _inproj_kernel(x_ref, g_ref, b_ref, *refs):
    n = (len(refs) - 1) // 2
    w_refs, h_ref, o_refs = refs[:n], refs[n], refs[n + 1:]
    h = _ln_rows(x_ref[...], g_ref[...], b_ref[...])
    h_ref[...] = h
    hb = h.astype(BF16)
    for w_ref, o_ref in zip(w_refs, o_refs):
        o_ref[...] = _dot(hb, w_ref[...]).astype(o_ref.dtype)


def _ln_inproj(x2, g, b, weights, out_dtypes, tm=512):
    T, D = x2.shape
    row = lambda w: pl.BlockSpec((tm, w), lambda i: (i, 0))
    widths = [w.shape[1] for w in weights]
    return pl.pallas_call(
        _inproj_kernel,
        out_shape=(jax.ShapeDtypeStruct((T, D), F32),
                   *[jax.ShapeDtypeStruct((T, n), dt) for n, dt in zip(widths, out_dtypes)]),
        grid=(T // tm,),
        in_specs=[row(D), _const_spec((1, D)), _const_spec((1, D)), *[_const_spec(w.shape) for w in weights]],
        out_specs=(row(D), *[row(n) for n in widths]),
        compiler_params=_cparams("parallel"),
    )(x2, g.reshape(1, D), b.reshape(1, D), *weights)


def _ssd_kernel(xbc_ref, z_ref, dt_ref, convw_ref, convb_ref, dtb_ref, dtbT_ref,
                a_ref, aT_ref, dskip_ref, ng_ref, e_ref, y_ref, xpad, state, ybuf):
    c = pl.program_id(1)
    Q = xbc_ref.shape[0]
    G, N, P = SSD_GROUPS, SSD_STATE, SSD_HEADDIM
    GW = SSD_INNER // G
    HALO = SUBLANES

    @pl.when(c == 0)
    def _():
        xpad[0:HALO, :] = jnp.zeros((HALO, SSD_XBC), F32)
        state[...] = jnp.zeros(state.shape, F32)

    @pl.when(c > 0)
    def _():
        xpad[0:HALO, :] = xpad[Q:Q + HALO, :]

    xpad[HALO:HALO + Q, :] = xbc_ref[...].astype(F32)
    conv = convb_ref[...]
    xwin = xpad[0:HALO + Q, :]
    for k in range(SSD_CONV):
        conv = conv + convw_ref[k:k + 1, :] * _rows_back(xwin, SSD_CONV - 1 - k, HALO, Q)
    xc = _silu(conv)
    xs = xc[:, :SSD_INNER]

    dt_blk = dt_ref[...]
    dt = _softplus(dt_blk[:, :SSD_HEADS] + dtb_ref[...])
    a = dt * a_ref[...]
    dtT = _softplus(dt_blk.T[:SSD_HEADS, :] + dtbT_ref[...])
    aT = dtT * aT_ref[...]

    row = lax.broadcasted_iota(jnp.int32, (Q, Q), 0)
    col = lax.broadcasted_iota(jnp.int32, (Q, Q), 1)
    causal = row >= col
    tril = jnp.where(causal, 1.0, 0.0).astype(BF16)
    triu = jnp.where(row <= col, 1.0, 0.0).astype(BF16)
    acum = _dot_exact01(tril, a, False)
    acumT = _dot_exact01(triu, aT, True)

    e = e_ref[...]
    dt_e = _dot_exact01(e, dt, True)
    eac_e = _dot_exact01(e, jnp.exp(acum), True)
    ds_e = _dot_exact01(e, jnp.exp(acum[Q - 1:Q, :] - acum), True)
    X = xs * dt_e
    Xb = X.astype(BF16)
    Xd = (X * ds_e).astype(BF16)
    lane = lax.broadcasted_iota(jnp.int32, (Q, LANES), 1)
    lo_half = lane < P

    for g in range(G):
        Bg = xc[:, SSD_INNER + g * N:SSD_INNER + (g + 1) * N].astype(BF16)
        Cg = xc[:, SSD_INNER + G * N + g * N:SSD_INNER + G * N + (g + 1) * N].astype(BF16)
        cb = _dot_nt(Cg, Bg)
        st = state[g]
        eg = eac_e[:, g * GW:(g + 1) * GW]
        yoff = _dot(Cg, st.astype(BF16)) * eg
        new = _dot_tn(Bg, Xd[:, g * GW:(g + 1) * GW])
        state[g] = st * eg[Q - 1:Q, :] + new
        for jj in range(GW // LANES):
            p = g * (GW // LANES) + jj
            h0 = 2 * p
            Xp = Xb[:, p * LANES:(p + 1) * LANES]
            zero = jnp.zeros_like(Xp)
            top = jnp.where(lo_half, Xp, zero)
            bot = jnp.where(lo_half, zero, Xp)
            yd = None
            for hh, rhs in ((h0, top), (h0 + 1, bot)):
                seg = acum[:, hh:hh + 1] - acumT[hh:hh + 1, :]
                decay = jnp.exp(jnp.where(causal, seg, -jnp.inf))
                part = _dot((cb * decay).astype(BF16), rhs)
                yd = part if yd is None else yd + part
            ybuf[:, p * LANES:(p + 1) * LANES] = yd + yoff[:, jj * LANES:(jj + 1) * LANES]

    y = ybuf[...] + xs * dskip_ref[...]
    for g in range(G):
        zg = z_ref[:, g * GW:(g + 1) * GW]
        v = y[:, g * GW:(g + 1) * GW] * _silu(zg.astype(F32))
        v = v * lax.rsqrt(jnp.mean(v * v, -1, keepdims=True) + LN_EPS)
        y_ref[:, g * GW:(g + 1) * GW] = (v * ng_ref[:, g * GW:(g + 1) * GW]).astype(BF16)


def _ssd_branch(xbc, z, small, conv_w, conv_b, dt_bias, A_log, D_skip, norm_g, B, L):
    T = B * L
    Q = SSD_Q
    nc = L // Q
    H = SSD_HEADS
    A = -jnp.exp(A_log.astype(F32))
    expand = jnp.asarray(np.kron(np.eye(H, dtype=np.float32), np.ones((1, SSD_HEADDIM), np.float32)), BF16)
    d_e = jnp.repeat(D_skip.astype(F32), SSD_HEADDIM).reshape(1, SSD_INNER)
    tok = lambda w, j: pl.BlockSpec((Q, w), lambda b, c, j=j: (b * nc + c, j))
    return pl.pallas_call(
        _ssd_kernel,
        out_shape=jax.ShapeDtypeStruct((T, SSD_INNER), BF16),
        grid=(B, nc),
        in_specs=[tok(SSD_XBC, 0), tok(SSD_INNER, 0), tok(LANES, SMALL_DT_BLOCK),
                  _const_spec((SSD_CONV, SSD_XBC)), _const_spec((1, SSD_XBC)),
                  _const_spec((1, H)), _const_spec((H, 1)), _const_spec((1, H)), _const_spec((H, 1)),
                  _const_spec((1, SSD_INNER)), _const_spec((1, SSD_INNER)), _const_spec((H, SSD_INNER))],
        out_specs=pl.BlockSpec((Q, SSD_INNER), lambda b, c: (b * nc + c, 0)),
        scratch_shapes=[pltpu.VMEM((Q + SUBLANES, SSD_XBC), F32),
                        pltpu.VMEM((SSD_GROUPS, SSD_STATE, SSD_INNER // SSD_GROUPS), F32),
                        pltpu.VMEM((Q, SSD_INNER), F32)],
        compiler_params=_cparams("parallel", "arbitrary"),
    )(xbc, z, small, conv_w, conv_b.reshape(1, -1), dt_bias.reshape(1, H), dt_bias.reshape(H, 1),
      A.reshape(1, H), A.reshape(H, 1), d_e, norm_g.reshape(1, -1), expand)


def _prep_kernel(ckv_ref, kidx_ref, kvg_ref, ig_ref, ib_ref, ckvn_ref, ckvt_ref, ckvt2_ref, kn_ref):
    c = ckv_ref[...].astype(F32)
    cn = c * lax.rsqrt(jnp.mean(c * c, -1, keepdims=True) + LN_EPS) * kvg_ref[...]
    ckvn_ref[...] = cn.astype(BF16)
    cnt = cn.T.astype(BF16)
    ckvt2_ref[0] = cnt
    ckvt_ref[0] = cnt[:, :ATT_TK]
    ckvt_ref[1] = cnt[:, ATT_TK:]
    k = kidx_ref[...]
    lane = lax.broadcasted_iota(jnp.int32, k.shape, 1)
    live = lane < IDX_HEADDIM
    mu = jnp.sum(k, -1, keepdims=True) * (1.0 / IDX_HEADDIM)
    kc = jnp.where(live, k - mu, 0.0)
    var = jnp.sum(kc * kc, -1, keepdims=True) * (1.0 / IDX_HEADDIM)
    kn = jnp.where(live, kc * lax.rsqrt(var + LN_EPS) * ig_ref[...] + ib_ref[...], 0.0)
    kn_ref[...] = kn.astype(BF16)


def _dsa_prep(qc, small, kv_g, idx_g, idx_b):
    T = qc.shape[0]
    tm = 2 * ATT_TK
    pad = lambda v: jnp.pad(v.astype(F32), (0, LANES - IDX_HEADDIM)).reshape(1, LANES)
    return pl.pallas_call(
        _prep_kernel,
        out_shape=(jax.ShapeDtypeStruct((T, ATT_LATENT), BF16),
                   jax.ShapeDtypeStruct((T // ATT_TK, ATT_LATENT, ATT_TK), BF16),
                   jax.ShapeDtypeStruct((T // tm, ATT_LATENT, tm), BF16),
                   jax.ShapeDtypeStruct((T, LANES), BF16)),
        grid=(T // tm,),
        in_specs=[pl.BlockSpec((tm, ATT_LATENT), lambda i: (i, 0)),
                  pl.BlockSpec((tm, LANES), lambda i: (i, SMALL_KIDX_BLOCK)),
                  _const_spec((1, ATT_LATENT)), _const_spec((1, LANES)), _const_spec((1, LANES))],
        out_specs=(pl.BlockSpec((tm, ATT_LATENT), lambda i: (i, 0)),
                   pl.BlockSpec((2, ATT_LATENT, ATT_TK), lambda i: (i, 0, 0)),
                   pl.BlockSpec((1, ATT_LATENT, tm), lambda i: (i, 0, 0)),
                   pl.BlockSpec((tm, LANES), lambda i: (i, 0))),
        compiler_params=_cparams("parallel"),
    )(qc, small, kv_g.reshape(1, -1), pad(idx_g), pad(idx_b))


def _attn_kernel(q_ref, qidx_ref, widx_ref, ckv_ref, ckvt_ref, ckvt2_ref, kidx_ref, brow_ref, wuk_ref, wuvt_ref,
                 o_ref, bias_ref, keys, khi, klo, qt, qit, qlat, m_scr, l_scr, acc, ot, *, topk):
    i = pl.program_id(1)
    QB, TK, NH, C, DH, DI = ATT_QB, ATT_TK, ATT_HEADS, ATT_LATENT, ATT_HEADDIM, IDX_HEADDIM
    n_tiles = i + 1

    @pl.when(i == 0)
    def _():
        period = brow_ref.shape[1]
        for h in range(NH):
            base = jnp.broadcast_to(brow_ref[h:h + 1, :], (2 * TK, period))
            bias_ref[h] = pltpu.roll(base, 0, 1, stride=1, stride_axis=0)[:, :QB]

    qt[...] = q_ref[...].astype(F32).T.astype(BF16)
    for h in range(NH):
        ql = _dot(wuk_ref[h], qt[h * DH:(h + 1) * DH, :])
        qlat[h] = (ql * (DH ** -0.5 * LOG2E)).astype(BF16)

    qit[0:IDX_HEADS * DI, :] = qidx_ref[...].T.astype(BF16)
    qit[IDX_HEADS * DI:, :] = jnp.zeros((LANES - DI, QB), BF16)
    wt = widx_ref[...].T[:IDX_HEADS, :] * (IDX_HEADS ** -0.5 * DI ** -0.5)
    qchunk = (i * QB + lax.broadcasted_iota(jnp.int32, (TK, QB), 1)) // CHUNK
    kin = lax.broadcasted_iota(jnp.int32, (TK, QB), 0)

    def score_tile(j, diagonal):
        kt = kidx_ref[pl.ds(pl.multiple_of(j * TK, TK), TK), :]
        s = jnp.zeros((TK, QB), F32)
        for h in range(IDX_HEADS):
            sh = _dot(kt, qit[h * DI:h * DI + LANES, :])
            s = s + wt[h:h + 1, :] * jnp.maximum(sh, 0.0)
        bits = pltpu.bitcast(s, jnp.int32)
        skey = bits ^ ((bits >> 31) & 0x7FFFFFFF)
        if diagonal:
            adm = ((j * TK + kin) // CHUNK) <= qchunk
            skey = jnp.where(adm, skey, INT_MIN)
        keys[j] = skey
        khi[j] = (skey >> 16).astype(jnp.int16)
        klo[j] = ((skey & 0xFFFF) - HALF16).astype(jnp.int16)

    def score_earlier(j, _):
        score_tile(j, False)
        return 0

    lax.fori_loop(0, i, score_earlier, 0)
    score_tile(i, True)

    PK = 2 * SUBLANES

    @pl.when(n_tiles % 2 == 1)
    def _():
        filler = jnp.full((TK, QB), -HALF16, jnp.int16)
        khi[n_tiles] = filler
        klo[n_tiles] = filler

    def count16(ref, test):
        def body(p, cnt):
            for j in (2 * p, 2 * p + 1):
                hit = jnp.where(test(ref[j]), jnp.int16(1), jnp.int16(0))
                for r in range(TK // PK):
                    cnt = cnt + hit[r * PK:(r + 1) * PK, :]
            return cnt
        cnt = lax.fori_loop(0, (n_tiles + 1) // 2, body, jnp.zeros((PK, QB), jnp.int16))
        return jnp.sum(cnt.astype(jnp.int32), axis=0, keepdims=True)

    def search16(ref, base_count, count_at_zero):
        def bit_step(b, carry):
            t, ct = carry
            cand = t + jnp.left_shift(jnp.int32(1), 15 - b)
            c16 = (cand - HALF16).astype(jnp.int16)
            c = base_count + count16(ref, lambda v: v >= c16)
            ok = c >= topk
            return jnp.where(ok, cand, t), jnp.where(ok, c, ct)
        return lax.fori_loop(0, 16, bit_step, (jnp.zeros((1, QB), jnp.int32), count_at_zero))

    zero = jnp.zeros((1, QB), jnp.int32)
    thi, n_ge_hi = search16(khi, zero, zero + n_tiles * TK)
    thi16 = (thi - HALF16).astype(jnp.int16)
    above = count16(khi, lambda v: v > thi16)

    def mark_equal(j, _):
        klo[j] = jnp.where(khi[j] == thi16, klo[j], jnp.int16(-HALF16))
        return 0

    lax.fori_loop(0, n_tiles, mark_equal, 0)
    tlo, n_ge = search16(klo, above, n_ge_hi)
    tau = jnp.left_shift(thi - HALF16, 16) | tlo
    tau = jnp.maximum(tau, INT_MIN + 1)

    def count32(test):
        def body(j, cnt):
            hit = jnp.where(test(keys[j]), 1, 0)
            return cnt + jnp.sum(hit.reshape(TK // SUBLANES, SUBLANES, QB), axis=0)
        cnt = lax.fori_loop(0, n_tiles, body, jnp.zeros((SUBLANES, QB), jnp.int32))
        return jnp.sum(cnt, axis=0, keepdims=True)

    surplus = jnp.max(n_ge) > topk
    m_scr[...] = jnp.full(m_scr.shape, NEG_BIG, F32)
    l_scr[...] = jnp.zeros(l_scr.shape, F32)
    acc[...] = jnp.zeros(acc.shape, F32)

    @pl.when(surplus)
    def _():
        need = (topk - count32(lambda v: v > tau)).astype(F32)
        r = lax.broadcasted_iota(jnp.int32, (TK, TK), 0)
        c = lax.broadcasted_iota(jnp.int32, (TK, TK), 1)
        tril = jnp.where(r >= c, 1.0, 0.0).astype(BF16)

        def demote(j, seen):
            kj = keys[j]
            tie = kj == tau
            rank = _dot(tril, jnp.where(tie, 1.0, 0.0).astype(BF16)) + seen
            keys[j] = jnp.where(tie & (rank > need), INT_MIN, kj)
            return rank[TK - 1:TK, :]

        lax.fori_loop(0, n_tiles, demote, jnp.zeros((1, QB), F32))

    def attend(kv, kvts, key_tiles, bias_rows):
        maskb = jnp.where(jnp.concatenate(key_tiles, axis=0) >= tau, 0.0, NEG_BIG)
        for h in range(NH):
            s = _dot(kv, qlat[h]) + maskb
            if bias_rows is not None:
                s = s + bias_ref[h, bias_rows, :]
            m_old = m_scr[h]
            m_new = jnp.maximum(m_old, jnp.max(s, axis=0, keepdims=True))
            alpha = jnp.exp2(m_old - m_new)
            pr = jnp.exp2(s - m_new)
            l_scr[h] = alpha * l_scr[h] + jnp.sum(pr, axis=0, keepdims=True)
            pb = pr.astype(BF16)
            pv, row = None, 0
            for kvt in kvts:
                part = _dot(kvt, pb[row:row + kvt.shape[1], :])
                pv = part if pv is None else pv + part
                row += kvt.shape[1]
            acc[h] = alpha * acc[h] + pv
            m_scr[h] = m_new

    def kv_rows(first_tile, n):
        return ckv_ref[pl.ds(pl.multiple_of(first_tile * TK, TK), n * TK), :]

    def far_pair(jp, _):
        attend(kv_rows(2 * jp, 2), [ckvt2_ref[jp]], [keys[2 * jp], keys[2 * jp + 1]], None)
        return 0

    n_far = jnp.maximum(i - 1, 0)
    lax.fori_loop(0, n_far // 2, far_pair, 0)

    @pl.when(n_far % 2 == 1)
    def _():
        attend(kv_rows(n_far - 1, 1), [ckvt_ref[n_far - 1]], [keys[n_far - 1]], None)

    @pl.when(i > 0)
    def _():
        attend(kv_rows(i - 1, 2), [ckvt_ref[i - 1], ckvt_ref[i]], [keys[i - 1], keys[i]], slice(0, 2 * TK))

    @pl.when(i == 0)
    def _():
        attend(kv_rows(0, 1), [ckvt_ref[0]], [keys[0]], slice(TK, 2 * TK))

    for h in range(NH):
        ol = (acc[h] / l_scr[h]).astype(BF16)
        ot[h * DH:(h + 1) * DH, :] = _dot(wuvt_ref[h], ol)
    o_ref[...] = ot[...].T.astype(o_ref.dtype)


def _t5_bucket_np(rel):
    nb = REL_BUCKETS // 2
    max_exact = nb // 2
    n = np.abs(rel).astype(np.int64)
    nn = np.maximum(n, 1)
    sq = nn * nn
    log2_sq = np.floor(np.log2(sq.astype(np.float64))).astype(np.int64)
    log2_sq = np.where(2 ** (log2_sq + 1) <= sq, log2_sq + 1, log2_sq)
    log2_sq = np.where(2 ** log2_sq > sq, log2_sq - 1, log2_sq)
    large = np.minimum(max_exact + (log2_sq - 6), nb - 1)
    return np.where(rel > 0, nb, 0) + np.where(n < max_exact, n, large)


def _dsa_attention(qc, small, ckv_n, ckv_t, ckv_t2, kidx_n, w_uk, w_uv, rel_bias, B, L):
    T = B * L
    QB, TK, NH, C, DH = ATT_QB, ATT_TK, ATT_HEADS, ATT_LATENT, ATT_HEADDIM
    nq = L // QB
    topk = min(TOPK_MAX, L // 4)
    n_rel = 2 * TK + QB - 1
    period = n_rel + 1
    rel_of = np.arange(n_rel) - (TK + QB - 1)
    far_bucket = int(_t5_bucket_np(np.array([-(TK + 1)]))[0])
    table = (rel_bias.astype(F32) - rel_bias[far_bucket].astype(F32)[None, :]) * LOG2E
    v = jnp.take(table, jnp.asarray(_t5_bucket_np(rel_of)), axis=0).T
    u = jnp.pad(v, ((0, 0), (0, 1)))[:, (QB - 1 - np.arange(period)) % period]
    wuk = w_uk.astype(BF16)
    wuvt = jnp.transpose(w_uv, (0, 2, 1)).astype(BF16)

    return pl.pallas_call(
        functools.partial(_attn_kernel, topk=topk),
        out_shape=jax.ShapeDtypeStruct((T, NH * DH), BF16),
        grid=(B, nq),
        in_specs=[
            pl.BlockSpec((QB, NH * DH), lambda b, i: (b * nq + i, 0)),
            pl.BlockSpec((QB, IDX_HEADS * IDX_HEADDIM), lambda b, i: (b * nq + i, 0)),
            pl.BlockSpec((QB, LANES), lambda b, i: (b * nq + i, SMALL_WIDX_BLOCK)),
            pl.BlockSpec((L, C), lambda b, i: (b, 0)),
            pl.BlockSpec((L // TK, C, TK), lambda b, i: (b, 0, 0)),
            pl.BlockSpec((L // (2 * TK), C, 2 * TK), lambda b, i: (b, 0, 0)),
            pl.BlockSpec((L, LANES), lambda b, i: (b, 0)),
            _const_spec((NH, period)),
            _const_spec((NH, C, DH)),
            _const_spec((NH, DH, C)),
        ],
        out_specs=pl.BlockSpec((QB, NH * DH), lambda b, i: (b * nq + i, 0)),
        scratch_shapes=[pltpu.VMEM((NH, 2 * TK, QB), F32),
                        pltpu.VMEM((L // TK, TK, QB), jnp.int32),
                        pltpu.VMEM((L // TK, TK, QB), jnp.int16),
                        pltpu.VMEM((L // TK, TK, QB), jnp.int16),
                        pltpu.VMEM((NH * DH, QB), BF16),
                        pltpu.VMEM((IDX_HEADS * IDX_HEADDIM + LANES - IDX_HEADDIM, QB), BF16),
                        pltpu.VMEM((NH, C, QB), BF16),
                        pltpu.VMEM((NH, 1, QB), F32),
                        pltpu.VMEM((NH, 1, QB), F32),
                        pltpu.VMEM((NH, C, QB), F32),
                        pltpu.VMEM((NH * DH, QB), F32)],
        compiler_params=_cparams("parallel", "arbitrary"),
    )(qc, small, small, ckv_n, ckv_t, ckv_t2, kidx_n, u, wuk, wuvt)


def _merge_kernel(ys_ref, ya_ref, gs_ref, ga_ref, h_ref, wbs_ref, wba_ref, wo_ref, g_ref, b_ref,
                  hf_ref, hb_ref):
    sig = lambda v: 1.0 / (1.0 + jnp.exp(-v.astype(F32)))
    m = sig(gs_ref[...]) * _dot(ys_ref[...], wbs_ref[...]) + sig(ga_ref[...]) * _dot(ya_ref[...], wba_ref[...])
    mix = _dot(m.astype(BF16), wo_ref[...])
    h1 = _ln_rows(DN_ALPHA * h_ref[...] + mix, g_ref[...], b_ref[...])
    hf_ref[...] = h1
    hb_ref[...] = h1.astype(BF16)


def _merge(y_ssd, y_att, g_ssd, g_att, h, w_br_ssd, w_br_att, w_out, g, b, tm=512):
    T, D = h.shape
    row = lambda w, j=0: pl.BlockSpec((tm, w), lambda i, j=j: (i, j))
    return pl.pallas_call(
        _merge_kernel,
        out_shape=(jax.ShapeDtypeStruct((T, D), F32), jax.ShapeDtypeStruct((T, D), BF16)),
        grid=(T // tm,),
        in_specs=[row(SSD_INNER), row(ATT_HEADS * ATT_HEADDIM), row(D), row(D), row(D),
                  _const_spec(w_br_ssd.shape), _const_spec(w_br_att.shape), _const_spec(w_out.shape),
                  _const_spec((1, D)), _const_spec((1, D))],
        out_specs=(row(D), row(D)),
        compiler_params=_cparams("parallel"),
    )(y_ssd, y_att, g_ssd, g_att, h, w_br_ssd, w_br_att, w_out, g.reshape(1, D), b.reshape(1, D))


FFN_COLS = 1408


def _ffn_kernel(hb_ref, h_ref, wup_ref, cw_ref, cb_ref, wd_ref, lg_ref, lb_ref, o_ref, upad, tail):
    c = pl.program_id(1)
    tm = hb_ref.shape[0]
    HALO = SUBLANES

    @pl.when(c == 0)
    def _():
        tail[...] = jnp.zeros(tail.shape, F32)

    hb = hb_ref[...]
    f = None
    for cc in range(D_FF // FFN_COLS):
        halves = []
        for half in range(2):
            cols = slice(half * D_FF + cc * FFN_COLS, half * D_FF + (cc + 1) * FFN_COLS)
            upad[0:HALO, :] = tail[:, cols]
            upad[HALO:HALO + tm, :] = _dot(hb, wup_ref[:, cols])
            tail[:, cols] = upad[tm:tm + HALO, :]
            u = cb_ref[:, cols]
            for k in range(FFN_CONV):
                u = u + cw_ref[k:k + 1, cols] * upad[pl.ds(HALO - (FFN_CONV - 1) + k, tm), :]
            halves.append(u)
        g = (_silu(halves[0]) * halves[1]).astype(BF16)
        part = _dot(g, wd_ref[cc * FFN_COLS:(cc + 1) * FFN_COLS, :])
        f = part if f is None else f + part
    o_ref[...] = _ln_rows(DN_ALPHA * h_ref[...] + f, lg_ref[...], lb_ref[...])


def _conv_ffn(h1b, h1, w_up, conv_w, conv_b, w_down, lg, lb, B, L, tm=512):
    T, D = h1.shape
    nt = L // tm
    tok = pl.BlockSpec((tm, D), lambda b, c: (b * nt + c, 0))
    return pl.pallas_call(
        _ffn_kernel,
        out_shape=jax.ShapeDtypeStruct((T, D), F32),
        grid=(B, nt),
        in_specs=[tok, tok, _const_spec(w_up.shape), _const_spec(conv_w.shape), _const_spec((1, 2 * D_FF)),
                  _const_spec(w_down.shape), _const_spec((1, D)), _const_spec((1, D))],
        out_specs=tok,
        scratch_shapes=[pltpu.VMEM((tm + SUBLANES, FFN_COLS), F32),
                        pltpu.VMEM((SUBLANES, 2 * D_FF), F32)],
        compiler_params=_cparams("parallel", "arbitrary"),
    )(h1b, h1, w_up, conv_w, conv_b.reshape(1, -1), w_down, lg.reshape(1, D), lb.reshape(1, D))


def _split_w_in(w_in):
    sizes = (SSD_INNER, SSD_XBC, SSD_HEADS, ATT_HEADS * ATT_HEADDIM, ATT_LATENT,
             IDX_HEADS * IDX_HEADDIM, IDX_HEADDIM, IDX_HEADS, D_MODEL, D_MODEL)
    pts = np.cumsum(sizes)[:-1].tolist()
    return jnp.split(w_in, pts, axis=1)


def _pad_cols(w, width):
    return jnp.pad(w, ((0, 0), (0, width - w.shape[1])))


def kernel(x, ln_in_g, ln_in_b, w_in, ssd_conv_w, ssd_conv_b, ssd_dt_bias, ssd_A_log, ssd_D, ssd_norm_g, att_kv_norm_g, att_w_uk, att_w_uv, idx_k_norm_g, idx_k_norm_b, rel_bias, w_br_ssd, w_br_att, w_out, ln1_g, ln1_b, ffn_w_up, ffn_conv_w, ffn_conv_b, ffn_w_down, ln2_g, ln2_b):
    B, L, D = x.shape
    T = B * L
    l = 0
    wz, wxbc, wdt, wq, wckv, wqi, wki, wwi, wgs, wga = _split_w_in(w_in[l])
    w_small = jnp.concatenate([wqi, _pad_cols(wki, LANES), _pad_cols(wwi, LANES), _pad_cols(wdt, LANES)],
                              axis=1)
    weights = [w.astype(BF16) for w in (wxbc, wz, wq, wckv, wgs, wga, w_small)]
    h, xbc, z, q, ckv, g_ssd, g_att, small = _ln_inproj(
        x.reshape(T, D), ln_in_g, ln_in_b, weights, [BF16] * 6 + [F32])

    y_ssd = _ssd_branch(xbc, z, small, ssd_conv_w[l], ssd_conv_b[l], ssd_dt_bias[l], ssd_A_log[l], ssd_D[l],
                        ssd_norm_g[l], B, L)
    ckv_n, ckv_t, ckv_t2, kidx_n = _dsa_prep(ckv, small, att_kv_norm_g[l], idx_k_norm_g[l], idx_k_norm_b[l])
    y_att = _dsa_attention(q, small, ckv_n, ckv_t, ckv_t2, kidx_n, att_w_uk[l], att_w_uv[l], rel_bias, B, L)

    h1, h1b = _merge(y_ssd, y_att, g_ssd, g_att, h, w_br_ssd[l].astype(BF16), w_br_att[l].astype(BF16),
                     w_out[l].astype(BF16), ln1_g[l], ln1_b[l])
    out = _conv_ffn(h1b, h1, ffn_w_up[l].astype(BF16), ffn_conv_w[l], ffn_conv_b[l],
                    ffn_w_down[l].astype(BF16), ln2_g[l], ln2_b[l], B, L)
    return out.reshape(B, L, D).astype(x.dtype)
```
